```python
import math
import jax
import jax.numpy as jnp
from jax import lax
import numpy as np

D_MODEL = 2048
BATCH = 2
SEQ = 4096
DEPTH = 1
DEC_BATCH = 8
DEC_SEQ = 1
PAST_LEN = 16384
PAGE_SIZE = 128

N_HEADS = D_MODEL // 256
HEAD_DIM = 128
NSA_WIDTH = N_HEADS * HEAD_DIM
KV_GROUPS = 2
HEADS_PER_GROUP = N_HEADS // KV_GROUPS
KV_WIDTH = KV_GROUPS * HEAD_DIM
CMP_STRIDE = 16
CMP_BLOCK = 2 * CMP_STRIDE
SEL_BLOCK = 64
TOP_N = 16
WINDOW = 512
Q_BLOCK = 128
SSM_CH = D_MODEL - NSA_WIDTH
SSM_GROUP_CH = 16
N_SSM_GROUPS = SSM_CH // SSM_GROUP_CH
SSM_STATE = 64
IN_COLS = NSA_WIDTH + 6 * KV_WIDTH + 3 * N_HEADS + SSM_CH
N_EXPERT_GROUPS = 4
EXPERTS_PER_GROUP = 8
N_EXPERTS = N_EXPERT_GROUPS * EXPERTS_PER_GROUP
TOP_K_INNER = 2
EXPERT_HIDDEN = D_MODEL // 4
PLE_DIM = 256
RMS_EPS = 1e-6
NEG_INF = -1e30
FORCE_SCORE = 1e4
TINY = 1e-30

kernel_name = "nsa_s5_parallel_heads_hmoe_step"


def _rmsnorm(x, g):
    xf = x.astype(jnp.float32)
    y = xf * lax.rsqrt(jnp.mean(xf * xf, axis=-1, keepdims=True) + RMS_EPS)
    return (y * g.astype(jnp.float32)).astype(x.dtype)


def _alibi_slopes():
    return jnp.exp2(-8.0 * (jnp.arange(N_HEADS, dtype=jnp.float32) + 1.0) / N_HEADS)


def _masked_softmax(s, mask):
    s = jnp.where(mask, s, NEG_INF)
    m = jnp.max(s, axis=-1, keepdims=True)
    p = jnp.where(mask, jnp.exp(s - m), 0.0)
    return p / jnp.maximum(jnp.sum(p, axis=-1, keepdims=True), TINY)


def _split(z):
    b, t = z.shape[:2]
    sizes = [NSA_WIDTH] + [KV_WIDTH] * 6 + [3 * N_HEADS, SSM_CH]
    cuts = np.cumsum(sizes)[:-1].tolist()
    q, kc, vc, ks, vs, kw, vw, gt, u = jnp.split(z, cuts, axis=-1)
    kv = lambda a: a.reshape(b, t, KV_GROUPS, HEAD_DIM)
    gates = jax.nn.sigmoid(gt.astype(jnp.float32)).reshape(b, t, N_HEADS, 3)
    return q.reshape(b, t, N_HEADS, HEAD_DIM), kv(kc), kv(vc), kv(ks), kv(vs), kv(kw), kv(vw), gates, u


def _compress(k, pe, w1, w2):
    k = k.astype(jnp.float32)
    pe = pe.astype(jnp.float32)
    w1 = w1.astype(jnp.float32)
    b, l = k.shape[:2]
    l16 = -(-l // CMP_STRIDE) * CMP_STRIDE
    k = jnp.pad(k, ((0, 0), (0, l16 - l), (0, 0), (0, 0)))
    ch = k.reshape(b, l16 // CMP_STRIDE, CMP_STRIDE, KV_GROUPS, HEAD_DIM)
    lo = jnp.einsum('bnjgd,jde->bnge', ch + pe[:CMP_STRIDE, None, :], w1[:CMP_STRIDE])
    hi = jnp.einsum('bnjgd,jde->bnge', ch + pe[CMP_STRIDE:, None, :], w1[CMP_STRIDE:])
    hid = jax.nn.gelu(lo[:, :-1] + hi[:, 1:])
    return jnp.einsum('bnge,ef->bngf', hid, w2.astype(jnp.float32))


def _to_blocks(k):
    b, l = k.shape[:2]
    lp = -(-l // SEL_BLOCK) * SEL_BLOCK
    k = jnp.pad(k.astype(jnp.float32), ((0, 0), (0, lp - l), (0, 0), (0, 0)))
    return k.reshape(b, lp // SEL_BLOCK, SEL_BLOCK, KV_GROUPS, HEAD_DIM).transpose(0, 3, 1, 2, 4)


def _nsa_keys(kc, vc, ks, vs, pe_k, w1_k, w2_k, pe_v, w1_v, w2_v):
    ck = _compress(kc, pe_k, w1_k, w2_k)
    cv = _compress(vc, pe_v, w1_v, w2_v)
    cend = CMP_STRIDE * jnp.arange(ck.shape[1], dtype=jnp.int32) + (CMP_BLOCK - 1)
    return ck, cv, cend, _to_blocks(ks), _to_blocks(vs)


def _group_queries(q):
    b, t = q.shape[:2]
    return q.astype(jnp.float32).reshape(b, t, KV_GROUPS, HEADS_PER_GROUP, HEAD_DIM) * (HEAD_DIM ** -0.5)


def _cmp_sel_attend(qg, qpos, ck, cv, cend, ksb, vsb, sl):
    b, tq = qg.shape[:2]
    dist_c = qpos[:, None] - cend[None, :]
    s_c = jnp.einsum('bqgrd,bcgd->bgrqc', qg, ck) - sl * dist_c.astype(jnp.float32)
    p_c = _masked_softmax(s_c, dist_c >= 0)
    o_c = jnp.einsum('bgrqc,bcgd->bqgrd', p_c, cv)
    n_cmp = ck.shape[1]
    n_blk = ksb.shape[2]
    ratio = SEL_BLOCK // CMP_STRIDE
    imp = jnp.pad(p_c.sum(axis=2), ((0, 0), (0, 0), (0, 0), (1, ratio * n_blk - n_cmp)))
    imp = imp[..., :ratio * n_blk].reshape(b, KV_GROUPS, tq, n_blk, ratio).sum(-1) + imp[..., ratio::ratio]
    blk = jnp.arange(n_blk, dtype=jnp.int32)[None, :]
    cur = (qpos // SEL_BLOCK)[:, None]
    valid = blk <= cur
    forced = (blk == 0) | (blk == cur) | (blk == cur - 1)
    score = jnp.where(forced & valid, FORCE_SCORE, jnp.where(valid, imp, -FORCE_SCORE))
    n_pick = min(TOP_N, n_blk)
    _, idx = lax.top_k(score, n_pick)
    bi = jnp.arange(b)[:, None, None, None]
    gi = jnp.arange(KV_GROUPS)[None, :, None, None]
    n_keys = n_pick * SEL_BLOCK
    k_sel = ksb[bi, gi, idx].reshape(b, KV_GROUPS, tq, n_keys, HEAD_DIM)
    v_sel = vsb[bi, gi, idx].reshape(b, KV_GROUPS, tq, n_keys, HEAD_DIM)
    kpos = (idx[..., None] * SEL_BLOCK + jnp.arange(SEL_BLOCK, dtype=jnp.int32)).reshape(b, KV_GROUPS, tq, n_keys)
    dist_s = (qpos[:, None] - kpos)[:, :, None]
    s_s = jnp.einsum('bqgrd,bgqkd->bgrqk', qg, k_sel) - sl * dist_s.astype(jnp.float32)
    p_s = _masked_softmax(s_s, dist_s >= 0)
    o_s = jnp.einsum('bgrqk,bgqkd->bqgrd', p_s, v_sel)
    return o_c, o_s


def _window_attend(qg, qpos, kw, vw, kpos, sl):
    dist = qpos[:, None] - kpos[None, :]
    mask = (dist >= 0) & (dist < WINDOW) & (kpos >= 0)[None, :]
    s = jnp.einsum('bqgrd,bkgd->bgrqk', qg, kw) - sl * dist.astype(jnp.float32)
    p = _masked_softmax(s, mask)
    return jnp.einsum('bgrqk,bkgd->bqgrd', p, vw)


def _nsa_merge(gates, o_c, o_s, o_w):
    b, tq = gates.shape[:2]
    g = gates.reshape(b, tq, KV_GROUPS, HEADS_PER_GROUP, 3)
    o = g[..., 0:1] * o_c + g[..., 1:2] * o_s + g[..., 2:3] * o_w
    return o.reshape(b, tq, NSA_WIDTH)


def _nsa_prompt(q, kc, vc, ks, vs, kw, vw, gates, pe_k, w1_k, w2_k, pe_v, w1_v, w2_v, slopes):
    b, s = q.shape[:2]
    ck, cv, cend, ksb, vsb = _nsa_keys(kc, vc, ks, vs, pe_k, w1_k, w2_k, pe_v, w1_v, w2_v)
    qg = _group_queries(q)
    sl = slopes.reshape(KV_GROUPS, HEADS_PER_GROUP)[None, :, :, None, None]
    pad = ((0, 0), (WINDOW, 0), (0, 0), (0, 0))
    kw_pad = jnp.pad(kw.astype(jnp.float32), pad)
    vw_pad = jnp.pad(vw.astype(jnp.float32), pad)
    band = WINDOW + Q_BLOCK

    def one_block(n):
        start = n * Q_BLOCK
        qb = lax.dynamic_slice_in_dim(qg, start, Q_BLOCK, axis=1)
        gb = lax.dynamic_slice_in_dim(gates, start, Q_BLOCK, axis=1)
        qpos = start + jnp.arange(Q_BLOCK, dtype=jnp.int32)
        o_c, o_s = _cmp_sel_attend(qb, qpos, ck, cv, cend, ksb, vsb, sl)
        kb = lax.dynamic_slice_in_dim(kw_pad, start, band, axis=1)
        vb = lax.dynamic_slice_in_dim(vw_pad, start, band, axis=1)
        kpos = start - WINDOW + jnp.arange(band, dtype=jnp.int32)
        o_w = _window_attend(qb, qpos, kb, vb, kpos, sl)
        return _nsa_merge(gb, o_c, o_s, o_w)

    out = lax.map(one_block, jnp.arange(s // Q_BLOCK, dtype=jnp.int32))
    return out.transpose(1, 0, 2, 3).reshape(b, s, NSA_WIDTH)


def _nsa_sample(q, qpos, kc_all, vc_all, ks_all, vs_all, kw_all, vw_all, kwpos, gates,
                pe_k, w1_k, w2_k, pe_v, w1_v, w2_v, slopes):
    ck, cv, cend, ksb, vsb = _nsa_keys(kc_all, vc_all, ks_all, vs_all, pe_k, w1_k, w2_k, pe_v, w1_v, w2_v)
    qg = _group_queries(q)
    sl = slopes.reshape(KV_GROUPS, HEADS_PER_GROUP)[None, :, :, None, None]
    o_c, o_s = _cmp_sel_attend(qg, qpos, ck, cv, cend, ksb, vsb, sl)
    o_w = _window_attend(qg, qpos, kw_all.astype(jnp.float32), vw_all.astype(jnp.float32), kwpos, sl)
    return _nsa_merge(gates, o_c, o_s, o_w)


def _gather_pages(pool, page_table):
    g = pool[page_table]
    return g.reshape(g.shape[0], g.shape[1] * g.shape[2], KV_GROUPS, HEAD_DIM)


def _append(past, new):
    return jnp.concatenate([past.astype(jnp.float32), new.astype(jnp.float32)], axis=1)


def _cmul_combine(l, r):
    ar1, ai1, br1, bi1 = l
    ar2, ai2, br2, bi2 = r
    return (ar2 * ar1 - ai2 * ai1, ar2 * ai1 + ai2 * ar1,
            ar2 * br1 - ai2 * bi1 + br2, ar2 * bi1 + ai2 * br1 + bi2)


def _s5_mix(u, h0_re, h0_im, a_re, a_im, log_dt, b_re, b_im, c_re, c_im, d, w_glu, b_glu):
    bsz, t = u.shape[:2]
    f32 = jnp.float32
    uf = u.astype(f32).reshape(bsz, t, N_SSM_GROUPS, SSM_GROUP_CH)
    a_re = a_re.astype(f32)
    a_im = a_im.astype(f32)
    b_re = b_re.astype(f32)
    b_im = b_im.astype(f32)
    dt = jnp.exp(log_dt.astype(f32))[:, None]
    mag = jnp.exp(a_re * dt)
    lb_re = mag * jnp.cos(a_im * dt)
    lb_im = mag * jnp.sin(a_im * dt)
    den = a_re * a_re + a_im * a_im
    nr = lb_re - 1.0
    f_re = (nr * a_re + lb_im * a_im) / den
    f_im = (lb_im * a_re - nr * a_im) / den
    bb_re = f_re[..., None] * b_re - f_im[..., None] * b_im
    bb_im = f_re[..., None] * b_im + f_im[..., None] * b_re
    bu_re = jnp.einsum('btgc,gnc->btgn', uf, bb_re)
    bu_im = jnp.einsum('btgc,gnc->btgn', uf, bb_im)
    h0_re = h0_re.astype(f32)
    h0_im = h0_im.astype(f32)
    bu_re = bu_re.at[:, 0].add(lb_re * h0_re - lb_im * h0_im)
    bu_im = bu_im.at[:, 0].add(lb_re * h0_im + lb_im * h0_re)
    ar = jnp.broadcast_to(lb_re, bu_re.shape)
    ai = jnp.broadcast_to(lb_im, bu_re.shape)
    _, _, h_re, h_im = lax.associative_scan(_cmul_combine, (ar, ai, bu_re, bu_im), axis=1)
    y = (jnp.einsum('gcn,btgn->btgc', c_re.astype(f32), h_re)
         - jnp.einsum('gcn,btgn->btgc', c_im.astype(f32), h_im)
         + d.astype(f32).reshape(N_SSM_GROUPS, SSM_GROUP_CH) * uf)
    v = jax.nn.gelu(y.reshape(bsz, t, SSM_CH))
    out = v * jax.nn.sigmoid(v @ w_glu.astype(f32) + b_glu.astype(f32))
    return out, h_re[:, -1], h_im[:, -1]


def _hier_moe(x, rg_w, rg_b, re_w, re_b, w1, w3, w2):
    shp = x.shape
    xt = x.reshape(-1, D_MODEL)
    xf = xt.astype(jnp.float32)
    n_tok = xt.shape[0]
    tok = jnp.arange(n_tok)
    lg = xf @ rg_w.astype(jnp.float32) + rg_b.astype(jnp.float32)
    pg = jax.nn.softmax(lg, axis=-1)
    grp = jnp.argmax(lg, axis=-1)
    p_grp = pg[tok, grp][:, None]
    le = jnp.einsum('td,gde->tge', xf, re_w.astype(jnp.float32)) + re_b.astype(jnp.float32)
    le = le[tok, grp]
    tv, ti = lax.top_k(le, TOP_K_INNER)
    w = jax.nn.softmax(tv, axis=-1) * p_grp
    eid = grp[:, None] * EXPERTS_PER_GROUP + ti
    gate = jnp.sum(jax.nn.one_hot(eid, N_EXPERTS, dtype=jnp.float32) * w[..., None], axis=1)
    y = jnp.zeros(xf.shape, jnp.float32)
    for e in range(N_EXPERTS):
        hid = jax.nn.silu(xt @ w1[e]) * (xt @ w3[e])
        y = y + gate[:, e:e + 1] * (hid @ w2[e]).astype(jnp.float32)
    return y.astype(x.dtype).reshape(shp)


def _layer_tail(h, o_nsa, o_ssm, p_i, g_nsa, g_ssm, w_out, g_ffn, rg_w, rg_b, re_w, re_b,
                w1, w3, w2, g_ple, w_ple_gate, b_ple_gate, w_ple_proj):
    cat = jnp.concatenate([_rmsnorm(o_nsa, g_nsa), _rmsnorm(o_ssm, g_ssm)], axis=-1).astype(h.dtype)
    h = h + cat @ w_out
    h = h + _hier_moe(_rmsnorm(h, g_ffn), rg_w, rg_b, re_w, re_b, w1, w3, w2)
    emb = (p_i @ w_ple_proj).astype(jnp.float32)
    gate = jax.nn.sigmoid((_rmsnorm(h, g_ple) @ w_ple_gate).astype(jnp.float32) + b_ple_gate.astype(jnp.float32))
    return h + (gate * emb).astype(h.dtype)


def setup_inputs(seed: int = 0) -> dict:
    key = jax.random.key(seed)
    keys = iter(jax.random.split(key, 64))
    f32 = jnp.float32

    def nrm(shape, scale=1.0):
        return scale * jax.random.normal(next(keys), shape, f32)

    def gain(width):
        return 1.0 + nrm((DEPTH, width), 0.01)

    n_pages = PAST_LEN // PAGE_SIZE
    n_used = DEC_BATCH * n_pages
    n_phys = n_used + max(1, n_used // 4)
    win_buf = min(WINDOW, PAST_LEN)
    pool = (DEPTH, n_phys, PAGE_SIZE, KV_GROUPS, HEAD_DIM)
    page_table = jax.random.permutation(next(keys), n_phys)[:n_used].reshape(DEC_BATCH, n_pages).astype(jnp.int32)
    a_im = jnp.pi * jnp.arange(SSM_STATE, dtype=f32)[None, None, :] + nrm((DEPTH, N_SSM_GROUPS, SSM_STATE), 0.01)
    log_dt = jax.random.uniform(next(keys), (DEPTH, N_SSM_GROUPS), f32, math.log(1e-3), math.log(1e-1))
    return {
        "x_prompt": nrm((BATCH, SEQ, D_MODEL)),
        "x_sample": nrm((DEC_BATCH, DEC_SEQ, D_MODEL)),
        "p_prompt": nrm((DEPTH, BATCH, SEQ, PLE_DIM)),
        "p_sample": nrm((DEPTH, DEC_BATCH, DEC_SEQ, PLE_DIM)),
        "cache_cmp_k": nrm(pool),
        "cache_cmp_v": nrm(pool),
        "cache_sel_k": nrm(pool),
        "cache_sel_v": nrm(pool),
        "cache_win_k": nrm((DEPTH, DEC_BATCH, win_buf, KV_GROUPS, HEAD_DIM)),
        "cache_win_v": nrm((DEPTH, DEC_BATCH, win_buf, KV_GROUPS, HEAD_DIM)),
        "state_ssm_re": nrm((DEPTH, DEC_BATCH, N_SSM_GROUPS, SSM_STATE), 0.1),
        "state_ssm_im": nrm((DEPTH, DEC_BATCH, N_SSM_GROUPS, SSM_STATE), 0.1),
        "page_table": page_table,
        "norm_mix": gain(D_MODEL),
        "w_in": nrm((DEPTH, D_MODEL, IN_COLS), D_MODEL ** -0.5),
        "cmp_pe_k": nrm((DEPTH, CMP_BLOCK, HEAD_DIM), 0.02),
        "cmp_w1_k": nrm((DEPTH, CMP_BLOCK, HEAD_DIM, HEAD_DIM), (CMP_BLOCK * HEAD_DIM) ** -0.5),
        "cmp_w2_k": nrm((DEPTH, HEAD_DIM, HEAD_DIM), HEAD_DIM ** -0.5),
        "cmp_pe_v": nrm((DEPTH, CMP_BLOCK, HEAD_DIM), 0.02),
        "cmp_w1_v": nrm((DEPTH, CMP_BLOCK, HEAD_DIM, HEAD_DIM), (CMP_BLOCK * HEAD_DIM) ** -0.5),
        "cmp_w2_v": nrm((DEPTH, HEAD_DIM, HEAD_DIM), HEAD_DIM ** -0.5),
        "ssm_a_re": -0.5 + nrm((DEPTH, N_SSM_GROUPS, SSM_STATE), 0.01),
        "ssm_a_im": a_im,
        "ssm_log_dt": log_dt,
        "ssm_b_re": nrm((DEPTH, N_SSM_GROUPS, SSM_STATE, SSM_GROUP_CH), SSM_GROUP_CH ** -0.5),
        "ssm_b_im": nrm((DEPTH, N_SSM_GROUPS, SSM_STATE, SSM_GROUP_CH), SSM_GROUP_CH ** -0.5),
        "ssm_c_re": nrm((DEPTH, N_SSM_GROUPS, SSM_GROUP_CH, SSM_STATE), SSM_STATE ** -0.5),
        "ssm_c_im": nrm((DEPTH, N_SSM_GROUPS, SSM_GROUP_CH, SSM_STATE), SSM_STATE ** -0.5),
        "ssm_d": nrm((DEPTH, SSM_CH)),
        "w_glu": nrm((DEPTH, SSM_CH, SSM_CH), SSM_CH ** -0.5),
        "b_glu": nrm((DEPTH, SSM_CH), 0.01),
        "norm_nsa_out": gain(NSA_WIDTH),
        "norm_ssm_out": gain(SSM_CH),
        "w_out": nrm((DEPTH, D_MODEL, D_MODEL), D_MODEL ** -0.5),
        "norm_ffn": gain(D_MODEL),
        "router_grp_w": nrm((DEPTH, D_MODEL, N_EXPERT_GROUPS), D_MODEL ** -0.5),
        "router_grp_b": nrm((DEPTH, N_EXPERT_GROUPS), 0.01),
        "router_exp_w": nrm((DEPTH, N_EXPERT_GROUPS, D_MODEL, EXPERTS_PER_GROUP), D_MODEL ** -0.5),
        "router_exp_b": nrm((DEPTH, N_EXPERT_GROUPS, EXPERTS_PER_GROUP), 0.01),
        "exp_w1": nrm((DEPTH, N_EXPERTS, D_MODEL, EXPERT_HIDDEN), D_MODEL ** -0.5),
        "exp_w3": nrm((DEPTH, N_EXPERTS, D_MODEL, EXPERT_HIDDEN), D_MODEL ** -0.5),
        "exp_w2": nrm((DEPTH, N_EXPERTS, EXPERT_HIDDEN, D_MODEL), EXPERT_HIDDEN ** -0.5),
        "norm_ple": gain(D_MODEL),
        "w_ple_gate": nrm((DEPTH, D_MODEL, D_MODEL), D_MODEL ** -0.5),
        "b_ple_gate": nrm((DEPTH, D_MODEL), 0.01),
        "w_ple_proj": nrm((DEPTH, PLE_DIM, D_MODEL), PLE_DIM ** -0.5),
        "norm_final": 1.0 + nrm((D_MODEL,), 0.01),
    }


def reference(x_prompt, x_sample, p_prompt, p_sample, cache_cmp_k, cache_cmp_v, cache_sel_k, cache_sel_v,
              cache_win_k, cache_win_v, state_ssm_re, state_ssm_im, page_table,
              norm_mix, w_in, cmp_pe_k, cmp_w1_k, cmp_w2_k, cmp_pe_v, cmp_w1_v, cmp_w2_v,
              ssm_a_re, ssm_a_im, ssm_log_dt, ssm_b_re, ssm_b_im, ssm_c_re, ssm_c_im, ssm_d, w_glu, b_glu,
              norm_nsa_out, norm_ssm_out, w_out, norm_ffn, router_grp_w, router_grp_b, router_exp_w, router_exp_b,
              exp_w1, exp_w3, exp_w2, norm_ple, w_ple_gate, b_ple_gate, w_ple_proj, norm_final):
    slopes = _alibi_slopes()
    bsz, s = x_prompt.shape[:2]
    t = x_sample.shape[1]
    past_len = page_table.shape[1] * PAGE_SIZE
    wb = cache_win_k.shape[2]
    wp = min(WINDOW, s)
    names = ('ck', 'cv', 'sk', 'sv', 'wk', 'wv', 'hre', 'him')
    st_p = {n: [] for n in names}
    st_s = {n: [] for n in names}
    h_p = x_prompt
    h_s = x_sample
    for i in range(DEPTH):
        cmp_par = (cmp_pe_k[i], cmp_w1_k[i], cmp_w2_k[i], cmp_pe_v[i], cmp_w1_v[i], cmp_w2_v[i])
        ssm_par = (ssm_a_re[i], ssm_a_im[i], ssm_log_dt[i], ssm_b_re[i], ssm_b_im[i],
                   ssm_c_re[i], ssm_c_im[i], ssm_d[i], w_glu[i], b_glu[i])
        tail_par = (norm_nsa_out[i], norm_ssm_out[i], w_out[i], norm_ffn[i], router_grp_w[i], router_grp_b[i],
                    router_exp_w[i], router_exp_b[i], exp_w1[i], exp_w3[i], exp_w2[i],
                    norm_ple[i], w_ple_gate[i], b_ple_gate[i], w_ple_proj[i])
        q, kc, vc, ks, vs, kw, vw, gates, u = _split(_rmsnorm(h_p, norm_mix[i]) @ w_in[i])
        o_nsa = _nsa_prompt(q, kc, vc, ks, vs, kw, vw, gates, *cmp_par, slopes)
        zero_state = jnp.zeros((bsz, N_SSM_GROUPS, SSM_STATE), jnp.float32)
        o_ssm, hre, him = _s5_mix(u, zero_state, zero_state, *ssm_par)
        h_p = _layer_tail(h_p, o_nsa, o_ssm, p_prompt[i], *tail_par)
        for n, v in zip(names, (kc, vc, ks, vs, kw[:, s - wp:], vw[:, s - wp:], hre, him)):
            st_p[n].append(v)
        q, kc, vc, ks, vs, kw, vw, gates, u = _split(_rmsnorm(h_s, norm_mix[i]) @ w_in[i])
        qpos = past_len + jnp.arange(t, dtype=jnp.int32)
        kc_all = _append(_gather_pages(cache_cmp_k[i], page_table), kc)
        vc_all = _append(_gather_pages(cache_cmp_v[i], page_table), vc)
        ks_all = _append(_gather_pages(cache_sel_k[i], page_table), ks)
        vs_all = _append(_gather_pages(cache_sel_v[i], page_table), vs)
        win_k = jnp.concatenate([cache_win_k[i], kw.astype(cache_win_k.dtype)], axis=1)
        win_v = jnp.concatenate([cache_win_v[i], vw.astype(cache_win_v.dtype)], axis=1)
        kwpos = past_len - wb + jnp.arange(wb + t, dtype=jnp.int32)
        o_nsa = _nsa_sample(q, qpos, kc_all, vc_all, ks_all, vs_all, win_k, win_v, kwpos, gates, *cmp_par, slopes)
        o_ssm, hre, him = _s5_mix(u, state_ssm_re[i], state_ssm_im[i], *ssm_par)
        h_s = _layer_tail(h_s, o_nsa, o_ssm, p_sample[i], *tail_par)
        for n, v in zip(names, (kc, vc, ks, vs, win_k[:, t:], win_v[:, t:], hre, him)):
            st_s[n].append(v)
    y_prompt = _rmsnorm(h_p, norm_final)
    y_sample = _rmsnorm(h_s, norm_final)
    return (y_prompt, y_sample,
            jnp.stack(st_p['ck']), jnp.stack(st_p['cv']), jnp.stack(st_p['sk']), jnp.stack(st_p['sv']),
            jnp.stack(st_p['wk']), jnp.stack(st_p['wv']), jnp.stack(st_p['hre']), jnp.stack(st_p['him']),
            jnp.stack(st_s['ck']), jnp.stack(st_s['cv']), jnp.stack(st_s['sk']), jnp.stack(st_s['sv']),
            jnp.stack(st_s['wk']), jnp.stack(st_s['wv']), jnp.stack(st_s['hre']), jnp.stack(st_s['him']))
```

```python
import functools

import jax
import jax.numpy as jnp
import numpy as np
from jax import lax
from jax.experimental import pallas as pl
from jax.experimental.pallas import tpu as pltpu

F32 = jnp.float32
BF16 = jnp.bfloat16

HEAD_DIM = 128
N_HEADS = 8
KV_GROUPS = 2
HEADS_PER_GROUP = N_HEADS // KV_GROUPS
NSA_WIDTH = N_HEADS * HEAD_DIM
KV_WIDTH = KV_GROUPS * HEAD_DIM
CMP_STRIDE = 16
CMP_BLOCK = 2 * CMP_STRIDE
SEL_BLOCK = 64
TOP_N = 16
WINDOW = 512
Q_BLOCK = 128
SSM_GROUP_CH = 16
SSM_STATE = 64
N_EXPERT_GROUPS = 4
EXPERTS_PER_GROUP = 8
N_EXPERTS = N_EXPERT_GROUPS * EXPERTS_PER_GROUP
RMS_EPS = 1e-6
NEG_INF = -1e30
FORCE_SCORE = 1e4
TINY = 1e-30

LANES = 128
SUBLANES = 8
GATE_COLS = KV_GROUPS * LANES
ROUTER_COLS = LANES
ROUTER_EXP_OFF = N_EXPERT_GROUPS
SEL_KV_TILE = 512
VMEM_LIMIT_BYTES = 56 * 1024 * 1024


def _cparams(sem):
    return pltpu.CompilerParams(dimension_semantics=sem, vmem_limit_bytes=VMEM_LIMIT_BYTES)


def _rms(x, g):
    return x * lax.rsqrt(jnp.mean(x * x, axis=-1, keepdims=True) + RMS_EPS) * g


def _gelu(x):
    return x * (0.5 * (1.0 + jnp.tanh(0.7978845608028654 * (x + 0.044715 * (x * x * x)))))


def _dot(a, b):
    return jnp.dot(a, b, preferred_element_type=F32)


def _dot_nt(a, b):
    return lax.dot_general(a, b, (((1,), (1,)), ((), ())), preferred_element_type=F32)


def _split2(x):
    hi = x.astype(BF16)
    lo = (x - hi.astype(F32)).astype(BF16)
    return hi, lo


def _split3(x):
    hi = x.astype(BF16)
    r = x - hi.astype(F32)
    mid = r.astype(BF16)
    lo = (r - mid.astype(F32)).astype(BF16)
    return hi, mid, lo


def _masked_softmax(s, mask):
    s = jnp.where(mask, s, NEG_INF)
    m = jnp.max(s, axis=-1, keepdims=True)
    p = jnp.where(mask, jnp.exp(s - m), 0.0)
    return p / jnp.maximum(jnp.sum(p, axis=-1, keepdims=True), TINY)


def _row_tile(m, pref):
    return pref if m % pref == 0 else m


Q_END = NSA_WIDTH
KV_END = Q_END + 6 * KV_WIDTH
U_OFF = KV_END


def _inproj_kernel(x_ref, g_ref, w_ref, qbf, kc, vc, ks, vs, kw, vw, ksb, vsb, kwb, vwb, u, gates, *, ssm_ch):
    xn = _rms(x_ref[...], g_ref[...]).astype(BF16)

    def mm(lo, hi):
        return _dot(xn, w_ref[:, lo:hi])

    qbf[...] = (mm(0, Q_END) * (HEAD_DIM ** -0.5)).astype(BF16)
    for i, (f, b) in enumerate(((kc, None), (vc, None), (ks, ksb), (vs, vsb), (kw, kwb), (vw, vwb))):
        z = mm(Q_END + KV_WIDTH * i, Q_END + KV_WIDTH * (i + 1))
        f[...] = z
        if b is not None:
            b[...] = z.astype(BF16)
    u[...] = mm(U_OFF, U_OFF + ssm_ch)
    gates[...] = jax.nn.sigmoid(mm(U_OFF + ssm_ch, U_OFF + ssm_ch + GATE_COLS))


def _prep_w_in(w_in, ssm_ch):
    wq = w_in[:, :KV_END]
    wg = w_in[:, KV_END:KV_END + 3 * N_HEADS]
    wu = w_in[:, KV_END + 3 * N_HEADS:]
    d = w_in.shape[0]
    wg = wg.reshape(d, KV_GROUPS, HEADS_PER_GROUP, 3).transpose(0, 1, 3, 2).reshape(d, KV_GROUPS, 3 * HEADS_PER_GROUP)
    wg = jnp.pad(wg, ((0, 0), (0, 0), (0, LANES - 3 * HEADS_PER_GROUP))).reshape(d, GATE_COLS)
    return jnp.concatenate([wq, wu, wg], axis=1).astype(BF16)


def _inproj(x, g, w, ssm_ch, tm_pref):
    m, d = x.shape
    tm = _row_tile(m, tm_pref)
    ncols = w.shape[1]
    row = lambda c: pl.BlockSpec((tm, c), lambda i: (i, 0))
    f32s = lambda c: jax.ShapeDtypeStruct((m, c), F32)
    bfs = lambda c: jax.ShapeDtypeStruct((m, c), BF16)
    out_shape = ([bfs(NSA_WIDTH)] + [f32s(KV_WIDTH)] * 6 + [bfs(KV_WIDTH)] * 4 + [f32s(ssm_ch), f32s(GATE_COLS)])
    out_specs = ([row(NSA_WIDTH)] + [row(KV_WIDTH)] * 10 + [row(ssm_ch), row(GATE_COLS)])
    return pl.pallas_call(
        functools.partial(_inproj_kernel, ssm_ch=ssm_ch),
        grid=(m // tm,),
        in_specs=[row(d), pl.BlockSpec((1, d), lambda i: (0, 0)),
                  pl.BlockSpec((d, ncols), lambda i: (0, 0), pipeline_mode=pl.Buffered(1))],
        out_specs=out_specs,
        out_shape=out_shape,
        compiler_params=_cparams(("parallel",)),
        name="inproj",
    )(x, g, w)


def _prep_cmp_w1(w1):
    return jnp.concatenate([w1[:CMP_STRIDE], w1[CMP_STRIDE:]], axis=-1).astype(BF16)


def _cmp_prompt_kernel(kc_ref, vc_ref, wk_ref, wv_ref, ok_ref, ov_ref, *, n_ch):
    for src, w, dst in ((kc_ref, wk_ref, ok_ref), (vc_ref, wv_ref, ov_ref)):
        acc = jnp.zeros((n_ch, 2 * HEAD_DIM), F32)
        for j in range(CMP_STRIDE):
            xj = src[pl.ds(j, n_ch, stride=CMP_STRIDE), :]
            acc = acc + _dot(xj.astype(BF16), w[j])
        dst[...] = acc


def _cmp_prompt(kc, vc, wk, wv):
    b, s, _ = kc.shape
    n_ch = s // CMP_STRIDE
    tok = pl.BlockSpec((None, s, HEAD_DIM), lambda i, g: (i, 0, g))
    wsp = pl.BlockSpec((CMP_STRIDE, HEAD_DIM, 2 * HEAD_DIM), lambda i, g: (0, 0, 0))
    osp = pl.BlockSpec((None, None, n_ch, 2 * HEAD_DIM), lambda i, g: (i, g, 0, 0))
    osh = jax.ShapeDtypeStruct((b, KV_GROUPS, n_ch, 2 * HEAD_DIM), F32)
    return pl.pallas_call(
        functools.partial(_cmp_prompt_kernel, n_ch=n_ch),
        grid=(b, KV_GROUPS),
        in_specs=[tok, tok, wsp, wsp],
        out_specs=[osp, osp],
        out_shape=[osh, osh],
        compiler_params=_cparams(("parallel", "parallel")),
        name="cmp_prompt",
    )(kc, vc, wk, wv)


def _cmp_finish_kernel(*refs, nch, has_tail):
    if has_tail:
        lk, lv, kn, vn, pek, pev, w1k, w1v, w2k, w2v, ock, ocv = refs
    else:
        lk, lv, pek, pev, w1k, w1v, w2k, w2v, ock, ocv = refs
        kn = vn = None
    row = lax.broadcasted_iota(jnp.int32, (nch, 1), 0)
    half = CMP_STRIDE * HEAD_DIM
    for l_ref, n_ref, pe_ref, w1_ref, w2_ref, o_ref in ((lk, kn, pek, w1k, w2k, ock), (lv, vn, pev, w1v, w2v, ocv)):
        lohi = l_ref[...]
        lo = lohi[:, :HEAD_DIM]
        hi = lohi[:, HEAD_DIM:]
        pe = jnp.broadcast_to(pe_ref[0:1, :], (SUBLANES, half)).astype(BF16)
        b_lo = _dot(pe, w1_ref[0:half, :].astype(BF16))[0:1]
        pe = jnp.broadcast_to(pe_ref[1:2, :], (SUBLANES, half)).astype(BF16)
        b_hi = _dot(pe, w1_ref[half:2 * half, :].astype(BF16))[0:1]
        hi_next = pltpu.roll(hi, nch - 1, 0)
        if has_tail:
            new = jnp.broadcast_to(n_ref[...], (SUBLANES, HEAD_DIM)).astype(BF16)
            tail = _dot(new, w1_ref[half:half + HEAD_DIM, :].astype(BF16))[0:1]
            hi_next = jnp.where(row == nch - 1, tail, hi_next)
        hid = _gelu(lo + hi_next + (b_lo + b_hi))
        o_ref[...] = _dot(hid.astype(BF16), w2_ref[...].astype(BF16)).astype(BF16)


def _cmp_finish(lk, lv, pek, pev, w1k, w1v, w2k, w2v, kn=None, vn=None):
    b, _, nch, _ = lk.shape
    has_tail = kn is not None
    lsp = pl.BlockSpec((None, None, nch, 2 * HEAD_DIM), lambda i, g: (i, g, 0, 0))
    full = lambda a: pl.BlockSpec(a.shape, lambda i, g: (0,) * a.ndim)
    pek2 = pek.reshape(2, CMP_STRIDE * HEAD_DIM)
    pev2 = pev.reshape(2, CMP_STRIDE * HEAD_DIM)
    w1k2 = w1k.reshape(CMP_BLOCK * HEAD_DIM, HEAD_DIM)
    w1v2 = w1v.reshape(CMP_BLOCK * HEAD_DIM, HEAD_DIM)
    args = [lk, lv]
    specs = [lsp, lsp]
    if has_tail:
        nsp = pl.BlockSpec((None, 1, HEAD_DIM), lambda i, g: (i, 0, g))
        args += [kn, vn]
        specs += [nsp, nsp]
    params = [pek2, pev2, w1k2, w1v2, w2k, w2v]
    args += params
    specs += [full(a) for a in params]
    osp = pl.BlockSpec((None, None, nch, HEAD_DIM), lambda i, g: (i, g, 0, 0))
    osh = jax.ShapeDtypeStruct((b, KV_GROUPS, nch, HEAD_DIM), BF16)
    return pl.pallas_call(
        functools.partial(_cmp_finish_kernel, nch=nch, has_tail=has_tail),
        grid=(b, KV_GROUPS),
        in_specs=specs,
        out_specs=[osp, osp],
        out_shape=[osh, osh],
        compiler_params=_cparams(("parallel", "parallel")),
        name="cmp_finish",
    )(*args)


def _imp_matrix(n_cmp_pad, n_blk_pad):
    ratio = SEL_BLOCK // CMP_STRIDE
    c = np.arange(n_cmp_pad)[:, None]
    j = np.arange(n_blk_pad)[None, :]
    return jnp.asarray(((c >= ratio * j - 1) & (c <= ratio * j + ratio - 1)).astype(np.float32), dtype=BF16)


def _expand_matrix(n_blk_pad, n_keys):
    j = np.arange(n_blk_pad)[:, None]
    k = np.arange(n_keys)[None, :]
    return jnp.asarray((k // SEL_BLOCK == j).astype(np.float32), dtype=BF16)


def _group_slopes(g):
    sg = jnp.where(g == 0, 1.0, 2.0 ** -HEADS_PER_GROUP).astype(F32)
    return [jnp.full((Q_BLOCK, 1), 2.0 ** -(h + 1), F32) * sg for h in range(HEADS_PER_GROUP)]


def _nsa_prompt_kernel(q_ref, ck_ref, cv_ref, ks_ref, vs_ref, kw_ref, vw_ref, gt_ref, a_ref, e_ref, o_ref, sk_ref,
                       *, seq, n_cmp, n_blk, n_pick):
    g = pl.program_id(1)
    n = pl.program_id(2)
    rows = HEADS_PER_GROUP * Q_BLOCK
    qb = q_ref[...]
    q = jnp.concatenate([qb[:, h * HEAD_DIM:(h + 1) * HEAD_DIM] for h in range(HEADS_PER_GROUP)], axis=0)
    qpos1 = n * Q_BLOCK + lax.broadcasted_iota(jnp.int32, (Q_BLOCK, 1), 0)
    qpos = jnp.concatenate([qpos1] * HEADS_PER_GROUP, axis=0)
    slope = jnp.concatenate(_group_slopes(g), axis=0)

    ncp = ck_ref.shape[0]
    cidx = lax.broadcasted_iota(jnp.int32, (1, ncp), 1)
    dist = qpos - (cidx * CMP_STRIDE + (CMP_BLOCK - 1))
    mask = (dist >= 0) & (cidx < n_cmp)
    p = _masked_softmax(_dot_nt(q, ck_ref[...]) - slope * dist.astype(F32), mask)
    o_c = _dot(p.astype(BF16), cv_ref[...])
    p_sum = p[0:Q_BLOCK]
    for h in range(1, HEADS_PER_GROUP):
        p_sum = p_sum + p[h * Q_BLOCK:(h + 1) * Q_BLOCK]

    imp = sum(_dot(t, a_ref[...]) for t in _split3(p_sum))
    blk = lax.broadcasted_iota(jnp.int32, (1, LANES), 1)
    cur = qpos1 // SEL_BLOCK
    valid = blk <= cur
    forced = (blk == 0) | (blk == cur) | (blk == cur - 1)
    score = jnp.where(forced & valid, FORCE_SCORE, jnp.where(valid, imp, -FORCE_SCORE))
    score_t = score.T[:n_blk]
    bidx = lax.broadcasted_iota(jnp.int32, (n_blk, 1), 0)
    rank = jnp.zeros((n_blk, Q_BLOCK), F32)
    for j in range(n_blk):
        r = score_t[j:j + 1, :]
        rank = rank + jnp.where((r > score_t) | ((r == score_t) & (j < bidx)), 1.0, 0.0)
    sel_t = jnp.where(rank < n_pick, 1.0, 0.0)
    if n_blk < LANES:
        sel_t = jnp.concatenate([sel_t, jnp.zeros((LANES - n_blk, Q_BLOCK), F32)], axis=0)
    sk_ref[...] = _dot(sel_t.T.astype(BF16), e_ref[...])

    tk = min(SEL_KV_TILE, seq)
    n_tiles = (n * Q_BLOCK + Q_BLOCK + tk - 1) // tk

    def body(t, carry):
        m, l, acc = carry
        k0 = pl.multiple_of(t * tk, tk)
        kpos = k0 + lax.broadcasted_iota(jnp.int32, (1, tk), 1)
        dist = qpos - kpos
        picked = sk_ref[:, pl.ds(k0, tk)] > 0.5
        mask = (dist >= 0) & jnp.concatenate([picked] * HEADS_PER_GROUP, axis=0)
        s = _dot_nt(q, ks_ref[pl.ds(k0, tk), :]) - slope * dist.astype(F32)
        s = jnp.where(mask, s, NEG_INF)
        m_new = jnp.maximum(m, jnp.max(s, axis=-1, keepdims=True))
        p = jnp.where(mask, jnp.exp(s - m_new), 0.0)
        alpha = jnp.exp(m - m_new)
        l = alpha * l + jnp.sum(p, axis=-1, keepdims=True)
        acc = alpha * acc + _dot(p.astype(BF16), vs_ref[pl.ds(k0, tk), :])
        return m_new, l, acc

    init = (jnp.full((rows, 1), NEG_INF, F32), jnp.zeros((rows, 1), F32), jnp.zeros((rows, HEAD_DIM), F32))
    _, l, acc = lax.fori_loop(0, n_tiles, body, init)
    o_s = acc / jnp.maximum(l, TINY)

    band = min(WINDOW + Q_BLOCK, seq)
    st = pl.multiple_of(jnp.minimum(jnp.maximum(n * Q_BLOCK - WINDOW, 0), seq - band), Q_BLOCK)
    kpos = st + lax.broadcasted_iota(jnp.int32, (1, band), 1)
    dist = qpos - kpos
    mask = (dist >= 0) & (dist < WINDOW)
    p = _masked_softmax(_dot_nt(q, kw_ref[pl.ds(st, band), :]) - slope * dist.astype(F32), mask)
    o_w = _dot(p.astype(BF16), vw_ref[pl.ds(st, band), :])

    gt = gt_ref[...]
    for h in range(HEADS_PER_GROUP):
        sl = slice(h * Q_BLOCK, (h + 1) * Q_BLOCK)
        o_ref[:, h * HEAD_DIM:(h + 1) * HEAD_DIM] = (
            gt[:, h:h + 1] * o_c[sl]
            + gt[:, HEADS_PER_GROUP + h:HEADS_PER_GROUP + h + 1] * o_s[sl]
            + gt[:, 2 * HEADS_PER_GROUP + h:2 * HEADS_PER_GROUP + h + 1] * o_w[sl])


def _nsa_prompt(qbf, ck, cv, ksb, vsb, kwb, vwb, gates):
    b, s, _ = qbf.shape
    ncp = ck.shape[2]
    n_cmp = s // CMP_STRIDE - 1
    n_blk = s // SEL_BLOCK
    assert s % Q_BLOCK == 0 and TOP_N <= n_blk <= LANES
    a = _imp_matrix(ncp, LANES)
    e = _expand_matrix(LANES, s)
    gw = HEADS_PER_GROUP * HEAD_DIM
    qsp = pl.BlockSpec((None, Q_BLOCK, gw), lambda i, g, n: (i, n, g))
    csp = pl.BlockSpec((None, None, ncp, HEAD_DIM), lambda i, g, n: (i, g, 0, 0))
    ksp = pl.BlockSpec((None, s, HEAD_DIM), lambda i, g, n: (i, 0, g))
    gsp = pl.BlockSpec((None, Q_BLOCK, LANES), lambda i, g, n: (i, n, g))
    full = lambda x: pl.BlockSpec(x.shape, lambda i, g, n: (0,) * x.ndim)
    return pl.pallas_call(
        functools.partial(_nsa_prompt_kernel, seq=s, n_cmp=n_cmp, n_blk=n_blk, n_pick=TOP_N),
        grid=(b, KV_GROUPS, s // Q_BLOCK),
        in_specs=[qsp, csp, csp, ksp, ksp, ksp, ksp, gsp, full(a), full(e)],
        out_specs=qsp,
        out_shape=jax.ShapeDtypeStruct((b, s, NSA_WIDTH), F32),
        scratch_shapes=[pltpu.VMEM((Q_BLOCK, s), F32)],
        compiler_params=_cparams(("parallel", "parallel", "arbitrary")),
        name="nsa_prompt",
    )(qbf, ck, cv, ksb, vsb, kwb, vwb, gates, a, e)


SSM_LANE_GROUPS = LANES // SSM_GROUP_CH
SSM_STATE_TILE = SSM_LANE_GROUPS * SSM_STATE


def _prep_s5(a_re, a_im, log_dt, b_re, b_im, c_re, c_im):
    ng = a_re.shape[0]
    nk = ng // SSM_LANE_GROUPS
    dt = jnp.exp(log_dt)[:, None]
    mag = jnp.exp(a_re * dt)
    lb_re = mag * jnp.cos(a_im * dt)
    lb_im = mag * jnp.sin(a_im * dt)
    den = a_re * a_re + a_im * a_im
    nr = lb_re - 1.0
    f_re = (nr * a_re + lb_im * a_im) / den
    f_im = (lb_im * a_re - nr * a_im) / den
    bb_re = f_re[..., None] * b_re - f_im[..., None] * b_im
    bb_im = f_re[..., None] * b_im + f_im[..., None] * b_re
    eye = jnp.eye(SSM_LANE_GROUPS, dtype=F32)

    def in_map(bb):
        t = bb.reshape(nk, SSM_LANE_GROUPS, SSM_STATE, SSM_GROUP_CH)
        t = jnp.einsum('kgnc,gh->kgchn', t, eye)
        return t.reshape(nk, LANES, SSM_STATE_TILE)

    def out_map(c):
        t = c.reshape(nk, SSM_LANE_GROUPS, SSM_GROUP_CH, SSM_STATE)
        t = jnp.einsum('kgcn,gh->kgnhc', t, eye)
        return t.reshape(nk, SSM_STATE_TILE, LANES)

    bb = jnp.concatenate([in_map(bb_re), in_map(bb_im)], axis=-1)
    cm = jnp.concatenate([out_map(c_re), -out_map(c_im)], axis=1)
    bb_hi = bb.astype(BF16)
    bb_lo = (bb - bb_hi.astype(F32)).astype(BF16)
    lam = jnp.stack([lb_re.reshape(-1), lb_im.reshape(-1)])
    return lam, bb_hi, bb_lo, cm.astype(BF16)


def _s5_kernel(u_ref, h0r_ref, h0i_ref, lam_ref, bbh_ref, bbl_ref, cm_ref, d_ref, wglu_ref, bglu_ref,
               o_ref, hr_ref, hi_ref, *scratch, rows, scan, precise):
    nk = bbh_ref.shape[0]
    st = SSM_STATE_TILE
    if scan:
        cr_scr, ci_scr = scratch
        t = pl.program_id(1)

        @pl.when(t == 0)
        def _():
            cr_scr[...] = h0r_ref[...]
            ci_scr[...] = h0i_ref[...]

        row = lax.broadcasted_iota(jnp.int32, (rows, 1), 0)
    u = u_ref[...]
    ys = []
    for k in range(nk):
        uk = u[:, k * LANES:(k + 1) * LANES]
        if precise:
            uh, ul = _split2(uk)
            bu = _dot(uh, bbh_ref[k]) + _dot(ul, bbh_ref[k]) + _dot(uh, bbl_ref[k])
        else:
            bu = _dot(uk.astype(BF16), bbh_ref[k])
        br = bu[:, :st]
        bi = bu[:, st:]
        ks = slice(k * st, (k + 1) * st)
        lr = lam_ref[0:1, ks]
        li = lam_ref[1:2, ks]
        if scan:
            cr = cr_scr[:, ks]
            ci = ci_scr[:, ks]
            first = row == 0
            br = br + jnp.where(first, lr * cr - li * ci, 0.0)
            bi = bi + jnp.where(first, lr * ci + li * cr, 0.0)
            pr, pi = lr, li
            sh = 1
            while sh < rows:
                sr = jnp.where(row >= sh, pltpu.roll(br, sh, 0), 0.0)
                si = jnp.where(row >= sh, pltpu.roll(bi, sh, 0), 0.0)
                br, bi = br + (pr * sr - pi * si), bi + (pr * si + pi * sr)
                pr, pi = pr * pr - pi * pi, 2.0 * (pr * pi)
                sh *= 2
            cr_scr[:, ks] = br[rows - 1:rows]
            ci_scr[:, ks] = bi[rows - 1:rows]
        else:
            h0r = h0r_ref[:, ks]
            h0i = h0i_ref[:, ks]
            br, bi = br + (lr * h0r - li * h0i), bi + (lr * h0i + li * h0r)
            hr_ref[:, ks] = br
            hi_ref[:, ks] = bi
        hcat = jnp.concatenate([br, bi], axis=1).astype(BF16)
        ys.append(_dot(hcat, cm_ref[k]) + d_ref[:, k * LANES:(k + 1) * LANES] * uk)
    v = _gelu(jnp.concatenate(ys, axis=1))
    o_ref[...] = v * jax.nn.sigmoid(_dot(v.astype(BF16), wglu_ref[...]) + bglu_ref[...])
    if scan:
        hr_ref[...] = cr_scr[...]
        hi_ref[...] = ci_scr[...]


def _s5(u, h0r, h0i, prep, d, wglu, bglu, *, scan, precise, tc_pref=256):
    lam, bbh, bbl, cm = prep
    ns = lam.shape[1]
    full = lambda a: pl.BlockSpec(a.shape, lambda *idx: (0,) * a.ndim)
    params = [lam, bbh, bbl, cm, d, wglu, bglu]
    if scan:
        b, t, c = u.shape
        tc = _row_tile(t, tc_pref)
        grid = (b, t // tc)
        usp = pl.BlockSpec((None, tc, c), lambda i, j: (i, j, 0))
        hsp = pl.BlockSpec((None, 1, ns), lambda i, j: (i, 0, 0))
        rows = tc
        scratch = [pltpu.VMEM((1, ns), F32), pltpu.VMEM((1, ns), F32)]
        sem = ("parallel", "arbitrary")
        osh = [jax.ShapeDtypeStruct((b, t, c), F32)] + [jax.ShapeDtypeStruct((b, 1, ns), F32)] * 2
    else:
        b, c = u.shape
        grid = (1,)
        usp = pl.BlockSpec((b, c), lambda i: (0, 0))
        hsp = pl.BlockSpec((b, ns), lambda i: (0, 0))
        rows = b
        scratch = []
        sem = ("arbitrary",)
        osh = [jax.ShapeDtypeStruct((b, c), F32)] + [jax.ShapeDtypeStruct((b, ns), F32)] * 2
    return pl.pallas_call(
        functools.partial(_s5_kernel, rows=rows, scan=scan, precise=precise),
        grid=grid,
        in_specs=[usp, hsp, hsp] + [full(a) for a in params],
        out_specs=[usp, hsp, hsp],
        out_shape=osh,
        scratch_shapes=scratch,
        compiler_params=_cparams(sem),
        name="s5_scan" if scan else "s5_step",
    )(u, h0r, h0i, *params)


def _prep_router(rg_w, rg_b, re_w, re_b):
    d = rg_w.shape[0]
    w = jnp.concatenate([rg_w, re_w.transpose(1, 0, 2).reshape(d, N_EXPERTS)], axis=1)
    w = jnp.pad(w, ((0, 0), (0, ROUTER_COLS - w.shape[1])))
    b = jnp.pad(jnp.concatenate([rg_b, re_b.reshape(-1)]), (0, ROUTER_COLS - N_EXPERT_GROUPS - N_EXPERTS))
    hi = w.astype(BF16)
    lo = (w - hi.astype(F32)).astype(BF16)
    return hi, lo, b.reshape(1, ROUTER_COLS)


def _tail1_kernel(on_ref, os_ref, h_ref, gn_ref, gs_ref, wout_ref, gf_ref, wrh_ref, wrl_ref, br_ref,
                  h1_ref, xn_ref, gate_ref, rid_ref, rw_ref):
    cat = jnp.concatenate([_rms(on_ref[...], gn_ref[...]), _rms(os_ref[...], gs_ref[...])], axis=1)
    h1 = h_ref[...] + _dot(cat.astype(BF16), wout_ref[...])
    h1_ref[...] = h1
    xn = _rms(h1, gf_ref[...])
    xh, xl = _split2(xn)
    xn_ref[...] = xh
    lg = _dot(xh, wrh_ref[...]) + _dot(xl, wrh_ref[...]) + _dot(xh, wrl_ref[...]) + br_ref[...]
    lane_i = lax.broadcasted_iota(jnp.int32, (1, ROUTER_COLS), 1)
    lane = lane_i.astype(F32)
    ninf = -jnp.inf
    big = float(ROUTER_COLS)
    is_grp = lane_i < N_EXPERT_GROUPS
    lgm = jnp.where(is_grp, lg, ninf)
    mx = jnp.max(lgm, axis=-1, keepdims=True)
    grp = jnp.min(jnp.where(lgm == mx, lane, big), axis=-1, keepdims=True)
    p_grp = 1.0 / jnp.sum(jnp.where(is_grp, jnp.exp(lgm - mx), 0.0), axis=-1, keepdims=True)
    first = ROUTER_EXP_OFF + EXPERTS_PER_GROUP * grp
    lem = jnp.where((lane >= first) & (lane < first + EXPERTS_PER_GROUP), lg, ninf)
    v1 = jnp.max(lem, axis=-1, keepdims=True)
    i1 = jnp.min(jnp.where(lem == v1, lane, big), axis=-1, keepdims=True)
    lem2 = jnp.where(lane == i1, ninf, lem)
    v2 = jnp.max(lem2, axis=-1, keepdims=True)
    i2 = jnp.min(jnp.where(lem2 == v2, lane, big), axis=-1, keepdims=True)
    e2 = jnp.exp(v2 - v1)
    w1 = (1.0 / (1.0 + e2)) * p_grp
    w2 = (e2 / (1.0 + e2)) * p_grp
    gate_ref[...] = jnp.where(lane == i1, w1, jnp.where(lane == i2, w2, 0.0))
    rid_ref[...] = jnp.where(lane_i == 0, i1 - ROUTER_EXP_OFF,
                             jnp.where(lane_i == 1, i2 - ROUTER_EXP_OFF, 0.0)).astype(jnp.int32)
    rw_ref[...] = jnp.where(lane_i == 0, w1, jnp.where(lane_i == 1, w2, 0.0))


def _tail1(o_nsa, o_ssm, h, g_nsa, g_ssm, wout, g_ffn, router, tm_pref=256):
    m, d = h.shape
    tm = _row_tile(m, tm_pref)
    wrh, wrl, br = router
    row = lambda c: pl.BlockSpec((tm, c), lambda i: (i, 0))
    full = lambda a: pl.BlockSpec(a.shape, lambda i: (0,) * a.ndim)
    params = [g_nsa, g_ssm, wout, g_ffn, wrh, wrl, br]
    return pl.pallas_call(
        _tail1_kernel,
        grid=(m // tm,),
        in_specs=[row(o_nsa.shape[1]), row(o_ssm.shape[1]), row(d)] + [full(a) for a in params],
        out_specs=[row(d), row(d), row(ROUTER_COLS), row(ROUTER_COLS), row(ROUTER_COLS)],
        out_shape=[jax.ShapeDtypeStruct((m, d), F32), jax.ShapeDtypeStruct((m, d), BF16),
                   jax.ShapeDtypeStruct((m, ROUTER_COLS), F32), jax.ShapeDtypeStruct((m, ROUTER_COLS), jnp.int32),
                   jax.ShapeDtypeStruct((m, ROUTER_COLS), F32)],
        compiler_params=_cparams(("parallel",)),
        name="tail1",
    )(o_nsa, o_ssm, h, *params)


def _moe_dense_kernel(x_ref, gate_ref, w1_ref, w3_ref, w2_ref, o_ref):
    e = pl.program_id(1)

    @pl.when(e == 0)
    def _():
        o_ref[...] = jnp.zeros_like(o_ref)

    x = x_ref[...]
    a = _dot(x, w1_ref[...])
    hid = (a * jax.nn.sigmoid(a)) * _dot(x, w3_ref[...])
    y = _dot(hid.astype(BF16), w2_ref[...])
    lane = lax.broadcasted_iota(jnp.int32, (1, ROUTER_COLS), 1)
    gcol = jnp.sum(jnp.where(lane == e + ROUTER_EXP_OFF, gate_ref[...], 0.0), axis=-1, keepdims=True)
    o_ref[...] += gcol * y


def _moe_dense(xn, gate, w1, w3, w2, tm_pref=1024):
    m, d = xn.shape
    ne, _, hdim = w1.shape
    tm = _row_tile(m, tm_pref)
    return pl.pallas_call(
        _moe_dense_kernel,
        grid=(m // tm, ne),
        in_specs=[pl.BlockSpec((tm, d), lambda i, e: (i, 0)),
                  pl.BlockSpec((tm, ROUTER_COLS), lambda i, e: (i, 0)),
                  pl.BlockSpec((None, d, hdim), lambda i, e: (e, 0, 0)),
                  pl.BlockSpec((None, d, hdim), lambda i, e: (e, 0, 0)),
                  pl.BlockSpec((None, hdim, d), lambda i, e: (e, 0, 0))],
        out_specs=pl.BlockSpec((tm, d), lambda i, e: (i, 0)),
        out_shape=jax.ShapeDtypeStruct((m, d), F32),
        compiler_params=_cparams(("parallel", "arbitrary")),
        name="moe_dense",
    )(xn, gate, w1, w3, w2)


def _tail2_kernel(h1_ref, y_ref, p_ref, gp_ref, wg_ref, bg_ref, wp_ref, gfin_ref, o_ref):
    h = h1_ref[...] + y_ref[...]
    emb = _dot(p_ref[...].astype(BF16), wp_ref[...])
    gate = jax.nn.sigmoid(_dot(_rms(h, gp_ref[...]).astype(BF16), wg_ref[...]) + bg_ref[...])
    h = h + gate * emb
    o_ref[...] = _rms(h, gfin_ref[...])


def _tail2(h1, y, p, g_ple, wg, bg, wp, g_fin, tm_pref=256):
    m, d = h1.shape
    tm = _row_tile(m, tm_pref)
    row = lambda c: pl.BlockSpec((tm, c), lambda i: (i, 0))
    full = lambda a: pl.BlockSpec(a.shape, lambda i: (0,) * a.ndim)
    params = [g_ple, wg, bg, wp, g_fin]
    return pl.pallas_call(
        _tail2_kernel,
        grid=(m // tm,),
        in_specs=[row(d), row(d), row(p.shape[1])] + [full(a) for a in params],
        out_specs=row(d),
        out_shape=jax.ShapeDtypeStruct((m, d), F32),
        compiler_params=_cparams(("parallel",)),
        name="tail2",
    )(h1, y, p, *params)


PAGES_PER_STEP = 8
PAGE_ROWS_PER_TOKEN = KV_GROUPS


def _cmp_paged_kernel(pt_ref, *refs, pages):
    kp = refs[:pages]
    vp = refs[pages:2 * pages]
    wk_ref, wv_ref, ok_ref, ov_ref = refs[2 * pages:]
    chunks = kp[0].shape[0] // (PAGE_ROWS_PER_TOKEN * CMP_STRIDE)
    stride = PAGE_ROWS_PER_TOKEN * CMP_STRIDE
    for src, w, dst in ((kp, wk_ref, ok_ref), (vp, wv_ref, ov_ref)):
        for g in range(KV_GROUPS):
            acc = jnp.zeros((pages * chunks, 2 * HEAD_DIM), F32)
            for j in range(CMP_STRIDE):
                xj = jnp.concatenate(
                    [p[pl.ds(PAGE_ROWS_PER_TOKEN * j + g, chunks, stride=stride), :] for p in src], axis=0)
                acc = acc + _dot(xj.astype(BF16), w[j])
            dst[g] = acc


def _page_specs(page_rows, pages):
    return [pl.BlockSpec((page_rows, HEAD_DIM), functools.partial(lambda i, b, s, pt: (pt[b, s * pages + i], 0), i))
            for i in range(pages)]


def _cmp_paged(pool_k, pool_v, page_table, wk, wv, page_size):
    b, n_pages = page_table.shape
    pages = PAGES_PER_STEP if n_pages % PAGES_PER_STEP == 0 else n_pages
    page_rows = page_size * PAGE_ROWS_PER_TOKEN
    chunks = page_size // CMP_STRIDE
    wsp = pl.BlockSpec((CMP_STRIDE, HEAD_DIM, 2 * HEAD_DIM), lambda i, s, pt: (0, 0, 0))
    osp = pl.BlockSpec((None, KV_GROUPS, pages * chunks, 2 * HEAD_DIM), lambda i, s, pt: (i, 0, s, 0))
    osh = jax.ShapeDtypeStruct((b, KV_GROUPS, n_pages * chunks, 2 * HEAD_DIM), F32)
    return pl.pallas_call(
        functools.partial(_cmp_paged_kernel, pages=pages),
        grid_spec=pltpu.PrefetchScalarGridSpec(
            num_scalar_prefetch=1,
            grid=(b, n_pages // pages),
            in_specs=_page_specs(page_rows, pages) * 2 + [wsp, wsp],
            out_specs=[osp, osp]),
        out_shape=[osh, osh],
        compiler_params=_cparams(("parallel", "arbitrary")),
        name="cmp_paged",
    )(page_table, *([pool_k] * pages), *([pool_v] * pages), wk, wv)


def _nsa_sample_kernel(pt_ref, q_ref, ck_ref, cv_ref, a_ref, ksn_ref, vsn_ref, wk_ref, wv_ref, kwn_ref, vwn_ref,
                       gt_ref, *rest, pages, page_size, n_blk, n_cmp, n_pick, past_len):
    kp = rest[:pages]
    vp = rest[pages:2 * pages]
    o_ref, wko_ref, wvo_ref, sel_scr, oc_scr, m_scr, l_scr, acc_scr, q_scr = rest[2 * pages:]
    step = pl.program_id(1)
    last = pl.num_programs(1) - 1
    qpos = past_len
    nbp = sel_scr.shape[2]
    rowi = lax.broadcasted_iota(jnp.int32, (SUBLANES, 1), 0)
    lane = lax.broadcasted_iota(jnp.int32, (1, LANES), 1)
    blk = lax.broadcasted_iota(jnp.int32, (1, nbp), 1)

    def slopes(g):
        sl = jnp.zeros((SUBLANES, 1), F32)
        for h in range(HEADS_PER_GROUP):
            sl = jnp.where(rowi == h, 2.0 ** -(HEADS_PER_GROUP * g + h + 1), sl)
        return sl

    @pl.when(step == 0)
    def _():
        qrow = q_ref[...]
        ncp = ck_ref.shape[1]
        cidx = lax.broadcasted_iota(jnp.int32, (1, ncp), 1)
        dist = jnp.broadcast_to(qpos - (cidx * CMP_STRIDE + (CMP_BLOCK - 1)), (SUBLANES, ncp))
        mask = (dist >= 0) & (cidx < n_cmp)
        cur = qpos // SEL_BLOCK
        valid = blk <= cur
        forced = (blk == 0) | (blk == cur) | (blk == cur - 1)
        for g in range(KV_GROUPS):
            heads = [qrow[:, (HEADS_PER_GROUP * g + h) * HEAD_DIM:(HEADS_PER_GROUP * g + h + 1) * HEAD_DIM]
                     for h in range(HEADS_PER_GROUP)]
            q8 = jnp.concatenate(heads + [jnp.zeros((SUBLANES - HEADS_PER_GROUP, HEAD_DIM), BF16)], axis=0)
            q_scr[g] = q8
            p = _masked_softmax(_dot_nt(q8, ck_ref[g]) - slopes(g) * dist.astype(F32), mask)
            oc_scr[g] = _dot(p.astype(BF16), cv_ref[g])
            p_sum = p[0:1]
            for h in range(1, HEADS_PER_GROUP):
                p_sum = p_sum + p[h:h + 1]
            p_sum = jnp.broadcast_to(p_sum, (SUBLANES, ncp))
            imp = sum(_dot(t, a_ref[...]) for t in _split3(p_sum))
            score = jnp.where(forced & valid, FORCE_SCORE, jnp.where(valid, imp, -FORCE_SCORE))
            score = jnp.where(blk < n_blk, score, -2.0 * FORCE_SCORE)

            def rank_body(j, cnt):
                sj = jnp.max(jnp.where(blk == j, score, -jnp.inf), axis=-1, keepdims=True)
                return cnt + jnp.where((sj > score) | ((sj == score) & (j < blk)), 1.0, 0.0)

            rank = lax.fori_loop(0, n_blk, rank_body, jnp.zeros((SUBLANES, nbp), F32))
            sel_scr[g] = jnp.where((rank < n_pick) & (blk < n_blk), 1.0, 0.0)
            m_scr[g] = jnp.full((SUBLANES, LANES), NEG_INF, F32)
            l_scr[g] = jnp.zeros((SUBLANES, LANES), F32)
            acc_scr[g] = jnp.zeros((SUBLANES, HEAD_DIM), F32)

    blocks_per_page = page_size // SEL_BLOCK
    for g in range(KV_GROUPS):
        q8 = q_scr[g]
        selg = sel_scr[g]
        sl = slopes(g)
        scores, masks = [], []
        for i in range(pages):
            pg = step * pages + i
            k = kp[i][pl.ds(g, page_size, stride=PAGE_ROWS_PER_TOKEN), :].astype(BF16)
            dist = qpos - (pg * page_size + lane)
            picked = jnp.zeros((SUBLANES, LANES), F32)
            for r in range(blocks_per_page):
                sr = jnp.max(jnp.where(blk == blocks_per_page * pg + r, selg, 0.0), axis=-1, keepdims=True)
                picked = jnp.where(lane // SEL_BLOCK == r, sr, picked)
            mask = (dist >= 0) & (picked > 0.5)
            scores.append(jnp.where(mask, _dot_nt(q8, k) - sl * dist.astype(F32), NEG_INF))
            masks.append(mask)
        s = jnp.concatenate(scores, axis=1)
        mask = jnp.concatenate(masks, axis=1)
        v = jnp.concatenate([vp[i][pl.ds(g, page_size, stride=PAGE_ROWS_PER_TOKEN), :].astype(BF16)
                             for i in range(pages)], axis=0)
        m_old = m_scr[g][:, 0:1]
        m_new = jnp.maximum(m_old, jnp.max(s, axis=-1, keepdims=True))
        p = jnp.where(mask, jnp.exp(s - m_new), 0.0)
        alpha = jnp.exp(m_old - m_new)
        acc_scr[g] = alpha * acc_scr[g] + _dot(p.astype(BF16), v)
        l_scr[g] = jnp.broadcast_to(alpha * l_scr[g][:, 0:1] + jnp.sum(p, axis=-1, keepdims=True), (SUBLANES, LANES))
        m_scr[g] = jnp.broadcast_to(m_new, (SUBLANES, LANES))

    @pl.when(step == last)
    def _():
        wb = wk_ref.shape[0]
        wrow = lax.broadcasted_iota(jnp.int32, (wb, 1), 0)
        wk_new = jnp.where(wrow == wb - 1, kwn_ref[...], pltpu.roll(wk_ref[...], wb - 1, 0))
        wv_new = jnp.where(wrow == wb - 1, vwn_ref[...], pltpu.roll(wv_ref[...], wb - 1, 0))
        wko_ref[...] = wk_new
        wvo_ref[...] = wv_new
        widx = lax.broadcasted_iota(jnp.int32, (1, wb), 1)
        wdist = jnp.broadcast_to(wb - 1 - widx, (SUBLANES, wb))
        wmask = (wdist >= 0) & (wdist < WINDOW) & (qpos - wdist >= 0)
        for g in range(KV_GROUPS):
            gs = slice(g * HEAD_DIM, (g + 1) * HEAD_DIM)
            q8 = q_scr[g]
            sl = slopes(g)
            kn = ksn_ref[:, gs].astype(BF16).astype(F32)
            vn = vsn_ref[:, gs].astype(BF16).astype(F32)
            s_n = jnp.sum(q8.astype(F32) * kn, axis=-1, keepdims=True)
            sel_n = jnp.max(jnp.where(blk == n_blk - 1, sel_scr[g], 0.0), axis=-1, keepdims=True) > 0.5
            m_old = m_scr[g][:, 0:1]
            m_new = jnp.maximum(m_old, jnp.where(sel_n, s_n, NEG_INF))
            p_n = jnp.where(sel_n, jnp.exp(s_n - m_new), 0.0)
            alpha = jnp.exp(m_old - m_new)
            acc = alpha * acc_scr[g] + p_n.astype(BF16).astype(F32) * vn
            l = alpha * l_scr[g][:, 0:1] + p_n
            o_s = acc / jnp.maximum(l, TINY)
            p = _masked_softmax(_dot_nt(q8, wk_new[:, gs].astype(BF16)) - sl * wdist.astype(F32), wmask)
            o_w = _dot(p.astype(BF16), wv_new[:, gs].astype(BF16))
            gt = jnp.broadcast_to(gt_ref[:, g * LANES:(g + 1) * LANES], (SUBLANES, LANES))
            gcol = [jnp.sum(jnp.where(lane == k * HEADS_PER_GROUP + rowi, gt, 0.0), axis=-1, keepdims=True)
                    for k in range(3)]
            o = gcol[0] * oc_scr[g] + gcol[1] * o_s + gcol[2] * o_w
            for h in range(HEADS_PER_GROUP):
                hd = HEADS_PER_GROUP * g + h
                o_ref[:, hd * HEAD_DIM:(hd + 1) * HEAD_DIM] = o[h:h + 1]


def _nsa_sample(qbf, ck, cv, ks_new, vs_new, win_k, win_v, kw_new, vw_new, gates, pool_k, pool_v, page_table,
                page_size):
    b, n_pages = page_table.shape
    past_len = n_pages * page_size
    ncp = ck.shape[2]
    n_cmp = (past_len + 1 + CMP_STRIDE - 1) // CMP_STRIDE - 1
    n_blk = (past_len + 1 + SEL_BLOCK - 1) // SEL_BLOCK
    nbp = -(-n_blk // LANES) * LANES
    wb = win_k.shape[1]
    assert wb == WINDOW and n_blk >= TOP_N and ncp == n_cmp and page_size % SEL_BLOCK == 0
    pages = PAGES_PER_STEP if n_pages % PAGES_PER_STEP == 0 else n_pages
    page_rows = page_size * PAGE_ROWS_PER_TOKEN
    a = _imp_matrix(ncp, nbp)
    tok = lambda c: pl.BlockSpec((None, 1, c), lambda i, s, pt: (i, 0, 0))
    csp = pl.BlockSpec((None, KV_GROUPS, ncp, HEAD_DIM), lambda i, s, pt: (i, 0, 0, 0))
    wsp = pl.BlockSpec((None, wb, KV_WIDTH), lambda i, s, pt: (i, 0, 0))
    asp = pl.BlockSpec(a.shape, lambda i, s, pt: (0, 0))
    grp_scr = lambda c, dt: pltpu.VMEM((KV_GROUPS, SUBLANES, c), dt)
    return pl.pallas_call(
        functools.partial(_nsa_sample_kernel, pages=pages, page_size=page_size, n_blk=n_blk, n_cmp=n_cmp,
                          n_pick=TOP_N, past_len=past_len),
        grid_spec=pltpu.PrefetchScalarGridSpec(
            num_scalar_prefetch=1,
            grid=(b, n_pages // pages),
            in_specs=[tok(NSA_WIDTH), csp, csp, asp, tok(KV_WIDTH), tok(KV_WIDTH), wsp, wsp, tok(KV_WIDTH),
                      tok(KV_WIDTH), tok(GATE_COLS)] + _page_specs(page_rows, pages) * 2,
            out_specs=[tok(NSA_WIDTH), wsp, wsp],
            scratch_shapes=[grp_scr(nbp, F32), grp_scr(HEAD_DIM, F32), grp_scr(LANES, F32), grp_scr(LANES, F32),
                            grp_scr(HEAD_DIM, F32), grp_scr(HEAD_DIM, BF16)]),
        out_shape=[jax.ShapeDtypeStruct((b, 1, NSA_WIDTH), F32), jax.ShapeDtypeStruct((b, wb, KV_WIDTH), F32),
                   jax.ShapeDtypeStruct((b, wb, KV_WIDTH), F32)],
        compiler_params=_cparams(("parallel", "arbitrary")),
        name="nsa_sample",
    )(page_table, qbf, ck, cv, a, ks_new, vs_new, win_k, win_v, kw_new, vw_new, gates,
      *([pool_k] * pages), *([pool_v] * pages))


def kernel(x_prompt, x_sample, p_prompt, p_sample, cache_cmp_k, cache_cmp_v, cache_sel_k, cache_sel_v, cache_win_k, cache_win_v, state_ssm_re, state_ssm_im, page_table, norm_mix, w_in, cmp_pe_k, cmp_w1_k, cmp_w2_k, cmp_pe_v, cmp_w1_v, cmp_w2_v, ssm_a_re, ssm_a_im, ssm_log_dt, ssm_b_re, ssm_b_im, ssm_c_re, ssm_c_im, ssm_d, w_glu, b_glu, norm_nsa_out, norm_ssm_out, w_out, norm_ffn, router_grp_w, router_grp_b, router_exp_w, router_exp_b, exp_w1, exp_w3, exp_w2, norm_ple, w_ple_gate, b_ple_gate, w_ple_proj, norm_final):
    depth = w_in.shape[0]
    assert depth == 1, "one layer per step"
    i = 0
    b, s, d = x_prompt.shape
    db, t, _ = x_sample.shape
    assert t == 1, "the sample group decodes one token per sequence"
    ssm_ch = w_glu.shape[1]
    page_size = cache_cmp_k.shape[2]
    n_phys = cache_cmp_k.shape[1]
    wp = min(WINDOW, s)
    row1 = lambda a: a.reshape(1, -1)

    w_inp = _prep_w_in(w_in[i], ssm_ch)
    wk1 = _prep_cmp_w1(cmp_w1_k[i])
    wv1 = _prep_cmp_w1(cmp_w1_v[i])
    s5p = _prep_s5(ssm_a_re[i], ssm_a_im[i], ssm_log_dt[i], ssm_b_re[i], ssm_b_im[i], ssm_c_re[i], ssm_c_im[i])
    router = _prep_router(router_grp_w[i], router_grp_b[i], router_exp_w[i], router_exp_b[i])
    wglu = w_glu[i].astype(BF16)
    wout = w_out[i].astype(BF16)
    ew1 = exp_w1[i].astype(BF16)
    ew3 = exp_w3[i].astype(BF16)
    ew2 = exp_w2[i].astype(BF16)
    wpg = w_ple_gate[i].astype(BF16)
    wpp = w_ple_proj[i].astype(BF16)
    cmp_par = (cmp_pe_k[i], cmp_pe_v[i], cmp_w1_k[i], cmp_w1_v[i], cmp_w2_k[i], cmp_w2_v[i])

    def tail(h, o_nsa, o_ssm, p):
        h1, xn, gate, _, _ = _tail1(o_nsa, o_ssm, h, row1(norm_nsa_out[i]), row1(norm_ssm_out[i]), wout,
                                    row1(norm_ffn[i]), router)
        y = _moe_dense(xn, gate, ew1, ew3, ew2)
        return _tail2(h1, y, p, row1(norm_ple[i]), wpg, row1(b_ple_gate[i]), wpp, row1(norm_final))

    m = b * s
    hp = x_prompt.reshape(m, d)
    qbf, kc, vc, ks, vs, kw, vw, ksb, vsb, kwb, vwb, u, gates = _inproj(hp, row1(norm_mix[i]), w_inp, ssm_ch, 512)
    seq = lambda a: a.reshape(b, s, a.shape[-1])
    lk, lv = _cmp_prompt(seq(kc), seq(vc), wk1, wv1)
    ck, cv = _cmp_finish(lk, lv, *cmp_par)
    o_nsa = _nsa_prompt(seq(qbf), ck, cv, seq(ksb), seq(vsb), seq(kwb), seq(vwb), seq(gates))
    ns = s5p[0].shape[1]
    zero = jnp.zeros((b, 1, ns), F32)
    o_ssm, hre_p, him_p = _s5(seq(u), zero, zero, s5p, row1(ssm_d[i]), wglu, row1(b_glu[i]), scan=True, precise=False)
    y_prompt = tail(hp, o_nsa.reshape(m, -1), o_ssm.reshape(m, -1), p_prompt[i].reshape(m, -1)).reshape(b, s, d)
    kv5 = lambda a: a.reshape(1, b, s, KV_GROUPS, HEAD_DIM)
    win5 = lambda a: a.reshape(b, s, KV_GROUPS, HEAD_DIM)[None, :, s - wp:]
    st4 = lambda a, n: a.reshape(1, n, ns // SSM_STATE, SSM_STATE)
    prompt_state = (kv5(kc), kv5(vc), kv5(ks), kv5(vs), win5(kw), win5(vw), st4(hre_p, b), st4(him_p, b))

    hs = x_sample.reshape(db, d)
    qbf, kc, vc, ks, vs, kw, vw, _, _, _, _, u, gates = _inproj(hs, row1(norm_mix[i]), w_inp, ssm_ch, 512)
    pool = lambda c: c[i].reshape(n_phys * page_size * KV_GROUPS, HEAD_DIM)
    one = lambda a: a.reshape(db, 1, a.shape[-1])
    lk, lv = _cmp_paged(pool(cache_cmp_k), pool(cache_cmp_v), page_table, wk1, wv1, page_size)
    ck, cv = _cmp_finish(lk, lv, *cmp_par, kn=one(kc), vn=one(vc))
    wb = cache_win_k.shape[2]
    o_nsa, win_k, win_v = _nsa_sample(one(qbf), ck, cv, one(ks), one(vs), cache_win_k[i].reshape(db, wb, KV_WIDTH),
                                      cache_win_v[i].reshape(db, wb, KV_WIDTH), one(kw), one(vw), one(gates),
                                      pool(cache_sel_k), pool(cache_sel_v), page_table, page_size)
    o_ssm, hre_s, him_s = _s5(u, state_ssm_re[i].reshape(db, ns), state_ssm_im[i].reshape(db, ns), s5p,
                              row1(ssm_d[i]), wglu, row1(b_glu[i]), scan=False, precise=True)
    y_sample = tail(hs, o_nsa.reshape(db, -1), o_ssm, p_sample[i].reshape(db, -1)).reshape(db, 1, d)
    new5 = lambda a: a.reshape(1, db, 1, KV_GROUPS, HEAD_DIM)
    buf5 = lambda a: a.reshape(1, db, wb, KV_GROUPS, HEAD_DIM)
    sample_state = (new5(kc), new5(vc), new5(ks), new5(vs), buf5(win_k), buf5(win_v), st4(hre_s, db), st4(him_s, db))
    return (y_prompt, y_sample) + prompt_state + sample_state
```

```python
import functools

import jax
import jax.numpy as jnp
import numpy as np
from jax import lax
from jax.experimental import pallas as pl
from jax.experimental.pallas import tpu as pltpu

F32 = jnp.float32
BF16 = jnp.bfloat16

HEAD_DIM = 128
N_HEADS = 8
KV_GROUPS = 2
HEADS_PER_GROUP = N_HEADS // KV_GROUPS
NSA_WIDTH = N_HEADS * HEAD_DIM
KV_WIDTH = KV_GROUPS * HEAD_DIM
CMP_STRIDE = 16
CMP_BLOCK = 2 * CMP_STRIDE
SEL_BLOCK = 64
TOP_N = 16
WINDOW = 512
Q_BLOCK = 128
SSM_GROUP_CH = 16
SSM_STATE = 64
N_EXPERT_GROUPS = 4
EXPERTS_PER_GROUP = 8
N_EXPERTS = N_EXPERT_GROUPS * EXPERTS_PER_GROUP
RMS_EPS = 1e-6
NEG_INF = -1e30
FORCE_SCORE = 1e4
TINY = 1e-30

LANES = 128
SUBLANES = 8
GATE_COLS = KV_GROUPS * LANES
ROUTER_COLS = LANES
ROUTER_EXP_OFF = N_EXPERT_GROUPS
SEL_KV_TILE = 512
VMEM_LIMIT_BYTES = 56 * 1024 * 1024


def _cparams(sem):
    return pltpu.CompilerParams(dimension_semantics=sem, vmem_limit_bytes=VMEM_LIMIT_BYTES)


def _rms(x, g):
    return x * lax.rsqrt(jnp.mean(x * x, axis=-1, keepdims=True) + RMS_EPS) * g


def _gelu(x):
    return x * (0.5 * (1.0 + jnp.tanh(0.7978845608028654 * (x + 0.044715 * (x * x * x)))))


def _dot(a, b):
    return jnp.dot(a, b, preferred_element_type=F32)


def _dot_nt(a, b):
    return lax.dot_general(a, b, (((1,), (1,)), ((), ())), preferred_element_type=F32)


def _split2(x):
    hi = x.astype(BF16)
    lo = (x - hi.astype(F32)).astype(BF16)
    return hi, lo


def _split3(x):
    hi = x.astype(BF16)
    r = x - hi.astype(F32)
    mid = r.astype(BF16)
    lo = (r - mid.astype(F32)).astype(BF16)
    return hi, mid, lo


def _masked_softmax(s, mask):
    s = jnp.where(mask, s, NEG_INF)
    m = jnp.max(s, axis=-1, keepdims=True)
    p = jnp.where(mask, jnp.exp(s - m), 0.0)
    return p / jnp.maximum(jnp.sum(p, axis=-1, keepdims=True), TINY)


def _row_tile(m, pref):
    return pref if m % pref == 0 else m


Q_END = NSA_WIDTH
KV_END = Q_END + 6 * KV_WIDTH
U_OFF = KV_END


def _inproj_kernel(x_ref, g_ref, w_ref, qbf, kc, vc, ks, vs, kw, vw, ksb, vsb, kwb, vwb, u, gates, *, ssm_ch):
    xn = _rms(x_ref[...], g_ref[...]).astype(BF16)

    def mm(lo, hi):
        return _dot(xn, w_ref[:, lo:hi])

    qbf[...] = (mm(0, Q_END) * (HEAD_DIM ** -0.5)).astype(BF16)
    for i, (f, b) in enumerate(((kc, None), (vc, None), (ks, ksb), (vs, vsb), (kw, kwb), (vw, vwb))):
        z = mm(Q_END + KV_WIDTH * i, Q_END + KV_WIDTH * (i + 1))
        f[...] = z
        if b is not None:
            b[...] = z.astype(BF16)
    u[...] = mm(U_OFF, U_OFF + ssm_ch)
    gates[...] = jax.nn.sigmoid(mm(U_OFF + ssm_ch, U_OFF + ssm_ch + GATE_COLS))


def _prep_w_in(w_in, ssm_ch):
    wq = w_in[:, :KV_END]
    wg = w_in[:, KV_END:KV_END + 3 * N_HEADS]
    wu = w_in[:, KV_END + 3 * N_HEADS:]
    d = w_in.shape[0]
    wg = wg.reshape(d, KV_GROUPS, HEADS_PER_GROUP, 3).transpose(0, 1, 3, 2).reshape(d, KV_GROUPS, 3 * HEADS_PER_GROUP)
    wg = jnp.pad(wg, ((0, 0), (0, 0), (0, LANES - 3 * HEADS_PER_GROUP))).reshape(d, GATE_COLS)
    return jnp.concatenate([wq, wu, wg], axis=1).astype(BF16)


def _inproj(x, g, w, ssm_ch, tm_pref):
    m, d = x.shape
    tm = _row_tile(m, tm_pref)
    ncols = w.shape[1]
    row = lambda c: pl.BlockSpec((tm, c), lambda i: (i, 0))
    f32s = lambda c: jax.ShapeDtypeStruct((m, c), F32)
    bfs = lambda c: jax.ShapeDtypeStruct((m, c), BF16)
    out_shape = ([bfs(NSA_WIDTH)] + [f32s(KV_WIDTH)] * 6 + [bfs(KV_WIDTH)] * 4 + [f32s(ssm_ch), f32s(GATE_COLS)])
    out_specs = ([row(NSA_WIDTH)] + [row(KV_WIDTH)] * 10 + [row(ssm_ch), row(GATE_COLS)])
    return pl.pallas_call(
        functools.partial(_inproj_kernel, ssm_ch=ssm_ch),
        grid=(m // tm,),
        in_specs=[row(d), pl.BlockSpec((1, d), lambda i: (0, 0)),
                  pl.BlockSpec((d, ncols), lambda i: (0, 0), pipeline_mode=pl.Buffered(1))],
        out_specs=out_specs,
        out_shape=out_shape,
        compiler_params=_cparams(("parallel",)),
        name="inproj",
    )(x, g, w)


def _prep_cmp_w1(w1):
    return jnp.concatenate([w1[:CMP_STRIDE], w1[CMP_STRIDE:]], axis=-1).astype(BF16)


def _cmp_prompt_kernel(kc_ref, vc_ref, wk_ref, wv_ref, ok_ref, ov_ref, *, n_ch):
    for src, w, dst in ((kc_ref, wk_ref, ok_ref), (vc_ref, wv_ref, ov_ref)):
        acc = jnp.zeros((n_ch, 2 * HEAD_DIM), F32)
        for j in range(CMP_STRIDE):
            xj = src[pl.ds(j, n_ch, stride=CMP_STRIDE), :]
            acc = acc + _dot(xj.astype(BF16), w[j])
        dst[...] = acc


def _cmp_prompt(kc, vc, wk, wv):
    b, s, _ = kc.shape
    n_ch = s // CMP_STRIDE
    tok = pl.BlockSpec((None, s, HEAD_DIM), lambda i, g: (i, 0, g))
    wsp = pl.BlockSpec((CMP_STRIDE, HEAD_DIM, 2 * HEAD_DIM), lambda i, g: (0, 0, 0))
    osp = pl.BlockSpec((None, None, n_ch, 2 * HEAD_DIM), lambda i, g: (i, g, 0, 0))
    osh = jax.ShapeDtypeStruct((b, KV_GROUPS, n_ch, 2 * HEAD_DIM), F32)
    return pl.pallas_call(
        functools.partial(_cmp_prompt_kernel, n_ch=n_ch),
        grid=(b, KV_GROUPS),
        in_specs=[tok, tok, wsp, wsp],
        out_specs=[osp, osp],
        out_shape=[osh, osh],
        compiler_params=_cparams(("parallel", "parallel")),
        name="cmp_prompt",
    )(kc, vc, wk, wv)


def _cmp_finish_kernel(*refs, nch, has_tail):
    if has_tail:
        lk, lv, kn, vn, pek, pev, w1k, w1v, w2k, w2v, ock, ocv = refs
    else:
        lk, lv, pek, pev, w1k, w1v, w2k, w2v, ock, ocv = refs
        kn = vn = None
    row = lax.broadcasted_iota(jnp.int32, (nch, 1), 0)
    half = CMP_STRIDE * HEAD_DIM
    for l_ref, n_ref, pe_ref, w1_ref, w2_ref, o_ref in ((lk, kn, pek, w1k, w2k, ock), (lv, vn, pev, w1v, w2v, ocv)):
        lohi = l_ref[...]
        lo = lohi[:, :HEAD_DIM]
        hi = lohi[:, HEAD_DIM:]
        pe = jnp.broadcast_to(pe_ref[0:1, :], (SUBLANES, half)).astype(BF16)
        b_lo = _dot(pe, w1_ref[0:half, :].astype(BF16))[0:1]
        pe = jnp.broadcast_to(pe_ref[1:2, :], (SUBLANES, half)).astype(BF16)
        b_hi = _dot(pe, w1_ref[half:2 * half, :].astype(BF16))[0:1]
        hi_next = pltpu.roll(hi, nch - 1, 0)
        if has_tail:
            new = jnp.broadcast_to(n_ref[...], (SUBLANES, HEAD_DIM)).astype(BF16)
            tail = _dot(new, w1_ref[half:half + HEAD_DIM, :].astype(BF16))[0:1]
            hi_next = jnp.where(row == nch - 1, tail, hi_next)
        hid = _gelu(lo + hi_next + (b_lo + b_hi))
        o_ref[...] = _dot(hid.astype(BF16), w2_ref[...].astype(BF16)).astype(BF16)


def _cmp_finish(lk, lv, pek, pev, w1k, w1v, w2k, w2v, kn=None, vn=None):
    b, _, nch, _ = lk.shape
    has_tail = kn is not None
    lsp = pl.BlockSpec((None, None, nch, 2 * HEAD_DIM), lambda i, g: (i, g, 0, 0))
    full = lambda a: pl.BlockSpec(a.shape, lambda i, g: (0,) * a.ndim)
    pek2 = pek.reshape(2, CMP_STRIDE * HEAD_DIM)
    pev2 = pev.reshape(2, CMP_STRIDE * HEAD_DIM)
    w1k2 = w1k.reshape(CMP_BLOCK * HEAD_DIM, HEAD_DIM)
    w1v2 = w1v.reshape(CMP_BLOCK * HEAD_DIM, HEAD_DIM)
    args = [lk, lv]
    specs = [lsp, lsp]
    if has_tail:
        nsp = pl.BlockSpec((None, 1, HEAD_DIM), lambda i, g: (i, 0, g))
        args += [kn, vn]
        specs += [nsp, nsp]
    params = [pek2, pev2, w1k2, w1v2, w2k, w2v]
    args += params
    specs += [full(a) for a in params]
    osp = pl.BlockSpec((None, None, nch, HEAD_DIM), lambda i, g: (i, g, 0, 0))
    osh = jax.ShapeDtypeStruct((b, KV_GROUPS, nch, HEAD_DIM), BF16)
    return pl.pallas_call(
        functools.partial(_cmp_finish_kernel, nch=nch, has_tail=has_tail),
        grid=(b, KV_GROUPS),
        in_specs=specs,
        out_specs=[osp, osp],
        out_shape=[osh, osh],
        compiler_params=_cparams(("parallel", "parallel")),
        name="cmp_finish",
    )(*args)


def _imp_matrix(n_cmp_pad, n_blk_pad):
    ratio = SEL_BLOCK // CMP_STRIDE
    c = np.arange(n_cmp_pad)[:, None]
    j = np.arange(n_blk_pad)[None, :]
    return jnp.asarray(((c >= ratio * j - 1) & (c <= ratio * j + ratio - 1)).astype(np.float32), dtype=BF16)


def _expand_matrix(n_blk_pad, n_keys):
    j = np.arange(n_blk_pad)[:, None]
    k = np.arange(n_keys)[None, :]
    return jnp.asarray((k // SEL_BLOCK == j).astype(np.float32), dtype=BF16)


def _group_slopes(g):
    sg = jnp.where(g == 0, 1.0, 2.0 ** -HEADS_PER_GROUP).astype(F32)
    return [jnp.full((Q_BLOCK, 1), 2.0 ** -(h + 1), F32) * sg for h in range(HEADS_PER_GROUP)]


def _nsa_prompt_kernel(q_ref, ck_ref, cv_ref, ks_ref, vs_ref, kw_ref, vw_ref, gt_ref, a_ref, e_ref, o_ref, sk_ref,
                       *, seq, n_cmp, n_blk, n_pick):
    g = pl.program_id(1)
    n = pl.program_id(2)
    rows = HEADS_PER_GROUP * Q_BLOCK
    qb = q_ref[...]
    q = jnp.concatenate([qb[:, h * HEAD_DIM:(h + 1) * HEAD_DIM] for h in range(HEADS_PER_GROUP)], axis=0)
    qpos1 = n * Q_BLOCK + lax.broadcasted_iota(jnp.int32, (Q_BLOCK, 1), 0)
    qpos = jnp.concatenate([qpos1] * HEADS_PER_GROUP, axis=0)
    slope = jnp.concatenate(_group_slopes(g), axis=0)

    ncp = ck_ref.shape[0]
    cidx = lax.broadcasted_iota(jnp.int32, (1, ncp), 1)
    dist = qpos - (cidx * CMP_STRIDE + (CMP_BLOCK - 1))
    mask = (dist >= 0) & (cidx < n_cmp)
    p = _masked_softmax(_dot_nt(q, ck_ref[...]) - slope * dist.astype(F32), mask)
    o_c = _dot(p.astype(BF16), cv_ref[...])
    p_sum = p[0:Q_BLOCK]
    for h in range(1, HEADS_PER_GROUP):
        p_sum = p_sum + p[h * Q_BLOCK:(h + 1) * Q_BLOCK]

    imp = sum(_dot(t, a_ref[...]) for t in _split3(p_sum))
    blk = lax.broadcasted_iota(jnp.int32, (1, LANES), 1)
    cur = qpos1 // SEL_BLOCK
    valid = blk <= cur
    forced = (blk == 0) | (blk == cur) | (blk == cur - 1)
    score = jnp.where(forced & valid, FORCE_SCORE, jnp.where(valid, imp, -FORCE_SCORE))
    score_t = score.T[:n_blk]
    bidx = lax.broadcasted_iota(jnp.int32, (n_blk, 1), 0)
    rank = jnp.zeros((n_blk, Q_BLOCK), F32)
    for j in range(n_blk):
        r = score_t[j:j + 1, :]
        rank = rank + jnp.where((r > score_t) | ((r == score_t) & (j < bidx)), 1.0, 0.0)
    sel_t = jnp.where(rank < n_pick, 1.0, 0.0)
    if n_blk < LANES:
        sel_t = jnp.concatenate([sel_t, jnp.zeros((LANES - n_blk, Q_BLOCK), F32)], axis=0)
    sk_ref[...] = _dot(sel_t.T.astype(BF16), e_ref[...])

    tk = min(SEL_KV_TILE, seq)
    n_tiles = (n * Q_BLOCK + Q_BLOCK + tk - 1) // tk

    def body(t, carry):
        m, l, acc = carry
        k0 = pl.multiple_of(t * tk, tk)
        kpos = k0 + lax.broadcasted_iota(jnp.int32, (1, tk), 1)
        dist = qpos - kpos
        picked = sk_ref[:, pl.ds(k0, tk)] > 0.5
        mask = (dist >= 0) & jnp.concatenate([picked] * HEADS_PER_GROUP, axis=0)
        s = _dot_nt(q, ks_ref[pl.ds(k0, tk), :]) - slope * dist.astype(F32)
        s = jnp.where(mask, s, NEG_INF)
        m_new = jnp.maximum(m, jnp.max(s, axis=-1, keepdims=True))
        p = jnp.where(mask, jnp.exp(s - m_new), 0.0)
        alpha = jnp.exp(m - m_new)
        l = alpha * l + jnp.sum(p, axis=-1, keepdims=True)
        acc = alpha * acc + _dot(p.astype(BF16), vs_ref[pl.ds(k0, tk), :])
        return m_new, l, acc

    init = (jnp.full((rows, 1), NEG_INF, F32), jnp.zeros((rows, 1), F32), jnp.zeros((rows, HEAD_DIM), F32))
    _, l, acc = lax.fori_loop(0, n_tiles, body, init)
    o_s = acc / jnp.maximum(l, TINY)

    band = min(WINDOW + Q_BLOCK, seq)
    st = pl.multiple_of(jnp.minimum(jnp.maximum(n * Q_BLOCK - WINDOW, 0), seq - band), Q_BLOCK)
    kpos = st + lax.broadcasted_iota(jnp.int32, (1, band), 1)
    dist = qpos - kpos
    mask = (dist >= 0) & (dist < WINDOW)
    p = _masked_softmax(_dot_nt(q, kw_ref[pl.ds(st, band), :]) - slope * dist.astype(F32), mask)
    o_w = _dot(p.astype(BF16), vw_ref[pl.ds(st, band), :])

    gt = gt_ref[...]
    for h in range(HEADS_PER_GROUP):
        sl = slice(h * Q_BLOCK, (h + 1) * Q_BLOCK)
        o_ref[:, h * HEAD_DIM:(h + 1) * HEAD_DIM] = (
            gt[:, h:h + 1] * o_c[sl]
            + gt[:, HEADS_PER_GROUP + h:HEADS_PER_GROUP + h + 1] * o_s[sl]
            + gt[:, 2 * HEADS_PER_GROUP + h:2 * HEADS_PER_GROUP + h + 1] * o_w[sl])


def _nsa_prompt(qbf, ck, cv, ksb, vsb, kwb, vwb, gates):
    b, s, _ = qbf.shape
    ncp = ck.shape[2]
    n_cmp = s // CMP_STRIDE - 1
    n_blk = s // SEL_BLOCK
    assert s % Q_BLOCK == 0 and TOP_N <= n_blk <= LANES
    a = _imp_matrix(ncp, LANES)
    e = _expand_matrix(LANES, s)
    gw = HEADS_PER_GROUP * HEAD_DIM
    qsp = pl.BlockSpec((None, Q_BLOCK, gw), lambda i, g, n: (i, n, g))
    csp = pl.BlockSpec((None, None, ncp, HEAD_DIM), lambda i, g, n: (i, g, 0, 0))
    ksp = pl.BlockSpec((None, s, HEAD_DIM), lambda i, g, n: (i, 0, g))
    gsp = pl.BlockSpec((None, Q_BLOCK, LANES), lambda i, g, n: (i, n, g))
    full = lambda x: pl.BlockSpec(x.shape, lambda i, g, n: (0,) * x.ndim)
    return pl.pallas_call(
        functools.partial(_nsa_prompt_kernel, seq=s, n_cmp=n_cmp, n_blk=n_blk, n_pick=TOP_N),
        grid=(b, KV_GROUPS, s // Q_BLOCK),
        in_specs=[qsp, csp, csp, ksp, ksp, ksp, ksp, gsp, full(a), full(e)],
        out_specs=qsp,
        out_shape=jax.ShapeDtypeStruct((b, s, NSA_WIDTH), F32),
        scratch_shapes=[pltpu.VMEM((Q_BLOCK, s), F32)],
        compiler_params=_cparams(("parallel", "parallel", "arbitrary")),
        name="nsa_prompt",
    )(qbf, ck, cv, ksb, vsb, kwb, vwb, gates, a, e)


SSM_LANE_GROUPS = LANES // SSM_GROUP_CH
SSM_STATE_TILE = SSM_LANE_GROUPS * SSM_STATE


def _prep_s5(a_re, a_im, log_dt, b_re, b_im, c_re, c_im):
    ng = a_re.shape[0]
    nk = ng // SSM_LANE_GROUPS
    dt = jnp.exp(log_dt)[:, None]
    mag = jnp.exp(a_re * dt)
    lb_re = mag * jnp.cos(a_im * dt)
    lb_im = mag * jnp.sin(a_im * dt)
    den = a_re * a_re + a_im * a_im
    nr = lb_re - 1.0
    f_re = (nr * a_re + lb_im * a_im) / den
    f_im = (lb_im * a_re - nr * a_im) / den
    bb_re = f_re[..., None] * b_re - f_im[..., None] * b_im
    bb_im = f_re[..., None] * b_im + f_im[..., None] * b_re
    eye = jnp.eye(SSM_LANE_GROUPS, dtype=F32)

    def in_map(bb):
        t = bb.reshape(nk, SSM_LANE_GROUPS, SSM_STATE, SSM_GROUP_CH)
        t = jnp.einsum('kgnc,gh->kgchn', t, eye)
        return t.reshape(nk, LANES, SSM_STATE_TILE)

    def out_map(c):
        t = c.reshape(nk, SSM_LANE_GROUPS, SSM_GROUP_CH, SSM_STATE)
        t = jnp.einsum('kgcn,gh->kgnhc', t, eye)
        return t.reshape(nk, SSM_STATE_TILE, LANES)

    bb = jnp.concatenate([in_map(bb_re), in_map(bb_im)], axis=-1)
    cm = jnp.concatenate([out_map(c_re), -out_map(c_im)], axis=1)
    bb_hi = bb.astype(BF16)
    bb_lo = (bb - bb_hi.astype(F32)).astype(BF16)
    lam = jnp.stack([lb_re.reshape(-1), lb_im.reshape(-1)])
    return lam, bb_hi, bb_lo, cm.astype(BF16)


def _s5_kernel(u_ref, h0r_ref, h0i_ref, lam_ref, bbh_ref, bbl_ref, cm_ref, d_ref, wglu_ref, bglu_ref,
               o_ref, hr_ref, hi_ref, *scratch, rows, scan, precise):
    nk = bbh_ref.shape[0]
    st = SSM_STATE_TILE
    if scan:
        cr_scr, ci_scr = scratch
        t = pl.program_id(1)

        @pl.when(t == 0)
        def _():
            cr_scr[...] = h0r_ref[...]
            ci_scr[...] = h0i_ref[...]

        row = lax.broadcasted_iota(jnp.int32, (rows, 1), 0)
    u = u_ref[...]
    ys = []
    for k in range(nk):
        uk = u[:, k * LANES:(k + 1) * LANES]
        if precise:
            uh, ul = _split2(uk)
            bu = _dot(uh, bbh_ref[k]) + _dot(ul, bbh_ref[k]) + _dot(uh, bbl_ref[k])
        else:
            bu = _dot(uk.astype(BF16), bbh_ref[k])
        br = bu[:, :st]
        bi = bu[:, st:]
        ks = slice(k * st, (k + 1) * st)
        lr = lam_ref[0:1, ks]
        li = lam_ref[1:2, ks]
        if scan:
            cr = cr_scr[:, ks]
            ci = ci_scr[:, ks]
            first = row == 0
            br = br + jnp.where(first, lr * cr - li * ci, 0.0)
            bi = bi + jnp.where(first, lr * ci + li * cr, 0.0)
            pr, pi = lr, li
            sh = 1
            while sh < rows:
                sr = jnp.where(row >= sh, pltpu.roll(br, sh, 0), 0.0)
                si = jnp.where(row >= sh, pltpu.roll(bi, sh, 0), 0.0)
                br, bi = br + (pr * sr - pi * si), bi + (pr * si + pi * sr)
                pr, pi = pr * pr - pi * pi, 2.0 * (pr * pi)
                sh *= 2
            cr_scr[:, ks] = br[rows - 1:rows]
            ci_scr[:, ks] = bi[rows - 1:rows]
        else:
            h0r = h0r_ref[:, ks]
            h0i = h0i_ref[:, ks]
            br, bi = br + (lr * h0r - li * h0i), bi + (lr * h0i + li * h0r)
            hr_ref[:, ks] = br
            hi_ref[:, ks] = bi
        hcat = jnp.concatenate([br, bi], axis=1).astype(BF16)
        ys.append(_dot(hcat, cm_ref[k]) + d_ref[:, k * LANES:(k + 1) * LANES] * uk)
    v = _gelu(jnp.concatenate(ys, axis=1))
    o_ref[...] = v * jax.nn.sigmoid(_dot(v.astype(BF16), wglu_ref[...]) + bglu_ref[...])
    if scan:
        hr_ref[...] = cr_scr[...]
        hi_ref[...] = ci_scr[...]


def _s5(u, h0r, h0i, prep, d, wglu, bglu, *, scan, precise, tc_pref=256):
    lam, bbh, bbl, cm = prep
    ns = lam.shape[1]
    full = lambda a: pl.BlockSpec(a.shape, lambda *idx: (0,) * a.ndim)
    params = [lam, bbh, bbl, cm, d, wglu, bglu]
    if scan:
        b, t, c = u.shape
        tc = _row_tile(t, tc_pref)
        grid = (b, t // tc)
        usp = pl.BlockSpec((None, tc, c), lambda i, j: (i, j, 0))
        hsp = pl.BlockSpec((None, 1, ns), lambda i, j: (i, 0, 0))
        rows = tc
        scratch = [pltpu.VMEM((1, ns), F32), pltpu.VMEM((1, ns), F32)]
        sem = ("parallel", "arbitrary")
        osh = [jax.ShapeDtypeStruct((b, t, c), F32)] + [jax.ShapeDtypeStruct((b, 1, ns), F32)] * 2
    else:
        b, c = u.shape
        grid = (1,)
        usp = pl.BlockSpec((b, c), lambda i: (0, 0))
        hsp = pl.BlockSpec((b, ns), lambda i: (0, 0))
        rows = b
        scratch = []
        sem = ("arbitrary",)
        osh = [jax.ShapeDtypeStruct((b, c), F32)] + [jax.ShapeDtypeStruct((b, ns), F32)] * 2
    return pl.pallas_call(
        functools.partial(_s5_kernel, rows=rows, scan=scan, precise=precise),
        grid=grid,
        in_specs=[usp, hsp, hsp] + [full(a) for a in params],
        out_specs=[usp, hsp, hsp],
        out_shape=osh,
        scratch_shapes=scratch,
        compiler_params=_cparams(sem),
        name="s5_scan" if scan else "s5_step",
    )(u, h0r, h0i, *params)


def _prep_router(rg_w, rg_b, re_w, re_b):
    d = rg_w.shape[0]
    w = jnp.concatenate([rg_w, re_w.transpose(1, 0, 2).reshape(d, N_EXPERTS)], axis=1)
    w = jnp.pad(w, ((0, 0), (0, ROUTER_COLS - w.shape[1])))
    b = jnp.pad(jnp.concatenate([rg_b, re_b.reshape(-1)]), (0, ROUTER_COLS - N_EXPERT_GROUPS - N_EXPERTS))
    hi = w.astype(BF16)
    lo = (w - hi.astype(F32)).astype(BF16)
    return hi, lo, b.reshape(1, ROUTER_COLS)


def _tail1_kernel(on_ref, os_ref, h_ref, gn_ref, gs_ref, wout_ref, gf_ref, wrh_ref, wrl_ref, br_ref,
                  h1_ref, xn_ref, rid_ref, rw_ref):
    cat = jnp.concatenate([_rms(on_ref[...], gn_ref[...]), _rms(os_ref[...], gs_ref[...])], axis=1)
    h1 = h_ref[...] + _dot(cat.astype(BF16), wout_ref[...])
    h1_ref[...] = h1
    xn = _rms(h1, gf_ref[...])
    xh, xl = _split2(xn)
    xn_ref[...] = xn
    lg = _dot(xh, wrh_ref[...]) + _dot(xl, wrh_ref[...]) + _dot(xh, wrl_ref[...]) + br_ref[...]
    lane_i = lax.broadcasted_iota(jnp.int32, (1, ROUTER_COLS), 1)
    lane = lane_i.astype(F32)
    ninf = -jnp.inf
    big = float(ROUTER_COLS)
    is_grp = lane_i < N_EXPERT_GROUPS
    lgm = jnp.where(is_grp, lg, ninf)
    mx = jnp.max(lgm, axis=-1, keepdims=True)
    grp = jnp.min(jnp.where(lgm == mx, lane, big), axis=-1, keepdims=True)
    p_grp = 1.0 / jnp.sum(jnp.where(is_grp, jnp.exp(lgm - mx), 0.0), axis=-1, keepdims=True)
    first = ROUTER_EXP_OFF + EXPERTS_PER_GROUP * grp
    lem = jnp.where((lane >= first) & (lane < first + EXPERTS_PER_GROUP), lg, ninf)
    v1 = jnp.max(lem, axis=-1, keepdims=True)
    i1 = jnp.min(jnp.where(lem == v1, lane, big), axis=-1, keepdims=True)
    lem2 = jnp.where(lane == i1, ninf, lem)
    v2 = jnp.max(lem2, axis=-1, keepdims=True)
    i2 = jnp.min(jnp.where(lem2 == v2, lane, big), axis=-1, keepdims=True)
    e2 = jnp.exp(v2 - v1)
    w1 = (1.0 / (1.0 + e2)) * p_grp
    w2 = (e2 / (1.0 + e2)) * p_grp
    rid_ref[...] = jnp.where(lane_i == 0, i1 - ROUTER_EXP_OFF,
                             jnp.where(lane_i == 1, i2 - ROUTER_EXP_OFF, 0.0)).astype(jnp.int32)
    rw_ref[...] = jnp.where(lane_i == 0, w1, jnp.where(lane_i == 1, w2, 0.0))


def _tail1(o_nsa, o_ssm, h, g_nsa, g_ssm, wout, g_ffn, router, tm_pref=256):
    m, d = h.shape
    tm = _row_tile(m, tm_pref)
    wrh, wrl, br = router
    row = lambda c: pl.BlockSpec((tm, c), lambda i: (i, 0))
    full = lambda a: pl.BlockSpec(a.shape, lambda i: (0,) * a.ndim)
    params = [g_nsa, g_ssm, wout, g_ffn, wrh, wrl, br]
    return pl.pallas_call(
        _tail1_kernel,
        grid=(m // tm,),
        in_specs=[row(o_nsa.shape[1]), row(o_ssm.shape[1]), row(d)] + [full(a) for a in params],
        out_specs=[row(d), row(d), row(ROUTER_COLS), row(ROUTER_COLS)],
        out_shape=[jax.ShapeDtypeStruct((m, d), F32), jax.ShapeDtypeStruct((m, d), F32),
                   jax.ShapeDtypeStruct((m, ROUTER_COLS), jnp.int32), jax.ShapeDtypeStruct((m, ROUTER_COLS), F32)],
        compiler_params=_cparams(("parallel",)),
        name="tail1",
    )(o_nsa, o_ssm, h, *params)


MOE_TILE = 256


def _moe_dispatch(rid):
    t = rid.shape[0]
    pairs = 2 * t
    e = rid.reshape(-1)
    onehot = (e[:, None] == jnp.arange(N_EXPERTS, dtype=jnp.int32)[None, :]).astype(jnp.int32)
    csum = jnp.cumsum(onehot, axis=0)
    rank = jnp.take_along_axis(csum, e[:, None], axis=1)[:, 0] - 1
    tiles = (csum[-1] + MOE_TILE - 1) // MOE_TILE
    tile_end = jnp.cumsum(tiles)
    pos = (tile_end - tiles)[e] * MOE_TILE + rank
    nt_max = -(-pairs // MOE_TILE) + N_EXPERTS
    n_tiles = tile_end[-1]
    tile_ids = jnp.arange(nt_max, dtype=jnp.int32)
    tile_expert = jnp.minimum(jnp.searchsorted(tile_end, tile_ids, side='right'), N_EXPERTS - 1).astype(jnp.int32)
    tile_expert = jnp.where(tile_ids < n_tiles, tile_expert, tile_expert[n_tiles - 1])
    row_token = jnp.zeros((nt_max * MOE_TILE,), jnp.int32).at[pos].set(jnp.arange(pairs, dtype=jnp.int32) // 2)
    return pos.astype(jnp.int32), tile_expert, row_token, n_tiles.astype(jnp.int32).reshape(1)


def _moe_sparse_kernel(te_ref, rt_ref, nt_ref, x_hbm, w1_ref, w3_ref, w2_ref, y_ref, xbuf, w1b, w3b, w2b, sem):
    t = pl.program_id(0)
    n = nt_ref[0]

    def row_copy(tile, slot, r):
        tok = rt_ref[tile * MOE_TILE + r]
        return pltpu.make_async_copy(x_hbm.at[pl.ds(tok, 1), :], xbuf.at[slot, pl.ds(r, 1), :], sem.at[slot])

    def start_tile(tile, slot):
        def body(r, c):
            row_copy(tile, slot, r).start()
            return c
        lax.fori_loop(0, MOE_TILE, body, 0, unroll=8)

    def wait_tile(tile, slot):
        def body(r, c):
            row_copy(tile, slot, r).wait()
            return c
        lax.fori_loop(0, MOE_TILE, body, 0, unroll=8)

    @pl.when(t == 0)
    def _():
        start_tile(0, 0)

    @pl.when(t + 1 < n)
    def _():
        start_tile(t + 1, (t + 1) % 2)

    @pl.when(t < n)
    def _():
        slot = t % 2
        wait_tile(t, slot)

        @pl.when((t == 0) | (te_ref[t] != te_ref[jnp.maximum(t - 1, 0)]))
        def _():
            w1b[...] = w1_ref[...].astype(BF16)
            w3b[...] = w3_ref[...].astype(BF16)
            w2b[...] = w2_ref[...].astype(BF16)

        x = xbuf[slot].astype(BF16)
        a = _dot(x, w1b[...])
        hid = (a * jax.nn.sigmoid(a)) * _dot(x, w3b[...])
        y_ref[...] = _dot(hid.astype(BF16), w2b[...])

    @pl.when(t >= n)
    def _():
        y_ref[...] = jnp.zeros_like(y_ref)


def _moe_sparse(xn, tile_expert, row_token, n_tiles, w1, w3, w2):
    _, d = xn.shape
    _, _, hdim = w1.shape
    nt_max = tile_expert.shape[0]
    wsp = lambda r, c: pl.BlockSpec((None, r, c), lambda t, te, rt, nt: (te[t], 0, 0))
    return pl.pallas_call(
        _moe_sparse_kernel,
        grid_spec=pltpu.PrefetchScalarGridSpec(
            num_scalar_prefetch=3,
            grid=(nt_max,),
            in_specs=[pl.BlockSpec(memory_space=pl.ANY), wsp(d, hdim), wsp(d, hdim), wsp(hdim, d)],
            out_specs=pl.BlockSpec((MOE_TILE, d), lambda t, te, rt, nt: (t, 0)),
            scratch_shapes=[pltpu.VMEM((2, MOE_TILE, d), F32), pltpu.VMEM((d, hdim), BF16),
                            pltpu.VMEM((d, hdim), BF16), pltpu.VMEM((hdim, d), BF16),
                            pltpu.SemaphoreType.DMA((2,))]),
        out_shape=jax.ShapeDtypeStruct((nt_max * MOE_TILE, d), F32),
        compiler_params=_cparams(("arbitrary",)),
        name="moe_sparse",
    )(tile_expert, row_token, n_tiles, xn, w1, w3, w2)


def _tail2_kernel(pos_ref, h1_ref, rw_ref, y_hbm, p_ref, gp_ref, wg_ref, bg_ref, wp_ref, gfin_ref, o_ref, ybuf, sem,
                  *, tm, tok_off):
    base = (tok_off + pl.program_id(0) * tm) * 2

    def row_copy(r, k):
        return pltpu.make_async_copy(y_hbm.at[pl.ds(pos_ref[base + 2 * r + k], 1), :],
                                     ybuf.at[k, pl.ds(r, 1), :], sem.at[0])

    def each_row(fn):
        def body(r, c):
            fn(row_copy(r, 0))
            fn(row_copy(r, 1))
            return c
        lax.fori_loop(0, tm, body, 0, unroll=8)

    each_row(lambda cp: cp.start())
    emb = _dot(p_ref[...].astype(BF16), wp_ref[...])
    each_row(lambda cp: cp.wait())
    rw = rw_ref[...]
    h = h1_ref[...] + (rw[:, 0:1] * ybuf[0] + rw[:, 1:2] * ybuf[1])
    gate = jax.nn.sigmoid(_dot(_rms(h, gp_ref[...]).astype(BF16), wg_ref[...]) + bg_ref[...])
    h = h + gate * emb
    o_ref[...] = _rms(h, gfin_ref[...])


def _tail2(pos, tok_off, h1, rw, y, p, g_ple, wg, bg, wp, g_fin, tm_pref=256):
    m, d = h1.shape
    tm = _row_tile(m, tm_pref)
    row = lambda c: pl.BlockSpec((tm, c), lambda i, ps: (i, 0))
    full = lambda a: pl.BlockSpec(a.shape, lambda i, ps: (0,) * a.ndim)
    params = [g_ple, wg, bg, wp, g_fin]
    return pl.pallas_call(
        functools.partial(_tail2_kernel, tm=tm, tok_off=tok_off),
        grid_spec=pltpu.PrefetchScalarGridSpec(
            num_scalar_prefetch=1,
            grid=(m // tm,),
            in_specs=[row(d), row(ROUTER_COLS), pl.BlockSpec(memory_space=pl.ANY), row(p.shape[1])]
            + [full(a) for a in params],
            out_specs=row(d),
            scratch_shapes=[pltpu.VMEM((2, tm, d), F32), pltpu.SemaphoreType.DMA((1,))]),
        out_shape=jax.ShapeDtypeStruct((m, d), F32),
        compiler_params=_cparams(("arbitrary",)),
        name="tail2",
    )(pos, h1, rw, y, p, *params)


PAGES_PER_STEP = 16
PAGE_ROWS_PER_TOKEN = KV_GROUPS


def _cmp_paged_kernel(pt_ref, *refs, pages):
    kp = refs[:pages]
    vp = refs[pages:2 * pages]
    wk_ref, wv_ref, ok_ref, ov_ref = refs[2 * pages:]
    chunks = kp[0].shape[0] // (PAGE_ROWS_PER_TOKEN * CMP_STRIDE)
    stride = PAGE_ROWS_PER_TOKEN * CMP_STRIDE
    for src, w, dst in ((kp, wk_ref, ok_ref), (vp, wv_ref, ov_ref)):
        for g in range(KV_GROUPS):
            acc = jnp.zeros((pages * chunks, 2 * HEAD_DIM), F32)
            for j in range(CMP_STRIDE):
                xj = jnp.concatenate(
                    [p[pl.ds(PAGE_ROWS_PER_TOKEN * j + g, chunks, stride=stride), :] for p in src], axis=0)
                acc = acc + _dot(xj.astype(BF16), w[j])
            dst[g] = acc


def _page_specs(page_rows, pages):
    return [pl.BlockSpec((page_rows, HEAD_DIM), functools.partial(lambda i, b, s, pt: (pt[b, s * pages + i], 0), i))
            for i in range(pages)]


def _cmp_paged(pool_k, pool_v, page_table, wk, wv, page_size):
    b, n_pages = page_table.shape
    pages = PAGES_PER_STEP if n_pages % PAGES_PER_STEP == 0 else n_pages
    page_rows = page_size * PAGE_ROWS_PER_TOKEN
    chunks = page_size // CMP_STRIDE
    wsp = pl.BlockSpec((CMP_STRIDE, HEAD_DIM, 2 * HEAD_DIM), lambda i, s, pt: (0, 0, 0))
    osp = pl.BlockSpec((None, KV_GROUPS, pages * chunks, 2 * HEAD_DIM), lambda i, s, pt: (i, 0, s, 0))
    osh = jax.ShapeDtypeStruct((b, KV_GROUPS, n_pages * chunks, 2 * HEAD_DIM), F32)
    return pl.pallas_call(
        functools.partial(_cmp_paged_kernel, pages=pages),
        grid_spec=pltpu.PrefetchScalarGridSpec(
            num_scalar_prefetch=1,
            grid=(b, n_pages // pages),
            in_specs=_page_specs(page_rows, pages) * 2 + [wsp, wsp],
            out_specs=[osp, osp]),
        out_shape=[osh, osh],
        compiler_params=_cparams(("parallel", "arbitrary")),
        name="cmp_paged",
    )(page_table, *([pool_k] * pages), *([pool_v] * pages), wk, wv)


def _nsa_sample_kernel(pt_ref, q_ref, ck_ref, cv_ref, a_ref, ksn_ref, vsn_ref, wk_ref, wv_ref, kwn_ref, vwn_ref,
                       gt_ref, *rest, pages, page_size, n_blk, n_cmp, n_pick, past_len):
    kp = rest[:pages]
    vp = rest[pages:2 * pages]
    o_ref, wko_ref, wvo_ref, sel_scr, oc_scr, m_scr, l_scr, acc_scr, q_scr = rest[2 * pages:]
    step = pl.program_id(1)
    last = pl.num_programs(1) - 1
    qpos = past_len
    nbp = sel_scr.shape[2]
    rowi = lax.broadcasted_iota(jnp.int32, (SUBLANES, 1), 0)
    lane = lax.broadcasted_iota(jnp.int32, (1, LANES), 1)
    blk = lax.broadcasted_iota(jnp.int32, (1, nbp), 1)

    def slopes(g):
        sl = jnp.zeros((SUBLANES, 1), F32)
        for h in range(HEADS_PER_GROUP):
            sl = jnp.where(rowi == h, 2.0 ** -(HEADS_PER_GROUP * g + h + 1), sl)
        return sl

    @pl.when(step == 0)
    def _():
        qrow = q_ref[...]
        ncp = ck_ref.shape[1]
        cidx = lax.broadcasted_iota(jnp.int32, (1, ncp), 1)
        dist = jnp.broadcast_to(qpos - (cidx * CMP_STRIDE + (CMP_BLOCK - 1)), (SUBLANES, ncp))
        mask = (dist >= 0) & (cidx < n_cmp)
        cur = qpos // SEL_BLOCK
        valid = blk <= cur
        forced = (blk == 0) | (blk == cur) | (blk == cur - 1)
        for g in range(KV_GROUPS):
            heads = [qrow[:, (HEADS_PER_GROUP * g + h) * HEAD_DIM:(HEADS_PER_GROUP * g + h + 1) * HEAD_DIM]
                     for h in range(HEADS_PER_GROUP)]
            q8 = jnp.concatenate(heads + [jnp.zeros((SUBLANES - HEADS_PER_GROUP, HEAD_DIM), BF16)], axis=0)
            q_scr[g] = q8
            p = _masked_softmax(_dot_nt(q8, ck_ref[g]) - slopes(g) * dist.astype(F32), mask)
            oc_scr[g] = _dot(p.astype(BF16), cv_ref[g])
            p_sum = p[0:1]
            for h in range(1, HEADS_PER_GROUP):
                p_sum = p_sum + p[h:h + 1]
            p_sum = jnp.broadcast_to(p_sum, (SUBLANES, ncp))
            imp = sum(_dot(t, a_ref[...]) for t in _split3(p_sum))
            score = jnp.where(forced & valid, FORCE_SCORE, jnp.where(valid, imp, -FORCE_SCORE))
            score = jnp.where(blk < n_blk, score, -2.0 * FORCE_SCORE)

            def rank_body(j, cnt):
                sj = jnp.max(jnp.where(blk == j, score, -jnp.inf), axis=-1, keepdims=True)
                return cnt + jnp.where((sj > score) | ((sj == score) & (j < blk)), 1.0, 0.0)

            rank = lax.fori_loop(0, n_blk, rank_body, jnp.zeros((SUBLANES, nbp), F32))
            sel_scr[g] = jnp.where((rank < n_pick) & (blk < n_blk), 1.0, 0.0)
            m_scr[g] = jnp.full((SUBLANES, LANES), NEG_INF, F32)
            l_scr[g] = jnp.zeros((SUBLANES, LANES), F32)
            acc_scr[g] = jnp.zeros((SUBLANES, HEAD_DIM), F32)

    blocks_per_page = page_size // SEL_BLOCK
    for g in range(KV_GROUPS):
        q8 = q_scr[g]
        selg = sel_scr[g]
        sl = slopes(g)
        scores, masks = [], []
        for i in range(pages):
            pg = step * pages + i
            k = kp[i][pl.ds(g, page_size, stride=PAGE_ROWS_PER_TOKEN), :].astype(BF16)
            dist = qpos - (pg * page_size + lane)
            picked = jnp.zeros((SUBLANES, LANES), F32)
            for r in range(blocks_per_page):
                sr = jnp.max(jnp.where(blk == blocks_per_page * pg + r, selg, 0.0), axis=-1, keepdims=True)
                picked = jnp.where(lane // SEL_BLOCK == r, sr, picked)
            mask = (dist >= 0) & (picked > 0.5)
            scores.append(jnp.where(mask, _dot_nt(q8, k) - sl * dist.astype(F32), NEG_INF))
            masks.append(mask)
        s = jnp.concatenate(scores, axis=1)
        mask = jnp.concatenate(masks, axis=1)
        v = jnp.concatenate([vp[i][pl.ds(g, page_size, stride=PAGE_ROWS_PER_TOKEN), :].astype(BF16)
                             for i in range(pages)], axis=0)
        m_old = m_scr[g][:, 0:1]
        m_new = jnp.maximum(m_old, jnp.max(s, axis=-1, keepdims=True))
        p = jnp.where(mask, jnp.exp(s - m_new), 0.0)
        alpha = jnp.exp(m_old - m_new)
        acc_scr[g] = alpha * acc_scr[g] + _dot(p.astype(BF16), v)
        l_scr[g] = jnp.broadcast_to(alpha * l_scr[g][:, 0:1] + jnp.sum(p, axis=-1, keepdims=True), (SUBLANES, LANES))
        m_scr[g] = jnp.broadcast_to(m_new, (SUBLANES, LANES))

    @pl.when(step == last)
    def _():
        wb = wk_ref.shape[0]
        wrow = lax.broadcasted_iota(jnp.int32, (wb, 1), 0)
        wk_new = jnp.where(wrow == wb - 1, kwn_ref[...], pltpu.roll(wk_ref[...], wb - 1, 0))
        wv_new = jnp.where(wrow == wb - 1, vwn_ref[...], pltpu.roll(wv_ref[...], wb - 1, 0))
        wko_ref[...] = wk_new
        wvo_ref[...] = wv_new
        widx = lax.broadcasted_iota(jnp.int32, (1, wb), 1)
        wdist = jnp.broadcast_to(wb - 1 - widx, (SUBLANES, wb))
        wmask = (wdist >= 0) & (wdist < WINDOW) & (qpos - wdist >= 0)
        for g in range(KV_GROUPS):
            gs = slice(g * HEAD_DIM, (g + 1) * HEAD_DIM)
            q8 = q_scr[g]
            sl = slopes(g)
            kn = ksn_ref[:, gs].astype(BF16).astype(F32)
            vn = vsn_ref[:, gs].astype(BF16).astype(F32)
            s_n = jnp.sum(q8.astype(F32) * kn, axis=-1, keepdims=True)
            sel_n = jnp.max(jnp.where(blk == n_blk - 1, sel_scr[g], 0.0), axis=-1, keepdims=True) > 0.5
            m_old = m_scr[g][:, 0:1]
            m_new = jnp.maximum(m_old, jnp.where(sel_n, s_n, NEG_INF))
            p_n = jnp.where(sel_n, jnp.exp(s_n - m_new), 0.0)
            alpha = jnp.exp(m_old - m_new)
            acc = alpha * acc_scr[g] + p_n.astype(BF16).astype(F32) * vn
            l = alpha * l_scr[g][:, 0:1] + p_n
            o_s = acc / jnp.maximum(l, TINY)
            p = _masked_softmax(_dot_nt(q8, wk_new[:, gs].astype(BF16)) - sl * wdist.astype(F32), wmask)
            o_w = _dot(p.astype(BF16), wv_new[:, gs].astype(BF16))
            gt = jnp.broadcast_to(gt_ref[:, g * LANES:(g + 1) * LANES], (SUBLANES, LANES))
            gcol = [jnp.sum(jnp.where(lane == k * HEADS_PER_GROUP + rowi, gt, 0.0), axis=-1, keepdims=True)
                    for k in range(3)]
            o = gcol[0] * oc_scr[g] + gcol[1] * o_s + gcol[2] * o_w
            for h in range(HEADS_PER_GROUP):
                hd = HEADS_PER_GROUP * g + h
                o_ref[:, hd * HEAD_DIM:(hd + 1) * HEAD_DIM] = o[h:h + 1]


def _nsa_sample(qbf, ck, cv, ks_new, vs_new, win_k, win_v, kw_new, vw_new, gates, pool_k, pool_v, page_table,
                page_size):
    b, n_pages = page_table.shape
    past_len = n_pages * page_size
    ncp = ck.shape[2]
    n_cmp = (past_len + 1 + CMP_STRIDE - 1) // CMP_STRIDE - 1
    n_blk = (past_len + 1 + SEL_BLOCK - 1) // SEL_BLOCK
    nbp = -(-n_blk // LANES) * LANES
    wb = win_k.shape[1]
    assert wb == WINDOW and n_blk >= TOP_N and ncp == n_cmp and page_size % SEL_BLOCK == 0
    pages = PAGES_PER_STEP if n_pages % PAGES_PER_STEP == 0 else n_pages
    page_rows = page_size * PAGE_ROWS_PER_TOKEN
    a = _imp_matrix(ncp, nbp)
    tok = lambda c: pl.BlockSpec((None, 1, c), lambda i, s, pt: (i, 0, 0))
    csp = pl.BlockSpec((None, KV_GROUPS, ncp, HEAD_DIM), lambda i, s, pt: (i, 0, 0, 0))
    wsp = pl.BlockSpec((None, wb, KV_WIDTH), lambda i, s, pt: (i, 0, 0))
    asp = pl.BlockSpec(a.shape, lambda i, s, pt: (0, 0))
    grp_scr = lambda c, dt: pltpu.VMEM((KV_GROUPS, SUBLANES, c), dt)
    return pl.pallas_call(
        functools.partial(_nsa_sample_kernel, pages=pages, page_size=page_size, n_blk=n_blk, n_cmp=n_cmp,
                          n_pick=TOP_N, past_len=past_len),
        grid_spec=pltpu.PrefetchScalarGridSpec(
            num_scalar_prefetch=1,
            grid=(b, n_pages // pages),
            in_specs=[tok(NSA_WIDTH), csp, csp, asp, tok(KV_WIDTH), tok(KV_WIDTH), wsp, wsp, tok(KV_WIDTH),
                      tok(KV_WIDTH), tok(GATE_COLS)] + _page_specs(page_rows, pages) * 2,
            out_specs=[tok(NSA_WIDTH), wsp, wsp],
            scratch_shapes=[grp_scr(nbp, F32), grp_scr(HEAD_DIM, F32), grp_scr(LANES, F32), grp_scr(LANES, F32),
                            grp_scr(HEAD_DIM, F32), grp_scr(HEAD_DIM, BF16)]),
        out_shape=[jax.ShapeDtypeStruct((b, 1, NSA_WIDTH), F32), jax.ShapeDtypeStruct((b, wb, KV_WIDTH), F32),
                   jax.ShapeDtypeStruct((b, wb, KV_WIDTH), F32)],
        compiler_params=_cparams(("parallel", "arbitrary")),
        name="nsa_sample",
    )(page_table, qbf, ck, cv, a, ks_new, vs_new, win_k, win_v, kw_new, vw_new, gates,
      *([pool_k] * pages), *([pool_v] * pages))


def kernel(x_prompt, x_sample, p_prompt, p_sample, cache_cmp_k, cache_cmp_v, cache_sel_k, cache_sel_v, cache_win_k, cache_win_v, state_ssm_re, state_ssm_im, page_table, norm_mix, w_in, cmp_pe_k, cmp_w1_k, cmp_w2_k, cmp_pe_v, cmp_w1_v, cmp_w2_v, ssm_a_re, ssm_a_im, ssm_log_dt, ssm_b_re, ssm_b_im, ssm_c_re, ssm_c_im, ssm_d, w_glu, b_glu, norm_nsa_out, norm_ssm_out, w_out, norm_ffn, router_grp_w, router_grp_b, router_exp_w, router_exp_b, exp_w1, exp_w3, exp_w2, norm_ple, w_ple_gate, b_ple_gate, w_ple_proj, norm_final):
    depth = w_in.shape[0]
    assert depth == 1, "one layer per step"
    i = 0
    b, s, d = x_prompt.shape
    db, t, _ = x_sample.shape
    assert t == 1, "the sample group decodes one token per sequence"
    ssm_ch = w_glu.shape[1]
    page_size = cache_cmp_k.shape[2]
    n_phys = cache_cmp_k.shape[1]
    wp = min(WINDOW, s)
    row1 = lambda a: a.reshape(1, -1)

    w_inp = _prep_w_in(w_in[i], ssm_ch)
    wk1 = _prep_cmp_w1(cmp_w1_k[i])
    wv1 = _prep_cmp_w1(cmp_w1_v[i])
    s5p = _prep_s5(ssm_a_re[i], ssm_a_im[i], ssm_log_dt[i], ssm_b_re[i], ssm_b_im[i], ssm_c_re[i], ssm_c_im[i])
    router = _prep_router(router_grp_w[i], router_grp_b[i], router_exp_w[i], router_exp_b[i])
    wglu = w_glu[i].astype(BF16)
    wout = w_out[i].astype(BF16)
    wpg = w_ple_gate[i].astype(BF16)
    wpp = w_ple_proj[i].astype(BF16)
    cmp_par = (cmp_pe_k[i], cmp_pe_v[i], cmp_w1_k[i], cmp_w1_v[i], cmp_w2_k[i], cmp_w2_v[i])

    def tail1(h, o_nsa, o_ssm):
        return _tail1(o_nsa, o_ssm, h, row1(norm_nsa_out[i]), row1(norm_ssm_out[i]), wout, row1(norm_ffn[i]), router)

    def tail2(pos, tok_off, h1, rw, y, p):
        return _tail2(pos, tok_off, h1, rw, y, p, row1(norm_ple[i]), wpg, row1(b_ple_gate[i]), wpp, row1(norm_final))

    m = b * s
    hp = x_prompt.reshape(m, d)
    qbf, kc, vc, ks, vs, kw, vw, ksb, vsb, kwb, vwb, u, gates = _inproj(hp, row1(norm_mix[i]), w_inp, ssm_ch, 512)
    seq = lambda a: a.reshape(b, s, a.shape[-1])
    lk, lv = _cmp_prompt(seq(kc), seq(vc), wk1, wv1)
    ck, cv = _cmp_finish(lk, lv, *cmp_par)
    o_nsa = _nsa_prompt(seq(qbf), ck, cv, seq(ksb), seq(vsb), seq(kwb), seq(vwb), seq(gates))
    ns = s5p[0].shape[1]
    zero = jnp.zeros((b, 1, ns), F32)
    o_ssm, hre_p, him_p = _s5(seq(u), zero, zero, s5p, row1(ssm_d[i]), wglu, row1(b_glu[i]), scan=True, precise=False)
    h1_p, xn_p, rid_p, rw_p = tail1(hp, o_nsa.reshape(m, -1), o_ssm.reshape(m, -1))
    kv5 = lambda a: a.reshape(1, b, s, KV_GROUPS, HEAD_DIM)
    win5 = lambda a: a.reshape(b, s, KV_GROUPS, HEAD_DIM)[None, :, s - wp:]
    st4 = lambda a, n: a.reshape(1, n, ns // SSM_STATE, SSM_STATE)
    prompt_state = (kv5(kc), kv5(vc), kv5(ks), kv5(vs), win5(kw), win5(vw), st4(hre_p, b), st4(him_p, b))

    hs = x_sample.reshape(db, d)
    qbf, kc, vc, ks, vs, kw, vw, _, _, _, _, u, gates = _inproj(hs, row1(norm_mix[i]), w_inp, ssm_ch, 512)
    pool = lambda c: c[i].reshape(n_phys * page_size * KV_GROUPS, HEAD_DIM)
    one = lambda a: a.reshape(db, 1, a.shape[-1])
    lk, lv = _cmp_paged(pool(cache_cmp_k), pool(cache_cmp_v), page_table, wk1, wv1, page_size)
    ck, cv = _cmp_finish(lk, lv, *cmp_par, kn=one(kc), vn=one(vc))
    wb = cache_win_k.shape[2]
    o_nsa, win_k, win_v = _nsa_sample(one(qbf), ck, cv, one(ks), one(vs), cache_win_k[i].reshape(db, wb, KV_WIDTH),
                                      cache_win_v[i].reshape(db, wb, KV_WIDTH), one(kw), one(vw), one(gates),
                                      pool(cache_sel_k), pool(cache_sel_v), page_table, page_size)
    o_ssm, hre_s, him_s = _s5(u, state_ssm_re[i].reshape(db, ns), state_ssm_im[i].reshape(db, ns), s5p,
                              row1(ssm_d[i]), wglu, row1(b_glu[i]), scan=False, precise=True)
    h1_s, xn_s, rid_s, rw_s = tail1(hs, o_nsa.reshape(db, -1), o_ssm)

    rid = jnp.concatenate([rid_p[:, :2], rid_s[:, :2]], axis=0)
    pos, tile_expert, row_token, n_tiles = _moe_dispatch(rid)
    y = _moe_sparse(jnp.concatenate([xn_p, xn_s], axis=0), tile_expert, row_token, n_tiles,
                    exp_w1[i], exp_w3[i], exp_w2[i])
    y_prompt = tail2(pos, 0, h1_p, rw_p, y, p_prompt[i].reshape(m, -1)).reshape(b, s, d)
    y_sample = tail2(pos, m, h1_s, rw_s, y, p_sample[i].reshape(db, -1)).reshape(db, 1, d)
    new5 = lambda a: a.reshape(1, db, 1, KV_GROUPS, HEAD_DIM)
    buf5 = lambda a: a.reshape(1, db, wb, KV_GROUPS, HEAD_DIM)
    sample_state = (new5(kc), new5(vc), new5(ks), new5(vs), buf5(win_k), buf5(win_v), st4(hre_s, db), st4(him_s, db))
    return (y_prompt, y_sample) + prompt_state + sample_state
```

```python
import functools

import jax
import jax.numpy as jnp
import numpy as np
from jax import lax
from jax.experimental import pallas as pl
from jax.experimental.pallas import tpu as pltpu

F32 = jnp.float32
BF16 = jnp.bfloat16

HEAD_DIM = 128
N_HEADS = 8
KV_GROUPS = 2
HEADS_PER_GROUP = N_HEADS // KV_GROUPS
NSA_WIDTH = N_HEADS * HEAD_DIM
KV_WIDTH = KV_GROUPS * HEAD_DIM
CMP_STRIDE = 16
CMP_BLOCK = 2 * CMP_STRIDE
SEL_BLOCK = 64
TOP_N = 16
WINDOW = 512
Q_BLOCK = 128
SSM_GROUP_CH = 16
SSM_STATE = 64
N_EXPERT_GROUPS = 4
EXPERTS_PER_GROUP = 8
N_EXPERTS = N_EXPERT_GROUPS * EXPERTS_PER_GROUP
RMS_EPS = 1e-6
NEG_INF = -1e30
FORCE_SCORE = 1e4
TINY = 1e-30

LANES = 128
SUBLANES = 8
GATE_COLS = KV_GROUPS * LANES
ROUTER_COLS = LANES
ROUTER_EXP_OFF = N_EXPERT_GROUPS
SEL_KV_TILE = 512
VMEM_LIMIT_BYTES = 56 * 1024 * 1024


def _cparams(sem):
    return pltpu.CompilerParams(dimension_semantics=sem, vmem_limit_bytes=VMEM_LIMIT_BYTES)


def _rms(x, g):
    return x * lax.rsqrt(jnp.mean(x * x, axis=-1, keepdims=True) + RMS_EPS) * g


def _gelu(x):
    return x * (0.5 * (1.0 + jnp.tanh(0.7978845608028654 * (x + 0.044715 * (x * x * x)))))


def _dot(a, b):
    return jnp.dot(a, b, preferred_element_type=F32)


def _dot_nt(a, b):
    return lax.dot_general(a, b, (((1,), (1,)), ((), ())), preferred_element_type=F32)


def _split2(x):
    hi = x.astype(BF16)
    lo = (x - hi.astype(F32)).astype(BF16)
    return hi, lo


def _split3(x):
    hi = x.astype(BF16)
    r = x - hi.astype(F32)
    mid = r.astype(BF16)
    lo = (r - mid.astype(F32)).astype(BF16)
    return hi, mid, lo


def _masked_softmax(s, mask):
    s = jnp.where(mask, s, NEG_INF)
    m = jnp.max(s, axis=-1, keepdims=True)
    p = jnp.where(mask, jnp.exp(s - m), 0.0)
    return p / jnp.maximum(jnp.sum(p, axis=-1, keepdims=True), TINY)


def _row_tile(m, pref):
    return pref if m % pref == 0 else m


Q_END = NSA_WIDTH
KV_END = Q_END + 6 * KV_WIDTH
U_OFF = KV_END


def _inproj_kernel(x_ref, g_ref, w_ref, qbf, kc, vc, ks, vs, kw, vw, ksb, vsb, kwb, vwb, u, gates, *, ssm_ch):
    xn = _rms(x_ref[...], g_ref[...]).astype(BF16)

    def mm(lo, hi):
        return _dot(xn, w_ref[:, lo:hi])

    qbf[...] = (mm(0, Q_END) * (HEAD_DIM ** -0.5)).astype(BF16)
    for i, (f, b) in enumerate(((kc, None), (vc, None), (ks, ksb), (vs, vsb), (kw, kwb), (vw, vwb))):
        z = mm(Q_END + KV_WIDTH * i, Q_END + KV_WIDTH * (i + 1))
        f[...] = z
        if b is not None:
            b[...] = z.astype(BF16)
    u[...] = mm(U_OFF, U_OFF + ssm_ch)
    gates[...] = jax.nn.sigmoid(mm(U_OFF + ssm_ch, U_OFF + ssm_ch + GATE_COLS))


def _prep_w_in(w_in, ssm_ch):
    wq = w_in[:, :KV_END]
    wg = w_in[:, KV_END:KV_END + 3 * N_HEADS]
    wu = w_in[:, KV_END + 3 * N_HEADS:]
    d = w_in.shape[0]
    wg = wg.reshape(d, KV_GROUPS, HEADS_PER_GROUP, 3).transpose(0, 1, 3, 2).reshape(d, KV_GROUPS, 3 * HEADS_PER_GROUP)
    wg = jnp.pad(wg, ((0, 0), (0, 0), (0, LANES - 3 * HEADS_PER_GROUP))).reshape(d, GATE_COLS)
    return jnp.concatenate([wq, wu, wg], axis=1).astype(BF16)


def _inproj(x, g, w, ssm_ch, tm_pref):
    m, d = x.shape
    tm = _row_tile(m, tm_pref)
    ncols = w.shape[1]
    row = lambda c: pl.BlockSpec((tm, c), lambda i: (i, 0))
    f32s = lambda c: jax.ShapeDtypeStruct((m, c), F32)
    bfs = lambda c: jax.ShapeDtypeStruct((m, c), BF16)
    out_shape = ([bfs(NSA_WIDTH)] + [f32s(KV_WIDTH)] * 6 + [bfs(KV_WIDTH)] * 4 + [f32s(ssm_ch), f32s(GATE_COLS)])
    out_specs = ([row(NSA_WIDTH)] + [row(KV_WIDTH)] * 10 + [row(ssm_ch), row(GATE_COLS)])
    return pl.pallas_call(
        functools.partial(_inproj_kernel, ssm_ch=ssm_ch),
        grid=(m // tm,),
        in_specs=[row(d), pl.BlockSpec((1, d), lambda i: (0, 0)),
                  pl.BlockSpec((d, ncols), lambda i: (0, 0), pipeline_mode=pl.Buffered(1))],
        out_specs=out_specs,
        out_shape=out_shape,
        compiler_params=_cparams(("parallel",)),
        name="inproj",
    )(x, g, w)


def _prep_cmp_w1(w1):
    return jnp.concatenate([w1[:CMP_STRIDE], w1[CMP_STRIDE:]], axis=-1).astype(BF16)


def _cmp_prompt_kernel(kc_ref, vc_ref, wk_ref, wv_ref, ok_ref, ov_ref, *, n_ch):
    for src, w, dst in ((kc_ref, wk_ref, ok_ref), (vc_ref, wv_ref, ov_ref)):
        acc = jnp.zeros((n_ch, 2 * HEAD_DIM), F32)
        for j in range(CMP_STRIDE):
            xj = src[pl.ds(j, n_ch, stride=CMP_STRIDE), :]
            acc = acc + _dot(xj.astype(BF16), w[j])
        dst[...] = acc


def _cmp_prompt(kc, vc, wk, wv):
    b, s, _ = kc.shape
    n_ch = s // CMP_STRIDE
    tok = pl.BlockSpec((None, s, HEAD_DIM), lambda i, g: (i, 0, g))
    wsp = pl.BlockSpec((CMP_STRIDE, HEAD_DIM, 2 * HEAD_DIM), lambda i, g: (0, 0, 0))
    osp = pl.BlockSpec((None, None, n_ch, 2 * HEAD_DIM), lambda i, g: (i, g, 0, 0))
    osh = jax.ShapeDtypeStruct((b, KV_GROUPS, n_ch, 2 * HEAD_DIM), F32)
    return pl.pallas_call(
        functools.partial(_cmp_prompt_kernel, n_ch=n_ch),
        grid=(b, KV_GROUPS),
        in_specs=[tok, tok, wsp, wsp],
        out_specs=[osp, osp],
        out_shape=[osh, osh],
        compiler_params=_cparams(("parallel", "parallel")),
        name="cmp_prompt",
    )(kc, vc, wk, wv)


def _cmp_finish_kernel(*refs, nch, has_tail):
    if has_tail:
        lk, lv, kn, vn, pek, pev, w1k, w1v, w2k, w2v, ock, ocv = refs
    else:
        lk, lv, pek, pev, w1k, w1v, w2k, w2v, ock, ocv = refs
        kn = vn = None
    row = lax.broadcasted_iota(jnp.int32, (nch, 1), 0)
    half = CMP_STRIDE * HEAD_DIM
    for l_ref, n_ref, pe_ref, w1_ref, w2_ref, o_ref in ((lk, kn, pek, w1k, w2k, ock), (lv, vn, pev, w1v, w2v, ocv)):
        lohi = l_ref[...]
        lo = lohi[:, :HEAD_DIM]
        hi = lohi[:, HEAD_DIM:]
        pe = jnp.broadcast_to(pe_ref[0:1, :], (SUBLANES, half)).astype(BF16)
        b_lo = _dot(pe, w1_ref[0:half, :].astype(BF16))[0:1]
        pe = jnp.broadcast_to(pe_ref[1:2, :], (SUBLANES, half)).astype(BF16)
        b_hi = _dot(pe, w1_ref[half:2 * half, :].astype(BF16))[0:1]
        hi_next = pltpu.roll(hi, nch - 1, 0)
        if has_tail:
            new = jnp.broadcast_to(n_ref[...], (SUBLANES, HEAD_DIM)).astype(BF16)
            tail = _dot(new, w1_ref[half:half + HEAD_DIM, :].astype(BF16))[0:1]
            hi_next = jnp.where(row == nch - 1, tail, hi_next)
        hid = _gelu(lo + hi_next + (b_lo + b_hi))
        o_ref[...] = _dot(hid.astype(BF16), w2_ref[...].astype(BF16)).astype(BF16)


def _cmp_finish(lk, lv, pek, pev, w1k, w1v, w2k, w2v, kn=None, vn=None):
    b, _, nch, _ = lk.shape
    has_tail = kn is not None
    lsp = pl.BlockSpec((None, None, nch, 2 * HEAD_DIM), lambda i, g: (i, g, 0, 0))
    full = lambda a: pl.BlockSpec(a.shape, lambda i, g: (0,) * a.ndim)
    pek2 = pek.reshape(2, CMP_STRIDE * HEAD_DIM)
    pev2 = pev.reshape(2, CMP_STRIDE * HEAD_DIM)
    w1k2 = w1k.reshape(CMP_BLOCK * HEAD_DIM, HEAD_DIM)
    w1v2 = w1v.reshape(CMP_BLOCK * HEAD_DIM, HEAD_DIM)
    args = [lk, lv]
    specs = [lsp, lsp]
    if has_tail:
        nsp = pl.BlockSpec((None, 1, HEAD_DIM), lambda i, g: (i, 0, g))
        args += [kn, vn]
        specs += [nsp, nsp]
    params = [pek2, pev2, w1k2, w1v2, w2k, w2v]
    args += params
    specs += [full(a) for a in params]
    osp = pl.BlockSpec((None, None, nch, HEAD_DIM), lambda i, g: (i, g, 0, 0))
    osh = jax.ShapeDtypeStruct((b, KV_GROUPS, nch, HEAD_DIM), BF16)
    return pl.pallas_call(
        functools.partial(_cmp_finish_kernel, nch=nch, has_tail=has_tail),
        grid=(b, KV_GROUPS),
        in_specs=specs,
        out_specs=[osp, osp],
        out_shape=[osh, osh],
        compiler_params=_cparams(("parallel", "parallel")),
        name="cmp_finish",
    )(*args)


def _imp_matrix(n_cmp_pad, n_blk_pad):
    ratio = SEL_BLOCK // CMP_STRIDE
    c = np.arange(n_cmp_pad)[:, None]
    j = np.arange(n_blk_pad)[None, :]
    return jnp.asarray(((c >= ratio * j - 1) & (c <= ratio * j + ratio - 1)).astype(np.float32), dtype=BF16)


def _expand_matrix(n_blk_pad, n_keys):
    j = np.arange(n_blk_pad)[:, None]
    k = np.arange(n_keys)[None, :]
    return jnp.asarray((k // SEL_BLOCK == j).astype(np.float32), dtype=BF16)


def _group_slopes(g):
    sg = jnp.where(g == 0, 1.0, 2.0 ** -HEADS_PER_GROUP).astype(F32)
    return [jnp.full((Q_BLOCK, 1), 2.0 ** -(h + 1), F32) * sg for h in range(HEADS_PER_GROUP)]


def _nsa_prompt_kernel(q_ref, ck_ref, cv_ref, ks_ref, vs_ref, kw_ref, vw_ref, gt_ref, a_ref, e_ref, o_ref, sk_ref,
                       *, seq, n_cmp, n_blk, n_pick):
    g = pl.program_id(1)
    n = pl.program_id(2)
    rows = HEADS_PER_GROUP * Q_BLOCK
    qb = q_ref[...]
    q = jnp.concatenate([qb[:, h * HEAD_DIM:(h + 1) * HEAD_DIM] for h in range(HEADS_PER_GROUP)], axis=0)
    qpos1 = n * Q_BLOCK + lax.broadcasted_iota(jnp.int32, (Q_BLOCK, 1), 0)
    qpos = jnp.concatenate([qpos1] * HEADS_PER_GROUP, axis=0)
    slope = jnp.concatenate(_group_slopes(g), axis=0)

    ncp = ck_ref.shape[0]
    cidx = lax.broadcasted_iota(jnp.int32, (1, ncp), 1)
    dist = qpos - (cidx * CMP_STRIDE + (CMP_BLOCK - 1))
    mask = (dist >= 0) & (cidx < n_cmp)
    p = _masked_softmax(_dot_nt(q, ck_ref[...]) - slope * dist.astype(F32), mask)
    o_c = _dot(p.astype(BF16), cv_ref[...])
    p_sum = p[0:Q_BLOCK]
    for h in range(1, HEADS_PER_GROUP):
        p_sum = p_sum + p[h * Q_BLOCK:(h + 1) * Q_BLOCK]

    imp = sum(_dot(t, a_ref[...]) for t in _split3(p_sum))
    blk = lax.broadcasted_iota(jnp.int32, (1, LANES), 1)
    cur = qpos1 // SEL_BLOCK
    valid = blk <= cur
    forced = (blk == 0) | (blk == cur) | (blk == cur - 1)
    score = jnp.where(forced & valid, FORCE_SCORE, jnp.where(valid, imp, -FORCE_SCORE))
    score_t = score.T[:n_blk]
    bidx = lax.broadcasted_iota(jnp.int32, (n_blk, 1), 0)
    rank = jnp.zeros((n_blk, Q_BLOCK), F32)
    for j in range(n_blk):
        r = score_t[j:j + 1, :]
        rank = rank + jnp.where((r > score_t) | ((r == score_t) & (j < bidx)), 1.0, 0.0)
    sel_t = jnp.where(rank < n_pick, 1.0, 0.0)
    if n_blk < LANES:
        sel_t = jnp.concatenate([sel_t, jnp.zeros((LANES - n_blk, Q_BLOCK), F32)], axis=0)
    sk_ref[...] = _dot(sel_t.T.astype(BF16), e_ref[...])

    tk = min(SEL_KV_TILE, seq)
    n_tiles = (n * Q_BLOCK + Q_BLOCK + tk - 1) // tk

    def body(t, carry):
        m, l, acc = carry
        k0 = pl.multiple_of(t * tk, tk)
        kpos = k0 + lax.broadcasted_iota(jnp.int32, (1, tk), 1)
        dist = qpos - kpos
        picked = sk_ref[:, pl.ds(k0, tk)] > 0.5
        mask = (dist >= 0) & jnp.concatenate([picked] * HEADS_PER_GROUP, axis=0)
        s = _dot_nt(q, ks_ref[pl.ds(k0, tk), :]) - slope * dist.astype(F32)
        s = jnp.where(mask, s, NEG_INF)
        m_new = jnp.maximum(m, jnp.max(s, axis=-1, keepdims=True))
        p = jnp.where(mask, jnp.exp(s - m_new), 0.0)
        alpha = jnp.exp(m - m_new)
        l = alpha * l + jnp.sum(p, axis=-1, keepdims=True)
        acc = alpha * acc + _dot(p.astype(BF16), vs_ref[pl.ds(k0, tk), :])
        return m_new, l, acc

    init = (jnp.full((rows, 1), NEG_INF, F32), jnp.zeros((rows, 1), F32), jnp.zeros((rows, HEAD_DIM), F32))
    _, l, acc = lax.fori_loop(0, n_tiles, body, init)
    o_s = acc / jnp.maximum(l, TINY)

    band = min(WINDOW + Q_BLOCK, seq)
    st = pl.multiple_of(jnp.minimum(jnp.maximum(n * Q_BLOCK - WINDOW, 0), seq - band), Q_BLOCK)
    kpos = st + lax.broadcasted_iota(jnp.int32, (1, band), 1)
    dist = qpos - kpos
    mask = (dist >= 0) & (dist < WINDOW)
    p = _masked_softmax(_dot_nt(q, kw_ref[pl.ds(st, band), :]) - slope * dist.astype(F32), mask)
    o_w = _dot(p.astype(BF16), vw_ref[pl.ds(st, band), :])

    gt = gt_ref[...]
    for h in range(HEADS_PER_GROUP):
        sl = slice(h * Q_BLOCK, (h + 1) * Q_BLOCK)
        o_ref[:, h * HEAD_DIM:(h + 1) * HEAD_DIM] = (
            gt[:, h:h + 1] * o_c[sl]
            + gt[:, HEADS_PER_GROUP + h:HEADS_PER_GROUP + h + 1] * o_s[sl]
            + gt[:, 2 * HEADS_PER_GROUP + h:2 * HEADS_PER_GROUP + h + 1] * o_w[sl])


def _nsa_prompt(qbf, ck, cv, ksb, vsb, kwb, vwb, gates):
    b, s, _ = qbf.shape
    ncp = ck.shape[2]
    n_cmp = s // CMP_STRIDE - 1
    n_blk = s // SEL_BLOCK
    assert s % Q_BLOCK == 0 and TOP_N <= n_blk <= LANES
    a = _imp_matrix(ncp, LANES)
    e = _expand_matrix(LANES, s)
    gw = HEADS_PER_GROUP * HEAD_DIM
    qsp = pl.BlockSpec((None, Q_BLOCK, gw), lambda i, g, n: (i, n, g))
    csp = pl.BlockSpec((None, None, ncp, HEAD_DIM), lambda i, g, n: (i, g, 0, 0))
    ksp = pl.BlockSpec((None, s, HEAD_DIM), lambda i, g, n: (i, 0, g))
    gsp = pl.BlockSpec((None, Q_BLOCK, LANES), lambda i, g, n: (i, n, g))
    full = lambda x: pl.BlockSpec(x.shape, lambda i, g, n: (0,) * x.ndim)
    return pl.pallas_call(
        functools.partial(_nsa_prompt_kernel, seq=s, n_cmp=n_cmp, n_blk=n_blk, n_pick=TOP_N),
        grid=(b, KV_GROUPS, s // Q_BLOCK),
        in_specs=[qsp, csp, csp, ksp, ksp, ksp, ksp, gsp, full(a), full(e)],
        out_specs=qsp,
        out_shape=jax.ShapeDtypeStruct((b, s, NSA_WIDTH), F32),
        scratch_shapes=[pltpu.VMEM((Q_BLOCK, s), F32)],
        compiler_params=_cparams(("parallel", "parallel", "arbitrary")),
        name="nsa_prompt",
    )(qbf, ck, cv, ksb, vsb, kwb, vwb, gates, a, e)


SSM_LANE_GROUPS = LANES // SSM_GROUP_CH
SSM_STATE_TILE = SSM_LANE_GROUPS * SSM_STATE


def _prep_s5(a_re, a_im, log_dt, b_re, b_im, c_re, c_im):
    ng = a_re.shape[0]
    nk = ng // SSM_LANE_GROUPS
    dt = jnp.exp(log_dt)[:, None]
    mag = jnp.exp(a_re * dt)
    lb_re = mag * jnp.cos(a_im * dt)
    lb_im = mag * jnp.sin(a_im * dt)
    den = a_re * a_re + a_im * a_im
    nr = lb_re - 1.0
    f_re = (nr * a_re + lb_im * a_im) / den
    f_im = (lb_im * a_re - nr * a_im) / den
    bb_re = f_re[..., None] * b_re - f_im[..., None] * b_im
    bb_im = f_re[..., None] * b_im + f_im[..., None] * b_re
    eye = jnp.eye(SSM_LANE_GROUPS, dtype=F32)

    def in_map(bb):
        t = bb.reshape(nk, SSM_LANE_GROUPS, SSM_STATE, SSM_GROUP_CH)
        t = jnp.einsum('kgnc,gh->kgchn', t, eye)
        return t.reshape(nk, LANES, SSM_STATE_TILE)

    def out_map(c):
        t = c.reshape(nk, SSM_LANE_GROUPS, SSM_GROUP_CH, SSM_STATE)
        t = jnp.einsum('kgcn,gh->kgnhc', t, eye)
        return t.reshape(nk, SSM_STATE_TILE, LANES)

    bb = jnp.concatenate([in_map(bb_re), in_map(bb_im)], axis=-1)
    cm = jnp.concatenate([out_map(c_re), -out_map(c_im)], axis=1)
    bb_hi = bb.astype(BF16)
    bb_lo = (bb - bb_hi.astype(F32)).astype(BF16)
    lam = jnp.stack([lb_re.reshape(-1), lb_im.reshape(-1)])
    return lam, bb_hi, bb_lo, cm.astype(BF16)


def _s5_kernel(u_ref, h0r_ref, h0i_ref, lam_ref, bbh_ref, bbl_ref, cm_ref, d_ref, wglu_ref, bglu_ref,
               o_ref, hr_ref, hi_ref, *scratch, rows, scan, precise):
    nk = bbh_ref.shape[0]
    st = SSM_STATE_TILE
    if scan:
        cr_scr, ci_scr = scratch
        t = pl.program_id(1)

        @pl.when(t == 0)
        def _():
            cr_scr[...] = h0r_ref[...]
            ci_scr[...] = h0i_ref[...]

        row = lax.broadcasted_iota(jnp.int32, (rows, 1), 0)
    u = u_ref[...]
    ys = []
    for k in range(nk):
        uk = u[:, k * LANES:(k + 1) * LANES]
        if precise:
            uh, ul = _split2(uk)
            bu = _dot(uh, bbh_ref[k]) + _dot(ul, bbh_ref[k]) + _dot(uh, bbl_ref[k])
        else:
            bu = _dot(uk.astype(BF16), bbh_ref[k])
        br = bu[:, :st]
        bi = bu[:, st:]
        ks = slice(k * st, (k + 1) * st)
        lr = lam_ref[0:1, ks]
        li = lam_ref[1:2, ks]
        if scan:
            cr = cr_scr[:, ks]
            ci = ci_scr[:, ks]
            first = row == 0
            br = br + jnp.where(first, lr * cr - li * ci, 0.0)
            bi = bi + jnp.where(first, lr * ci + li * cr, 0.0)
            pr, pi = lr, li
            sh = 1
            while sh < rows:
                sr = jnp.where(row >= sh, pltpu.roll(br, sh, 0), 0.0)
                si = jnp.where(row >= sh, pltpu.roll(bi, sh, 0), 0.0)
                br, bi = br + (pr * sr - pi * si), bi + (pr * si + pi * sr)
                pr, pi = pr * pr - pi * pi, 2.0 * (pr * pi)
                sh *= 2
            cr_scr[:, ks] = br[rows - 1:rows]
            ci_scr[:, ks] = bi[rows - 1:rows]
        else:
            h0r = h0r_ref[:, ks]
            h0i = h0i_ref[:, ks]
            br, bi = br + (lr * h0r - li * h0i), bi + (lr * h0i + li * h0r)
            hr_ref[:, ks] = br
            hi_ref[:, ks] = bi
        hcat = jnp.concatenate([br, bi], axis=1).astype(BF16)
        ys.append(_dot(hcat, cm_ref[k]) + d_ref[:, k * LANES:(k + 1) * LANES] * uk)
    v = _gelu(jnp.concatenate(ys, axis=1))
    o_ref[...] = v * jax.nn.sigmoid(_dot(v.astype(BF16), wglu_ref[...]) + bglu_ref[...])
    if scan:
        hr_ref[...] = cr_scr[...]
        hi_ref[...] = ci_scr[...]


def _s5(u, h0r, h0i, prep, d, wglu, bglu, *, scan, precise, tc_pref=256):
    lam, bbh, bbl, cm = prep
    ns = lam.shape[1]
    full = lambda a: pl.BlockSpec(a.shape, lambda *idx: (0,) * a.ndim)
    params = [lam, bbh, bbl, cm, d, wglu, bglu]
    if scan:
        b, t, c = u.shape
        tc = _row_tile(t, tc_pref)
        grid = (b, t // tc)
        usp = pl.BlockSpec((None, tc, c), lambda i, j: (i, j, 0))
        hsp = pl.BlockSpec((None, 1, ns), lambda i, j: (i, 0, 0))
        rows = tc
        scratch = [pltpu.VMEM((1, ns), F32), pltpu.VMEM((1, ns), F32)]
        sem = ("parallel", "arbitrary")
        osh = [jax.ShapeDtypeStruct((b, t, c), F32)] + [jax.ShapeDtypeStruct((b, 1, ns), F32)] * 2
    else:
        b, c = u.shape
        grid = (1,)
        usp = pl.BlockSpec((b, c), lambda i: (0, 0))
        hsp = pl.BlockSpec((b, ns), lambda i: (0, 0))
        rows = b
        scratch = []
        sem = ("arbitrary",)
        osh = [jax.ShapeDtypeStruct((b, c), F32)] + [jax.ShapeDtypeStruct((b, ns), F32)] * 2
    return pl.pallas_call(
        functools.partial(_s5_kernel, rows=rows, scan=scan, precise=precise),
        grid=grid,
        in_specs=[usp, hsp, hsp] + [full(a) for a in params],
        out_specs=[usp, hsp, hsp],
        out_shape=osh,
        scratch_shapes=scratch,
        compiler_params=_cparams(sem),
        name="s5_scan" if scan else "s5_step",
    )(u, h0r, h0i, *params)


def _prep_router(rg_w, rg_b, re_w, re_b):
    d = rg_w.shape[0]
    w = jnp.concatenate([rg_w, re_w.transpose(1, 0, 2).reshape(d, N_EXPERTS)], axis=1)
    w = jnp.pad(w, ((0, 0), (0, ROUTER_COLS - w.shape[1])))
    b = jnp.pad(jnp.concatenate([rg_b, re_b.reshape(-1)]), (0, ROUTER_COLS - N_EXPERT_GROUPS - N_EXPERTS))
    hi = w.astype(BF16)
    lo = (w - hi.astype(F32)).astype(BF16)
    return hi, lo, b.reshape(1, ROUTER_COLS)


def _tail1_kernel(on_ref, os_ref, h_ref, gn_ref, gs_ref, wout_ref, gf_ref, wrh_ref, wrl_ref, br_ref,
                  h1_ref, xn_ref, rid_ref, rw_ref):
    cat = jnp.concatenate([_rms(on_ref[...], gn_ref[...]), _rms(os_ref[...], gs_ref[...])], axis=1)
    h1 = h_ref[...] + _dot(cat.astype(BF16), wout_ref[...])
    h1_ref[...] = h1
    xn = _rms(h1, gf_ref[...])
    xh, xl = _split2(xn)
    xn_ref[...] = xn
    lg = _dot(xh, wrh_ref[...]) + _dot(xl, wrh_ref[...]) + _dot(xh, wrl_ref[...]) + br_ref[...]
    lane_i = lax.broadcasted_iota(jnp.int32, (1, ROUTER_COLS), 1)
    lane = lane_i.astype(F32)
    ninf = -jnp.inf
    big = float(ROUTER_COLS)
    is_grp = lane_i < N_EXPERT_GROUPS
    lgm = jnp.where(is_grp, lg, ninf)
    mx = jnp.max(lgm, axis=-1, keepdims=True)
    grp = jnp.min(jnp.where(lgm == mx, lane, big), axis=-1, keepdims=True)
    p_grp = 1.0 / jnp.sum(jnp.where(is_grp, jnp.exp(lgm - mx), 0.0), axis=-1, keepdims=True)
    first = ROUTER_EXP_OFF + EXPERTS_PER_GROUP * grp
    lem = jnp.where((lane >= first) & (lane < first + EXPERTS_PER_GROUP), lg, ninf)
    v1 = jnp.max(lem, axis=-1, keepdims=True)
    i1 = jnp.min(jnp.where(lem == v1, lane, big), axis=-1, keepdims=True)
    lem2 = jnp.where(lane == i1, ninf, lem)
    v2 = jnp.max(lem2, axis=-1, keepdims=True)
    i2 = jnp.min(jnp.where(lem2 == v2, lane, big), axis=-1, keepdims=True)
    e2 = jnp.exp(v2 - v1)
    w1 = (1.0 / (1.0 + e2)) * p_grp
    w2 = (e2 / (1.0 + e2)) * p_grp
    rid_ref[...] = jnp.where(lane_i == 0, i1 - ROUTER_EXP_OFF,
                             jnp.where(lane_i == 1, i2 - ROUTER_EXP_OFF, 0.0)).astype(jnp.int32)
    rw_ref[...] = jnp.where(lane_i == 0, w1, jnp.where(lane_i == 1, w2, 0.0))


def _tail1(o_nsa, o_ssm, h, g_nsa, g_ssm, wout, g_ffn, router, tm_pref=256):
    m, d = h.shape
    tm = _row_tile(m, tm_pref)
    wrh, wrl, br = router
    row = lambda c: pl.BlockSpec((tm, c), lambda i: (i, 0))
    full = lambda a: pl.BlockSpec(a.shape, lambda i: (0,) * a.ndim)
    params = [g_nsa, g_ssm, wout, g_ffn, wrh, wrl, br]
    return pl.pallas_call(
        _tail1_kernel,
        grid=(m // tm,),
        in_specs=[row(o_nsa.shape[1]), row(o_ssm.shape[1]), row(d)] + [full(a) for a in params],
        out_specs=[row(d), row(d), row(ROUTER_COLS), row(ROUTER_COLS)],
        out_shape=[jax.ShapeDtypeStruct((m, d), F32), jax.ShapeDtypeStruct((m, d), F32),
                   jax.ShapeDtypeStruct((m, ROUTER_COLS), jnp.int32), jax.ShapeDtypeStruct((m, ROUTER_COLS), F32)],
        compiler_params=_cparams(("parallel",)),
        name="tail1",
    )(o_nsa, o_ssm, h, *params)


MOE_TILE = 256


def _moe_dispatch(rid):
    t = rid.shape[0]
    pairs = 2 * t
    e = rid.reshape(-1)
    onehot = (e[:, None] == jnp.arange(N_EXPERTS, dtype=jnp.int32)[None, :]).astype(jnp.int32)
    csum = jnp.cumsum(onehot, axis=0)
    rank = jnp.take_along_axis(csum, e[:, None], axis=1)[:, 0] - 1
    tiles = (csum[-1] + MOE_TILE - 1) // MOE_TILE
    tile_end = jnp.cumsum(tiles)
    pos = (tile_end - tiles)[e] * MOE_TILE + rank
    nt_max = -(-pairs // MOE_TILE) + N_EXPERTS
    n_tiles = tile_end[-1]
    tile_ids = jnp.arange(nt_max, dtype=jnp.int32)
    tile_expert = jnp.minimum(jnp.searchsorted(tile_end, tile_ids, side='right'), N_EXPERTS - 1).astype(jnp.int32)
    tile_expert = jnp.where(tile_ids < n_tiles, tile_expert, tile_expert[n_tiles - 1])
    row_token = jnp.zeros((nt_max * MOE_TILE,), jnp.int32).at[pos].set(jnp.arange(pairs, dtype=jnp.int32) // 2)
    return pos.astype(jnp.int32), tile_expert, row_token, n_tiles.astype(jnp.int32).reshape(1)


def _moe_sparse_kernel(te_ref, rt_ref, nt_ref, x_hbm, w1_ref, w3_ref, w2_ref, y_ref, xbuf, w1b, w3b, w2b, sem):
    t = pl.program_id(0)
    n = nt_ref[0]

    def row_copy(tile, slot, r):
        tok = rt_ref[tile * MOE_TILE + r]
        return pltpu.make_async_copy(x_hbm.at[pl.ds(tok, 1), :], xbuf.at[slot, pl.ds(r, 1), :], sem.at[slot])

    def start_tile(tile, slot):
        def body(r, c):
            row_copy(tile, slot, r).start()
            return c
        lax.fori_loop(0, MOE_TILE, body, 0, unroll=8)

    def wait_tile(tile, slot):
        def body(r, c):
            row_copy(tile, slot, r).wait()
            return c
        lax.fori_loop(0, MOE_TILE, body, 0, unroll=8)

    @pl.when(t == 0)
    def _():
        start_tile(0, 0)

    @pl.when(t + 1 < n)
    def _():
        start_tile(t + 1, (t + 1) % 2)

    @pl.when(t < n)
    def _():
        slot = t % 2
        wait_tile(t, slot)

        @pl.when((t == 0) | (te_ref[t] != te_ref[jnp.maximum(t - 1, 0)]))
        def _():
            w1b[...] = w1_ref[...].astype(BF16)
            w3b[...] = w3_ref[...].astype(BF16)
            w2b[...] = w2_ref[...].astype(BF16)

        x = xbuf[slot].astype(BF16)
        a = _dot(x, w1b[...])
        hid = (a * jax.nn.sigmoid(a)) * _dot(x, w3b[...])
        y_ref[...] = _dot(hid.astype(BF16), w2b[...])

    @pl.when(t >= n)
    def _():
        y_ref[...] = jnp.zeros_like(y_ref)


def _moe_sparse(xn, tile_expert, row_token, n_tiles, w1, w3, w2):
    _, d = xn.shape
    _, _, hdim = w1.shape
    nt_max = tile_expert.shape[0]
    wsp = lambda r, c: pl.BlockSpec((None, r, c), lambda t, te, rt, nt: (te[t], 0, 0))
    return pl.pallas_call(
        _moe_sparse_kernel,
        grid_spec=pltpu.PrefetchScalarGridSpec(
            num_scalar_prefetch=3,
            grid=(nt_max,),
            in_specs=[pl.BlockSpec(memory_space=pl.ANY), wsp(d, hdim), wsp(d, hdim), wsp(hdim, d)],
            out_specs=pl.BlockSpec((MOE_TILE, d), lambda t, te, rt, nt: (t, 0)),
            scratch_shapes=[pltpu.VMEM((2, MOE_TILE, d), F32), pltpu.VMEM((d, hdim), BF16),
                            pltpu.VMEM((d, hdim), BF16), pltpu.VMEM((hdim, d), BF16),
                            pltpu.SemaphoreType.DMA((2,))]),
        out_shape=jax.ShapeDtypeStruct((nt_max * MOE_TILE, d), F32),
        compiler_params=_cparams(("arbitrary",)),
        name="moe_sparse",
    )(tile_expert, row_token, n_tiles, xn, w1, w3, w2)


def _tail2_kernel(pos_ref, h1_ref, rw_ref, y_hbm, p_ref, gp_ref, wg_ref, bg_ref, wp_ref, gfin_ref, o_ref, ybuf, sem,
                  *, tm, tok_off):
    base = (tok_off + pl.program_id(0) * tm) * 2

    def row_copy(r, k):
        return pltpu.make_async_copy(y_hbm.at[pl.ds(pos_ref[base + 2 * r + k], 1), :],
                                     ybuf.at[k, pl.ds(r, 1), :], sem.at[0])

    def each_row(fn):
        def body(r, c):
            fn(row_copy(r, 0))
            fn(row_copy(r, 1))
            return c
        lax.fori_loop(0, tm, body, 0, unroll=8)

    each_row(lambda cp: cp.start())
    emb = _dot(p_ref[...].astype(BF16), wp_ref[...])
    each_row(lambda cp: cp.wait())
    rw = rw_ref[...]
    h = h1_ref[...] + (rw[:, 0:1] * ybuf[0] + rw[:, 1:2] * ybuf[1])
    gate = jax.nn.sigmoid(_dot(_rms(h, gp_ref[...]).astype(BF16), wg_ref[...]) + bg_ref[...])
    h = h + gate * emb
    o_ref[...] = _rms(h, gfin_ref[...])


def _tail2(pos, tok_off, h1, rw, y, p, g_ple, wg, bg, wp, g_fin, tm_pref=256):
    m, d = h1.shape
    tm = _row_tile(m, tm_pref)
    row = lambda c: pl.BlockSpec((tm, c), lambda i, ps: (i, 0))
    full = lambda a: pl.BlockSpec(a.shape, lambda i, ps: (0,) * a.ndim)
    params = [g_ple, wg, bg, wp, g_fin]
    return pl.pallas_call(
        functools.partial(_tail2_kernel, tm=tm, tok_off=tok_off),
        grid_spec=pltpu.PrefetchScalarGridSpec(
            num_scalar_prefetch=1,
            grid=(m // tm,),
            in_specs=[row(d), row(ROUTER_COLS), pl.BlockSpec(memory_space=pl.ANY), row(p.shape[1])]
            + [full(a) for a in params],
            out_specs=row(d),
            scratch_shapes=[pltpu.VMEM((2, tm, d), F32), pltpu.SemaphoreType.DMA((1,))]),
        out_shape=jax.ShapeDtypeStruct((m, d), F32),
        compiler_params=_cparams(("arbitrary",)),
        name="tail2",
    )(pos, h1, rw, y, p, *params)


PAGES_PER_STEP = 16
CMP_PAGES_PER_STEP = 32
PAGE_ROWS_PER_TOKEN = KV_GROUPS


def _cmp_paged_kernel(pt_ref, *refs, pages):
    kp = refs[:pages]
    vp = refs[pages:2 * pages]
    wk_ref, wv_ref, ok_ref, ov_ref = refs[2 * pages:]
    chunks = kp[0].shape[0] // (PAGE_ROWS_PER_TOKEN * CMP_STRIDE)
    stride = PAGE_ROWS_PER_TOKEN * CMP_STRIDE
    for src, w, dst in ((kp, wk_ref, ok_ref), (vp, wv_ref, ov_ref)):
        for g in range(KV_GROUPS):
            acc = jnp.zeros((pages * chunks, 2 * HEAD_DIM), F32)
            for j in range(CMP_STRIDE):
                xj = jnp.concatenate(
                    [p[pl.ds(PAGE_ROWS_PER_TOKEN * j + g, chunks, stride=stride), :] for p in src], axis=0)
                acc = acc + _dot(xj.astype(BF16), w[j])
            dst[g] = acc


def _page_specs(page_rows, pages):
    return [pl.BlockSpec((page_rows, HEAD_DIM), functools.partial(lambda i, b, s, pt: (pt[b, s * pages + i], 0), i))
            for i in range(pages)]


def _cmp_paged(pool_k, pool_v, page_table, wk, wv, page_size):
    b, n_pages = page_table.shape
    pages = CMP_PAGES_PER_STEP if n_pages % CMP_PAGES_PER_STEP == 0 else n_pages
    page_rows = page_size * PAGE_ROWS_PER_TOKEN
    chunks = page_size // CMP_STRIDE
    wsp = pl.BlockSpec((CMP_STRIDE, HEAD_DIM, 2 * HEAD_DIM), lambda i, s, pt: (0, 0, 0))
    osp = pl.BlockSpec((None, KV_GROUPS, pages * chunks, 2 * HEAD_DIM), lambda i, s, pt: (i, 0, s, 0))
    osh = jax.ShapeDtypeStruct((b, KV_GROUPS, n_pages * chunks, 2 * HEAD_DIM), F32)
    return pl.pallas_call(
        functools.partial(_cmp_paged_kernel, pages=pages),
        grid_spec=pltpu.PrefetchScalarGridSpec(
            num_scalar_prefetch=1,
            grid=(b, n_pages // pages),
            in_specs=_page_specs(page_rows, pages) * 2 + [wsp, wsp],
            out_specs=[osp, osp]),
        out_shape=[osh, osh],
        compiler_params=_cparams(("parallel", "arbitrary")),
        name="cmp_paged",
    )(page_table, *([pool_k] * pages), *([pool_v] * pages), wk, wv)


def _nsa_sample_kernel(pt_ref, q_ref, ck_ref, cv_ref, a_ref, ksn_ref, vsn_ref, wk_ref, wv_ref, kwn_ref, vwn_ref,
                       gt_ref, *rest, pages, page_size, n_blk, n_cmp, n_pick, past_len):
    kp = rest[:pages]
    vp = rest[pages:2 * pages]
    o_ref, wko_ref, wvo_ref, sel_scr, oc_scr, m_scr, l_scr, acc_scr, q_scr = rest[2 * pages:]
    step = pl.program_id(1)
    last = pl.num_programs(1) - 1
    qpos = past_len
    nbp = sel_scr.shape[2]
    rowi = lax.broadcasted_iota(jnp.int32, (SUBLANES, 1), 0)
    lane = lax.broadcasted_iota(jnp.int32, (1, LANES), 1)
    blk = lax.broadcasted_iota(jnp.int32, (1, nbp), 1)

    def slopes(g):
        sl = jnp.zeros((SUBLANES, 1), F32)
        for h in range(HEADS_PER_GROUP):
            sl = jnp.where(rowi == h, 2.0 ** -(HEADS_PER_GROUP * g + h + 1), sl)
        return sl

    @pl.when(step == 0)
    def _():
        qrow = q_ref[...]
        ncp = ck_ref.shape[1]
        cidx = lax.broadcasted_iota(jnp.int32, (1, ncp), 1)
        dist = jnp.broadcast_to(qpos - (cidx * CMP_STRIDE + (CMP_BLOCK - 1)), (SUBLANES, ncp))
        mask = (dist >= 0) & (cidx < n_cmp)
        cur = qpos // SEL_BLOCK
        valid = blk <= cur
        forced = (blk == 0) | (blk == cur) | (blk == cur - 1)
        for g in range(KV_GROUPS):
            heads = [qrow[:, (HEADS_PER_GROUP * g + h) * HEAD_DIM:(HEADS_PER_GROUP * g + h + 1) * HEAD_DIM]
                     for h in range(HEADS_PER_GROUP)]
            q8 = jnp.concatenate(heads + [jnp.zeros((SUBLANES - HEADS_PER_GROUP, HEAD_DIM), BF16)], axis=0)
            q_scr[g] = q8
            p = _masked_softmax(_dot_nt(q8, ck_ref[g]) - slopes(g) * dist.astype(F32), mask)
            oc_scr[g] = _dot(p.astype(BF16), cv_ref[g])
            p_sum = p[0:1]
            for h in range(1, HEADS_PER_GROUP):
                p_sum = p_sum + p[h:h + 1]
            p_sum = jnp.broadcast_to(p_sum, (SUBLANES, ncp))
            imp = sum(_dot(t, a_ref[...]) for t in _split3(p_sum))
            score = jnp.where(forced & valid, FORCE_SCORE, jnp.where(valid, imp, -FORCE_SCORE))
            score = jnp.where(blk < n_blk, score, -2.0 * FORCE_SCORE)

            def rank_body(j, cnt):
                sj = jnp.max(jnp.where(blk == j, score, -jnp.inf), axis=-1, keepdims=True)
                return cnt + jnp.where((sj > score) | ((sj == score) & (j < blk)), 1.0, 0.0)

            rank = lax.fori_loop(0, n_blk, rank_body, jnp.zeros((SUBLANES, nbp), F32))
            sel_scr[g] = jnp.where((rank < n_pick) & (blk < n_blk), 1.0, 0.0)
            m_scr[g] = jnp.full((SUBLANES, LANES), NEG_INF, F32)
            l_scr[g] = jnp.zeros((SUBLANES, LANES), F32)
            acc_scr[g] = jnp.zeros((SUBLANES, HEAD_DIM), F32)

    blocks_per_page = page_size // SEL_BLOCK
    for g in range(KV_GROUPS):
        for i in range(pages):
            pg = step * pages + i
            first_blk = blocks_per_page * pg
            in_page = (blk >= first_blk) & (blk < first_blk + blocks_per_page)
            any_picked = jnp.max(jnp.where(in_page, sel_scr[g, 0:1, :], 0.0)) > 0.5

            @pl.when(any_picked)
            def _(g=g, i=i, pg=pg, first_blk=first_blk):
                selg = sel_scr[g]
                k = kp[i][pl.ds(g, page_size, stride=PAGE_ROWS_PER_TOKEN), :].astype(BF16)
                v = vp[i][pl.ds(g, page_size, stride=PAGE_ROWS_PER_TOKEN), :].astype(BF16)
                dist = qpos - (pg * page_size + lane)
                picked = jnp.zeros((SUBLANES, LANES), F32)
                for r in range(blocks_per_page):
                    sr = jnp.max(jnp.where(blk == first_blk + r, selg, 0.0), axis=-1, keepdims=True)
                    picked = jnp.where(lane // SEL_BLOCK == r, sr, picked)
                mask = (dist >= 0) & (picked > 0.5)
                s = jnp.where(mask, _dot_nt(q_scr[g], k) - slopes(g) * dist.astype(F32), NEG_INF)
                m_old = m_scr[g][:, 0:1]
                m_new = jnp.maximum(m_old, jnp.max(s, axis=-1, keepdims=True))
                p = jnp.where(mask, jnp.exp(s - m_new), 0.0)
                alpha = jnp.exp(m_old - m_new)
                acc_scr[g] = alpha * acc_scr[g] + _dot(p.astype(BF16), v)
                l_scr[g] = jnp.broadcast_to(alpha * l_scr[g][:, 0:1] + jnp.sum(p, axis=-1, keepdims=True),
                                            (SUBLANES, LANES))
                m_scr[g] = jnp.broadcast_to(m_new, (SUBLANES, LANES))

    @pl.when(step == last)
    def _():
        wb = wk_ref.shape[0]
        wrow = lax.broadcasted_iota(jnp.int32, (wb, 1), 0)
        wk_new = jnp.where(wrow == wb - 1, kwn_ref[...], pltpu.roll(wk_ref[...], wb - 1, 0))
        wv_new = jnp.where(wrow == wb - 1, vwn_ref[...], pltpu.roll(wv_ref[...], wb - 1, 0))
        wko_ref[...] = wk_new
        wvo_ref[...] = wv_new
        widx = lax.broadcasted_iota(jnp.int32, (1, wb), 1)
        wdist = jnp.broadcast_to(wb - 1 - widx, (SUBLANES, wb))
        wmask = (wdist >= 0) & (wdist < WINDOW) & (qpos - wdist >= 0)
        for g in range(KV_GROUPS):
            gs = slice(g * HEAD_DIM, (g + 1) * HEAD_DIM)
            q8 = q_scr[g]
            sl = slopes(g)
            kn = ksn_ref[:, gs].astype(BF16).astype(F32)
            vn = vsn_ref[:, gs].astype(BF16).astype(F32)
            s_n = jnp.sum(q8.astype(F32) * kn, axis=-1, keepdims=True)
            sel_n = jnp.max(jnp.where(blk == n_blk - 1, sel_scr[g], 0.0), axis=-1, keepdims=True) > 0.5
            m_old = m_scr[g][:, 0:1]
            m_new = jnp.maximum(m_old, jnp.where(sel_n, s_n, NEG_INF))
            p_n = jnp.where(sel_n, jnp.exp(s_n - m_new), 0.0)
            alpha = jnp.exp(m_old - m_new)
            acc = alpha * acc_scr[g] + p_n.astype(BF16).astype(F32) * vn
            l = alpha * l_scr[g][:, 0:1] + p_n
            o_s = acc / jnp.maximum(l, TINY)
            p = _masked_softmax(_dot_nt(q8, wk_new[:, gs].astype(BF16)) - sl * wdist.astype(F32), wmask)
            o_w = _dot(p.astype(BF16), wv_new[:, gs].astype(BF16))
            gt = jnp.broadcast_to(gt_ref[:, g * LANES:(g + 1) * LANES], (SUBLANES, LANES))
            gcol = [jnp.sum(jnp.where(lane == k * HEADS_PER_GROUP + rowi, gt, 0.0), axis=-1, keepdims=True)
                    for k in range(3)]
            o = gcol[0] * oc_scr[g] + gcol[1] * o_s + gcol[2] * o_w
            for h in range(HEADS_PER_GROUP):
                hd = HEADS_PER_GROUP * g + h
                o_ref[:, hd * HEAD_DIM:(hd + 1) * HEAD_DIM] = o[h:h + 1]


def _nsa_sample(qbf, ck, cv, ks_new, vs_new, win_k, win_v, kw_new, vw_new, gates, pool_k, pool_v, page_table,
                page_size):
    b, n_pages = page_table.shape
    past_len = n_pages * page_size
    ncp = ck.shape[2]
    n_cmp = (past_len + 1 + CMP_STRIDE - 1) // CMP_STRIDE - 1
    n_blk = (past_len + 1 + SEL_BLOCK - 1) // SEL_BLOCK
    nbp = -(-n_blk // LANES) * LANES
    wb = win_k.shape[1]
    assert wb == WINDOW and n_blk >= TOP_N and ncp == n_cmp and page_size % SEL_BLOCK == 0
    pages = PAGES_PER_STEP if n_pages % PAGES_PER_STEP == 0 else n_pages
    page_rows = page_size * PAGE_ROWS_PER_TOKEN
    a = _imp_matrix(ncp, nbp)
    tok = lambda c: pl.BlockSpec((None, 1, c), lambda i, s, pt: (i, 0, 0))
    csp = pl.BlockSpec((None, KV_GROUPS, ncp, HEAD_DIM), lambda i, s, pt: (i, 0, 0, 0))
    wsp = pl.BlockSpec((None, wb, KV_WIDTH), lambda i, s, pt: (i, 0, 0))
    asp = pl.BlockSpec(a.shape, lambda i, s, pt: (0, 0))
    grp_scr = lambda c, dt: pltpu.VMEM((KV_GROUPS, SUBLANES, c), dt)
    return pl.pallas_call(
        functools.partial(_nsa_sample_kernel, pages=pages, page_size=page_size, n_blk=n_blk, n_cmp=n_cmp,
                          n_pick=TOP_N, past_len=past_len),
        grid_spec=pltpu.PrefetchScalarGridSpec(
            num_scalar_prefetch=1,
            grid=(b, n_pages // pages),
            in_specs=[tok(NSA_WIDTH), csp, csp, asp, tok(KV_WIDTH), tok(KV_WIDTH), wsp, wsp, tok(KV_WIDTH),
                      tok(KV_WIDTH), tok(GATE_COLS)] + _page_specs(page_rows, pages) * 2,
            out_specs=[tok(NSA_WIDTH), wsp, wsp],
            scratch_shapes=[grp_scr(nbp, F32), grp_scr(HEAD_DIM, F32), grp_scr(LANES, F32), grp_scr(LANES, F32),
                            grp_scr(HEAD_DIM, F32), grp_scr(HEAD_DIM, BF16)]),
        out_shape=[jax.ShapeDtypeStruct((b, 1, NSA_WIDTH), F32), jax.ShapeDtypeStruct((b, wb, KV_WIDTH), F32),
                   jax.ShapeDtypeStruct((b, wb, KV_WIDTH), F32)],
        compiler_params=_cparams(("parallel", "arbitrary")),
        name="nsa_sample",
    )(page_table, qbf, ck, cv, a, ks_new, vs_new, win_k, win_v, kw_new, vw_new, gates,
      *([pool_k] * pages), *([pool_v] * pages))


def kernel(x_prompt, x_sample, p_prompt, p_sample, cache_cmp_k, cache_cmp_v, cache_sel_k, cache_sel_v, cache_win_k, cache_win_v, state_ssm_re, state_ssm_im, page_table, norm_mix, w_in, cmp_pe_k, cmp_w1_k, cmp_w2_k, cmp_pe_v, cmp_w1_v, cmp_w2_v, ssm_a_re, ssm_a_im, ssm_log_dt, ssm_b_re, ssm_b_im, ssm_c_re, ssm_c_im, ssm_d, w_glu, b_glu, norm_nsa_out, norm_ssm_out, w_out, norm_ffn, router_grp_w, router_grp_b, router_exp_w, router_exp_b, exp_w1, exp_w3, exp_w2, norm_ple, w_ple_gate, b_ple_gate, w_ple_proj, norm_final):
    depth = w_in.shape[0]
    assert depth == 1, "one layer per step"
    i = 0
    b, s, d = x_prompt.shape
    db, t, _ = x_sample.shape
    assert t == 1, "the sample group decodes one token per sequence"
    ssm_ch = w_glu.shape[1]
    page_size = cache_cmp_k.shape[2]
    n_phys = cache_cmp_k.shape[1]
    wp = min(WINDOW, s)
    row1 = lambda a: a.reshape(1, -1)

    w_inp = _prep_w_in(w_in[i], ssm_ch)
    wk1 = _prep_cmp_w1(cmp_w1_k[i])
    wv1 = _prep_cmp_w1(cmp_w1_v[i])
    s5p = _prep_s5(ssm_a_re[i], ssm_a_im[i], ssm_log_dt[i], ssm_b_re[i], ssm_b_im[i], ssm_c_re[i], ssm_c_im[i])
    router = _prep_router(router_grp_w[i], router_grp_b[i], router_exp_w[i], router_exp_b[i])
    wglu = w_glu[i].astype(BF16)
    wout = w_out[i].astype(BF16)
    wpg = w_ple_gate[i].astype(BF16)
    wpp = w_ple_proj[i].astype(BF16)
    cmp_par = (cmp_pe_k[i], cmp_pe_v[i], cmp_w1_k[i], cmp_w1_v[i], cmp_w2_k[i], cmp_w2_v[i])

    def tail1(h, o_nsa, o_ssm):
        return _tail1(o_nsa, o_ssm, h, row1(norm_nsa_out[i]), row1(norm_ssm_out[i]), wout, row1(norm_ffn[i]), router)

    def tail2(pos, tok_off, h1, rw, y, p):
        return _tail2(pos, tok_off, h1, rw, y, p, row1(norm_ple[i]), wpg, row1(b_ple_gate[i]), wpp, row1(norm_final))

    m = b * s
    hp = x_prompt.reshape(m, d)
    qbf, kc, vc, ks, vs, kw, vw, ksb, vsb, kwb, vwb, u, gates = _inproj(hp, row1(norm_mix[i]), w_inp, ssm_ch, 512)
    seq = lambda a: a.reshape(b, s, a.shape[-1])
    lk, lv = _cmp_prompt(seq(kc), seq(vc), wk1, wv1)
    ck, cv = _cmp_finish(lk, lv, *cmp_par)
    o_nsa = _nsa_prompt(seq(qbf), ck, cv, seq(ksb), seq(vsb), seq(kwb), seq(vwb), seq(gates))
    ns = s5p[0].shape[1]
    zero = jnp.zeros((b, 1, ns), F32)
    o_ssm, hre_p, him_p = _s5(seq(u), zero, zero, s5p, row1(ssm_d[i]), wglu, row1(b_glu[i]), scan=True, precise=False)
    h1_p, xn_p, rid_p, rw_p = tail1(hp, o_nsa.reshape(m, -1), o_ssm.reshape(m, -1))
    kv5 = lambda a: a.reshape(1, b, s, KV_GROUPS, HEAD_DIM)
    win5 = lambda a: a.reshape(b, s, KV_GROUPS, HEAD_DIM)[None, :, s - wp:]
    st4 = lambda a, n: a.reshape(1, n, ns // SSM_STATE, SSM_STATE)
    prompt_state = (kv5(kc), kv5(vc), kv5(ks), kv5(vs), win5(kw), win5(vw), st4(hre_p, b), st4(him_p, b))

    hs = x_sample.reshape(db, d)
    qbf, kc, vc, ks, vs, kw, vw, _, _, _, _, u, gates = _inproj(hs, row1(norm_mix[i]), w_inp, ssm_ch, 512)
    pool = lambda c: c[i].reshape(n_phys * page_size * KV_GROUPS, HEAD_DIM)
    one = lambda a: a.reshape(db, 1, a.shape[-1])
    lk, lv = _cmp_paged(pool(cache_cmp_k), pool(cache_cmp_v), page_table, wk1, wv1, page_size)
    ck, cv = _cmp_finish(lk, lv, *cmp_par, kn=one(kc), vn=one(vc))
    wb = cache_win_k.shape[2]
    o_nsa, win_k, win_v = _nsa_sample(one(qbf), ck, cv, one(ks), one(vs), cache_win_k[i].reshape(db, wb, KV_WIDTH),
                                      cache_win_v[i].reshape(db, wb, KV_WIDTH), one(kw), one(vw), one(gates),
                                      pool(cache_sel_k), pool(cache_sel_v), page_table, page_size)
    o_ssm, hre_s, him_s = _s5(u, state_ssm_re[i].reshape(db, ns), state_ssm_im[i].reshape(db, ns), s5p,
                              row1(ssm_d[i]), wglu, row1(b_glu[i]), scan=False, precise=True)
    h1_s, xn_s, rid_s, rw_s = tail1(hs, o_nsa.reshape(db, -1), o_ssm)

    rid = jnp.concatenate([rid_p[:, :2], rid_s[:, :2]], axis=0)
    pos, tile_expert, row_token, n_tiles = _moe_dispatch(rid)
    y = _moe_sparse(jnp.concatenate([xn_p, xn_s], axis=0), tile_expert, row_token, n_tiles,
                    exp_w1[i], exp_w3[i], exp_w2[i])
    y_prompt = tail2(pos, 0, h1_p, rw_p, y, p_prompt[i].reshape(m, -1)).reshape(b, s, d)
    y_sample = tail2(pos, m, h1_s, rw_s, y, p_sample[i].reshape(db, -1)).reshape(db, 1, d)
    new5 = lambda a: a.reshape(1, db, 1, KV_GROUPS, HEAD_DIM)
    buf5 = lambda a: a.reshape(1, db, wb, KV_GROUPS, HEAD_DIM)
    sample_state = (new5(kc), new5(vc), new5(ks), new5(vs), buf5(win_k), buf5(win_v), st4(hre_s, db), st4(him_s, db))
    return (y_prompt, y_sample) + prompt_state + sample_state
```

```python
import functools

import jax
import jax.numpy as jnp
import numpy as np
from jax import lax
from jax.experimental import pallas as pl
from jax.experimental.pallas import tpu as pltpu

F32 = jnp.float32
BF16 = jnp.bfloat16

HEAD_DIM = 128
N_HEADS = 8
KV_GROUPS = 2
HEADS_PER_GROUP = N_HEADS // KV_GROUPS
NSA_WIDTH = N_HEADS * HEAD_DIM
KV_WIDTH = KV_GROUPS * HEAD_DIM
CMP_STRIDE = 16
CMP_BLOCK = 2 * CMP_STRIDE
SEL_BLOCK = 64
TOP_N = 16
WINDOW = 512
Q_BLOCK = 128
SSM_GROUP_CH = 16
SSM_STATE = 64
N_EXPERT_GROUPS = 4
EXPERTS_PER_GROUP = 8
N_EXPERTS = N_EXPERT_GROUPS * EXPERTS_PER_GROUP
RMS_EPS = 1e-6
NEG_INF = -1e30
FORCE_SCORE = 1e4
TINY = 1e-30

LANES = 128
SUBLANES = 8
GATE_COLS = KV_GROUPS * LANES
ROUTER_COLS = LANES
ROUTER_EXP_OFF = N_EXPERT_GROUPS
SEL_KV_TILE = 512
VMEM_LIMIT_BYTES = 56 * 1024 * 1024


def _cparams(sem):
    return pltpu.CompilerParams(dimension_semantics=sem, vmem_limit_bytes=VMEM_LIMIT_BYTES)


def _rms(x, g):
    return x * lax.rsqrt(jnp.mean(x * x, axis=-1, keepdims=True) + RMS_EPS) * g


def _gelu(x):
    return x * (0.5 * (1.0 + jnp.tanh(0.7978845608028654 * (x + 0.044715 * (x * x * x)))))


def _dot(a, b):
    return jnp.dot(a, b, preferred_element_type=F32)


def _dot_nt(a, b):
    return lax.dot_general(a, b, (((1,), (1,)), ((), ())), preferred_element_type=F32)


def _split2(x):
    hi = x.astype(BF16)
    lo = (x - hi.astype(F32)).astype(BF16)
    return hi, lo


def _split3(x):
    hi = x.astype(BF16)
    r = x - hi.astype(F32)
    mid = r.astype(BF16)
    lo = (r - mid.astype(F32)).astype(BF16)
    return hi, mid, lo


def _masked_softmax(s, mask):
    s = jnp.where(mask, s, NEG_INF)
    m = jnp.max(s, axis=-1, keepdims=True)
    p = jnp.where(mask, jnp.exp(s - m), 0.0)
    return p / jnp.maximum(jnp.sum(p, axis=-1, keepdims=True), TINY)


def _row_tile(m, pref):
    return pref if m % pref == 0 else m


Q_END = NSA_WIDTH
KV_END = Q_END + 6 * KV_WIDTH
U_OFF = KV_END


def _inproj_kernel(x_ref, g_ref, w_ref, qbf, kc, vc, ks, vs, kw, vw, ksb, vsb, kwb, vwb, u, gates, *, ssm_ch):
    xn = _rms(x_ref[...], g_ref[...]).astype(BF16)

    def mm(lo, hi):
        return _dot(xn, w_ref[:, lo:hi])

    qbf[...] = (mm(0, Q_END) * (HEAD_DIM ** -0.5)).astype(BF16)
    for i, (f, b) in enumerate(((kc, None), (vc, None), (ks, ksb), (vs, vsb), (kw, kwb), (vw, vwb))):
        z = mm(Q_END + KV_WIDTH * i, Q_END + KV_WIDTH * (i + 1))
        f[...] = z
        if b is not None:
            b[...] = z.astype(BF16)
    u[...] = mm(U_OFF, U_OFF + ssm_ch)
    gates[...] = jax.nn.sigmoid(mm(U_OFF + ssm_ch, U_OFF + ssm_ch + GATE_COLS))


def _prep_w_in(w_in, ssm_ch):
    wq = w_in[:, :KV_END]
    wg = w_in[:, KV_END:KV_END + 3 * N_HEADS]
    wu = w_in[:, KV_END + 3 * N_HEADS:]
    d = w_in.shape[0]
    wg = wg.reshape(d, KV_GROUPS, HEADS_PER_GROUP, 3).transpose(0, 1, 3, 2).reshape(d, KV_GROUPS, 3 * HEADS_PER_GROUP)
    wg = jnp.pad(wg, ((0, 0), (0, 0), (0, LANES - 3 * HEADS_PER_GROUP))).reshape(d, GATE_COLS)
    return jnp.concatenate([wq, wu, wg], axis=1).astype(BF16)


def _inproj(x, g, w, ssm_ch, tm_pref):
    m, d = x.shape
    tm = _row_tile(m, tm_pref)
    ncols = w.shape[1]
    row = lambda c: pl.BlockSpec((tm, c), lambda i: (i, 0))
    f32s = lambda c: jax.ShapeDtypeStruct((m, c), F32)
    bfs = lambda c: jax.ShapeDtypeStruct((m, c), BF16)
    out_shape = ([bfs(NSA_WIDTH)] + [f32s(KV_WIDTH)] * 6 + [bfs(KV_WIDTH)] * 4 + [f32s(ssm_ch), f32s(GATE_COLS)])
    out_specs = ([row(NSA_WIDTH)] + [row(KV_WIDTH)] * 10 + [row(ssm_ch), row(GATE_COLS)])
    return pl.pallas_call(
        functools.partial(_inproj_kernel, ssm_ch=ssm_ch),
        grid=(m // tm,),
        in_specs=[row(d), pl.BlockSpec((1, d), lambda i: (0, 0)),
                  pl.BlockSpec((d, ncols), lambda i: (0, 0), pipeline_mode=pl.Buffered(1))],
        out_specs=out_specs,
        out_shape=out_shape,
        compiler_params=_cparams(("parallel",)),
        name="inproj",
    )(x, g, w)


def _prep_cmp_w1(w1):
    return jnp.concatenate([w1[:CMP_STRIDE], w1[CMP_STRIDE:]], axis=-1).astype(BF16)


def _cmp_prompt_kernel(kc_ref, vc_ref, wk_ref, wv_ref, ok_ref, ov_ref, *, n_ch):
    for src, w, dst in ((kc_ref, wk_ref, ok_ref), (vc_ref, wv_ref, ov_ref)):
        acc = jnp.zeros((n_ch, 2 * HEAD_DIM), F32)
        for j in range(CMP_STRIDE):
            xj = src[pl.ds(j, n_ch, stride=CMP_STRIDE), :]
            acc = acc + _dot(xj.astype(BF16), w[j])
        dst[...] = acc


def _cmp_prompt(kc, vc, wk, wv):
    b, s, _ = kc.shape
    n_ch = s // CMP_STRIDE
    tok = pl.BlockSpec((None, s, HEAD_DIM), lambda i, g: (i, 0, g))
    wsp = pl.BlockSpec((CMP_STRIDE, HEAD_DIM, 2 * HEAD_DIM), lambda i, g: (0, 0, 0))
    osp = pl.BlockSpec((None, None, n_ch, 2 * HEAD_DIM), lambda i, g: (i, g, 0, 0))
    osh = jax.ShapeDtypeStruct((b, KV_GROUPS, n_ch, 2 * HEAD_DIM), F32)
    return pl.pallas_call(
        functools.partial(_cmp_prompt_kernel, n_ch=n_ch),
        grid=(b, KV_GROUPS),
        in_specs=[tok, tok, wsp, wsp],
        out_specs=[osp, osp],
        out_shape=[osh, osh],
        compiler_params=_cparams(("parallel", "parallel")),
        name="cmp_prompt",
    )(kc, vc, wk, wv)


def _cmp_finish_kernel(*refs, nch, has_tail):
    if has_tail:
        lk, lv, kn, vn, pek, pev, w1k, w1v, w2k, w2v, ock, ocv = refs
    else:
        lk, lv, pek, pev, w1k, w1v, w2k, w2v, ock, ocv = refs
        kn = vn = None
    row = lax.broadcasted_iota(jnp.int32, (nch, 1), 0)
    half = CMP_STRIDE * HEAD_DIM
    for l_ref, n_ref, pe_ref, w1_ref, w2_ref, o_ref in ((lk, kn, pek, w1k, w2k, ock), (lv, vn, pev, w1v, w2v, ocv)):
        lohi = l_ref[...]
        lo = lohi[:, :HEAD_DIM]
        hi = lohi[:, HEAD_DIM:]
        pe = jnp.broadcast_to(pe_ref[0:1, :], (SUBLANES, half)).astype(BF16)
        b_lo = _dot(pe, w1_ref[0:half, :].astype(BF16))[0:1]
        pe = jnp.broadcast_to(pe_ref[1:2, :], (SUBLANES, half)).astype(BF16)
        b_hi = _dot(pe, w1_ref[half:2 * half, :].astype(BF16))[0:1]
        hi_next = pltpu.roll(hi, nch - 1, 0)
        if has_tail:
            new = jnp.broadcast_to(n_ref[...], (SUBLANES, HEAD_DIM)).astype(BF16)
            tail = _dot(new, w1_ref[half:half + HEAD_DIM, :].astype(BF16))[0:1]
            hi_next = jnp.where(row == nch - 1, tail, hi_next)
        hid = _gelu(lo + hi_next + (b_lo + b_hi))
        o_ref[...] = _dot(hid.astype(BF16), w2_ref[...].astype(BF16)).astype(BF16)


def _cmp_finish(lk, lv, pek, pev, w1k, w1v, w2k, w2v, kn=None, vn=None):
    b, _, nch, _ = lk.shape
    has_tail = kn is not None
    lsp = pl.BlockSpec((None, None, nch, 2 * HEAD_DIM), lambda i, g: (i, g, 0, 0))
    full = lambda a: pl.BlockSpec(a.shape, lambda i, g: (0,) * a.ndim)
    pek2 = pek.reshape(2, CMP_STRIDE * HEAD_DIM)
    pev2 = pev.reshape(2, CMP_STRIDE * HEAD_DIM)
    w1k2 = w1k.reshape(CMP_BLOCK * HEAD_DIM, HEAD_DIM)
    w1v2 = w1v.reshape(CMP_BLOCK * HEAD_DIM, HEAD_DIM)
    args = [lk, lv]
    specs = [lsp, lsp]
    if has_tail:
        nsp = pl.BlockSpec((None, 1, HEAD_DIM), lambda i, g: (i, 0, g))
        args += [kn, vn]
        specs += [nsp, nsp]
    params = [pek2, pev2, w1k2, w1v2, w2k, w2v]
    args += params
    specs += [full(a) for a in params]
    osp = pl.BlockSpec((None, None, nch, HEAD_DIM), lambda i, g: (i, g, 0, 0))
    osh = jax.ShapeDtypeStruct((b, KV_GROUPS, nch, HEAD_DIM), BF16)
    return pl.pallas_call(
        functools.partial(_cmp_finish_kernel, nch=nch, has_tail=has_tail),
        grid=(b, KV_GROUPS),
        in_specs=specs,
        out_specs=[osp, osp],
        out_shape=[osh, osh],
        compiler_params=_cparams(("parallel", "parallel")),
        name="cmp_finish",
    )(*args)


def _imp_matrix(n_cmp_pad, n_blk_pad):
    ratio = SEL_BLOCK // CMP_STRIDE
    c = np.arange(n_cmp_pad)[:, None]
    j = np.arange(n_blk_pad)[None, :]
    return jnp.asarray(((c >= ratio * j - 1) & (c <= ratio * j + ratio - 1)).astype(np.float32), dtype=BF16)


def _expand_matrix(n_blk_pad, n_keys):
    j = np.arange(n_blk_pad)[:, None]
    k = np.arange(n_keys)[None, :]
    return jnp.asarray((k // SEL_BLOCK == j).astype(np.float32), dtype=BF16)


def _group_slopes(g):
    sg = jnp.where(g == 0, 1.0, 2.0 ** -HEADS_PER_GROUP).astype(F32)
    return [jnp.full((Q_BLOCK, 1), 2.0 ** -(h + 1), F32) * sg for h in range(HEADS_PER_GROUP)]


def _nsa_prompt_kernel(q_ref, ck_ref, cv_ref, ks_ref, vs_ref, kw_ref, vw_ref, gt_ref, a_ref, e_ref, o_ref, sk_ref,
                       *, seq, n_cmp, n_blk, n_pick):
    g = pl.program_id(1)
    n = pl.program_id(2)
    rows = HEADS_PER_GROUP * Q_BLOCK
    qb = q_ref[...]
    q = jnp.concatenate([qb[:, h * HEAD_DIM:(h + 1) * HEAD_DIM] for h in range(HEADS_PER_GROUP)], axis=0)
    qpos1 = n * Q_BLOCK + lax.broadcasted_iota(jnp.int32, (Q_BLOCK, 1), 0)
    qpos = jnp.concatenate([qpos1] * HEADS_PER_GROUP, axis=0)
    slope = jnp.concatenate(_group_slopes(g), axis=0)

    ncp = ck_ref.shape[0]
    cidx = lax.broadcasted_iota(jnp.int32, (1, ncp), 1)
    dist = qpos - (cidx * CMP_STRIDE + (CMP_BLOCK - 1))
    mask = (dist >= 0) & (cidx < n_cmp)
    p = _masked_softmax(_dot_nt(q, ck_ref[...]) - slope * dist.astype(F32), mask)
    o_c = _dot(p.astype(BF16), cv_ref[...])
    p_sum = p[0:Q_BLOCK]
    for h in range(1, HEADS_PER_GROUP):
        p_sum = p_sum + p[h * Q_BLOCK:(h + 1) * Q_BLOCK]

    imp = sum(_dot(t, a_ref[...]) for t in _split3(p_sum))
    blk = lax.broadcasted_iota(jnp.int32, (1, LANES), 1)
    cur = qpos1 // SEL_BLOCK
    valid = blk <= cur
    forced = (blk == 0) | (blk == cur) | (blk == cur - 1)
    score = jnp.where(forced & valid, FORCE_SCORE, jnp.where(valid, imp, -FORCE_SCORE))
    score_t = score.T[:n_blk]
    bidx = lax.broadcasted_iota(jnp.int32, (n_blk, 1), 0)
    rank = jnp.zeros((n_blk, Q_BLOCK), F32)
    for j in range(n_blk):
        r = score_t[j:j + 1, :]
        rank = rank + jnp.where((r > score_t) | ((r == score_t) & (j < bidx)), 1.0, 0.0)
    sel_t = jnp.where(rank < n_pick, 1.0, 0.0)
    if n_blk < LANES:
        sel_t = jnp.concatenate([sel_t, jnp.zeros((LANES - n_blk, Q_BLOCK), F32)], axis=0)
    sk_ref[...] = _dot(sel_t.T.astype(BF16), e_ref[...])

    tk = min(SEL_KV_TILE, seq)
    n_tiles = (n * Q_BLOCK + Q_BLOCK + tk - 1) // tk
    col = lax.broadcasted_iota(jnp.int32, (1, HEAD_DIM), 1)
    q_aug = jnp.where(col == 0, slope * float(SEL_BLOCK), jnp.where(col == 1, slope, 0.0)).astype(BF16)
    q2 = jnp.concatenate([q, q_aug], axis=1)
    krow = lax.broadcasted_iota(jnp.int32, (tk, 1), 0)

    def body(t, carry):
        m, l, acc = carry
        k0 = pl.multiple_of(t * tk, tk)
        kpos = k0 + lax.broadcasted_iota(jnp.int32, (1, tk), 1)
        kp_col = k0 + krow
        k_aug = jnp.where(col == 0, kp_col // SEL_BLOCK, jnp.where(col == 1, kp_col % SEL_BLOCK, 0))
        k2 = jnp.concatenate([ks_ref[pl.ds(k0, tk), :], k_aug.astype(F32).astype(BF16)], axis=1)
        picked = sk_ref[:, pl.ds(k0, tk)] > 0.5
        bias = jnp.where((kpos <= qpos1) & picked, 0.0, NEG_INF)
        s = _dot_nt(q2, k2) + jnp.concatenate([bias] * HEADS_PER_GROUP, axis=0)
        m_new = jnp.maximum(m, jnp.max(s, axis=-1, keepdims=True))
        p = jnp.exp(s - m_new)
        alpha = jnp.exp(m - m_new)
        l = alpha * l + jnp.sum(p, axis=-1, keepdims=True)
        acc = alpha * acc + _dot(p.astype(BF16), vs_ref[pl.ds(k0, tk), :])
        return m_new, l, acc

    init = (jnp.full((rows, 1), NEG_INF, F32), jnp.zeros((rows, 1), F32), jnp.zeros((rows, HEAD_DIM), F32))
    _, l, acc = lax.fori_loop(0, n_tiles, body, init)
    o_s = acc / jnp.maximum(l, TINY)

    band = min(WINDOW + Q_BLOCK, seq)
    st = pl.multiple_of(jnp.minimum(jnp.maximum(n * Q_BLOCK - WINDOW, 0), seq - band), Q_BLOCK)
    kpos = st + lax.broadcasted_iota(jnp.int32, (1, band), 1)
    dist = qpos - kpos
    mask = (dist >= 0) & (dist < WINDOW)
    p = _masked_softmax(_dot_nt(q, kw_ref[pl.ds(st, band), :]) - slope * dist.astype(F32), mask)
    o_w = _dot(p.astype(BF16), vw_ref[pl.ds(st, band), :])

    gt = gt_ref[...]
    for h in range(HEADS_PER_GROUP):
        sl = slice(h * Q_BLOCK, (h + 1) * Q_BLOCK)
        o_ref[:, h * HEAD_DIM:(h + 1) * HEAD_DIM] = (
            gt[:, h:h + 1] * o_c[sl]
            + gt[:, HEADS_PER_GROUP + h:HEADS_PER_GROUP + h + 1] * o_s[sl]
            + gt[:, 2 * HEADS_PER_GROUP + h:2 * HEADS_PER_GROUP + h + 1] * o_w[sl])


def _nsa_prompt(qbf, ck, cv, ksb, vsb, kwb, vwb, gates):
    b, s, _ = qbf.shape
    ncp = ck.shape[2]
    n_cmp = s // CMP_STRIDE - 1
    n_blk = s // SEL_BLOCK
    assert s % Q_BLOCK == 0 and TOP_N <= n_blk <= LANES
    a = _imp_matrix(ncp, LANES)
    e = _expand_matrix(LANES, s)
    gw = HEADS_PER_GROUP * HEAD_DIM
    qsp = pl.BlockSpec((None, Q_BLOCK, gw), lambda i, g, n: (i, n, g))
    csp = pl.BlockSpec((None, None, ncp, HEAD_DIM), lambda i, g, n: (i, g, 0, 0))
    ksp = pl.BlockSpec((None, s, HEAD_DIM), lambda i, g, n: (i, 0, g))
    gsp = pl.BlockSpec((None, Q_BLOCK, LANES), lambda i, g, n: (i, n, g))
    full = lambda x: pl.BlockSpec(x.shape, lambda i, g, n: (0,) * x.ndim)
    return pl.pallas_call(
        functools.partial(_nsa_prompt_kernel, seq=s, n_cmp=n_cmp, n_blk=n_blk, n_pick=TOP_N),
        grid=(b, KV_GROUPS, s // Q_BLOCK),
        in_specs=[qsp, csp, csp, ksp, ksp, ksp, ksp, gsp, full(a), full(e)],
        out_specs=qsp,
        out_shape=jax.ShapeDtypeStruct((b, s, NSA_WIDTH), F32),
        scratch_shapes=[pltpu.VMEM((Q_BLOCK, s), F32)],
        compiler_params=_cparams(("parallel", "parallel", "arbitrary")),
        name="nsa_prompt",
    )(qbf, ck, cv, ksb, vsb, kwb, vwb, gates, a, e)


SSM_LANE_GROUPS = LANES // SSM_GROUP_CH
SSM_STATE_TILE = SSM_LANE_GROUPS * SSM_STATE


def _prep_s5(a_re, a_im, log_dt, b_re, b_im, c_re, c_im):
    ng = a_re.shape[0]
    nk = ng // SSM_LANE_GROUPS
    dt = jnp.exp(log_dt)[:, None]
    mag = jnp.exp(a_re * dt)
    lb_re = mag * jnp.cos(a_im * dt)
    lb_im = mag * jnp.sin(a_im * dt)
    den = a_re * a_re + a_im * a_im
    nr = lb_re - 1.0
    f_re = (nr * a_re + lb_im * a_im) / den
    f_im = (lb_im * a_re - nr * a_im) / den
    bb_re = f_re[..., None] * b_re - f_im[..., None] * b_im
    bb_im = f_re[..., None] * b_im + f_im[..., None] * b_re
    eye = jnp.eye(SSM_LANE_GROUPS, dtype=F32)

    def in_map(bb):
        t = bb.reshape(nk, SSM_LANE_GROUPS, SSM_STATE, SSM_GROUP_CH)
        t = jnp.einsum('kgnc,gh->kgchn', t, eye)
        return t.reshape(nk, LANES, SSM_STATE_TILE)

    def out_map(c):
        t = c.reshape(nk, SSM_LANE_GROUPS, SSM_GROUP_CH, SSM_STATE)
        t = jnp.einsum('kgcn,gh->kgnhc', t, eye)
        return t.reshape(nk, SSM_STATE_TILE, LANES)

    bb = jnp.concatenate([in_map(bb_re), in_map(bb_im)], axis=-1)
    cm = jnp.concatenate([out_map(c_re), -out_map(c_im)], axis=1)
    bb_hi = bb.astype(BF16)
    bb_lo = (bb - bb_hi.astype(F32)).astype(BF16)
    lam = jnp.stack([lb_re.reshape(-1), lb_im.reshape(-1)])
    return lam, bb_hi, bb_lo, cm.astype(BF16)


def _s5_kernel(u_ref, h0r_ref, h0i_ref, lam_ref, bbh_ref, bbl_ref, cm_ref, d_ref, wglu_ref, bglu_ref,
               o_ref, hr_ref, hi_ref, *scratch, rows, scan, precise):
    nk = bbh_ref.shape[0]
    st = SSM_STATE_TILE
    if scan:
        cr_scr, ci_scr = scratch
        t = pl.program_id(1)

        @pl.when(t == 0)
        def _():
            cr_scr[...] = h0r_ref[...]
            ci_scr[...] = h0i_ref[...]

        row = lax.broadcasted_iota(jnp.int32, (rows, 1), 0)
    u = u_ref[...]
    ys = []
    for k in range(nk):
        uk = u[:, k * LANES:(k + 1) * LANES]
        if precise:
            uh, ul = _split2(uk)
            bu = _dot(uh, bbh_ref[k]) + _dot(ul, bbh_ref[k]) + _dot(uh, bbl_ref[k])
        else:
            bu = _dot(uk.astype(BF16), bbh_ref[k])
        br = bu[:, :st]
        bi = bu[:, st:]
        ks = slice(k * st, (k + 1) * st)
        lr = lam_ref[0:1, ks]
        li = lam_ref[1:2, ks]
        if scan:
            cr = cr_scr[:, ks]
            ci = ci_scr[:, ks]
            first = row == 0
            br = br + jnp.where(first, lr * cr - li * ci, 0.0)
            bi = bi + jnp.where(first, lr * ci + li * cr, 0.0)
            pr, pi = lr, li
            sh = 1
            while sh < rows:
                sr = jnp.where(row >= sh, pltpu.roll(br, sh, 0), 0.0)
                si = jnp.where(row >= sh, pltpu.roll(bi, sh, 0), 0.0)
                br, bi = br + (pr * sr - pi * si), bi + (pr * si + pi * sr)
                pr, pi = pr * pr - pi * pi, 2.0 * (pr * pi)
                sh *= 2
            cr_scr[:, ks] = br[rows - 1:rows]
            ci_scr[:, ks] = bi[rows - 1:rows]
        else:
            h0r = h0r_ref[:, ks]
            h0i = h0i_ref[:, ks]
            br, bi = br + (lr * h0r - li * h0i), bi + (lr * h0i + li * h0r)
            hr_ref[:, ks] = br
            hi_ref[:, ks] = bi
        hcat = jnp.concatenate([br, bi], axis=1).astype(BF16)
        ys.append(_dot(hcat, cm_ref[k]) + d_ref[:, k * LANES:(k + 1) * LANES] * uk)
    v = _gelu(jnp.concatenate(ys, axis=1))
    o_ref[...] = v * jax.nn.sigmoid(_dot(v.astype(BF16), wglu_ref[...]) + bglu_ref[...])
    if scan:
        hr_ref[...] = cr_scr[...]
        hi_ref[...] = ci_scr[...]


def _s5(u, h0r, h0i, prep, d, wglu, bglu, *, scan, precise, tc_pref=256):
    lam, bbh, bbl, cm = prep
    ns = lam.shape[1]
    full = lambda a: pl.BlockSpec(a.shape, lambda *idx: (0,) * a.ndim)
    params = [lam, bbh, bbl, cm, d, wglu, bglu]
    if scan:
        b, t, c = u.shape
        tc = _row_tile(t, tc_pref)
        grid = (b, t // tc)
        usp = pl.BlockSpec((None, tc, c), lambda i, j: (i, j, 0))
        hsp = pl.BlockSpec((None, 1, ns), lambda i, j: (i, 0, 0))
        rows = tc
        scratch = [pltpu.VMEM((1, ns), F32), pltpu.VMEM((1, ns), F32)]
        sem = ("parallel", "arbitrary")
        osh = [jax.ShapeDtypeStruct((b, t, c), F32)] + [jax.ShapeDtypeStruct((b, 1, ns), F32)] * 2
    else:
        b, c = u.shape
        grid = (1,)
        usp = pl.BlockSpec((b, c), lambda i: (0, 0))
        hsp = pl.BlockSpec((b, ns), lambda i: (0, 0))
        rows = b
        scratch = []
        sem = ("arbitrary",)
        osh = [jax.ShapeDtypeStruct((b, c), F32)] + [jax.ShapeDtypeStruct((b, ns), F32)] * 2
    return pl.pallas_call(
        functools.partial(_s5_kernel, rows=rows, scan=scan, precise=precise),
        grid=grid,
        in_specs=[usp, hsp, hsp] + [full(a) for a in params],
        out_specs=[usp, hsp, hsp],
        out_shape=osh,
        scratch_shapes=scratch,
        compiler_params=_cparams(sem),
        name="s5_scan" if scan else "s5_step",
    )(u, h0r, h0i, *params)


def _prep_router(rg_w, rg_b, re_w, re_b):
    d = rg_w.shape[0]
    w = jnp.concatenate([rg_w, re_w.transpose(1, 0, 2).reshape(d, N_EXPERTS)], axis=1)
    w = jnp.pad(w, ((0, 0), (0, ROUTER_COLS - w.shape[1])))
    b = jnp.pad(jnp.concatenate([rg_b, re_b.reshape(-1)]), (0, ROUTER_COLS - N_EXPERT_GROUPS - N_EXPERTS))
    hi = w.astype(BF16)
    lo = (w - hi.astype(F32)).astype(BF16)
    return hi, lo, b.reshape(1, ROUTER_COLS)


def _tail1_kernel(on_ref, os_ref, h_ref, gn_ref, gs_ref, wout_ref, gf_ref, wrh_ref, wrl_ref, br_ref,
                  h1_ref, xn_ref, rid_ref, rw_ref):
    cat = jnp.concatenate([_rms(on_ref[...], gn_ref[...]), _rms(os_ref[...], gs_ref[...])], axis=1)
    h1 = h_ref[...] + _dot(cat.astype(BF16), wout_ref[...])
    h1_ref[...] = h1
    xn = _rms(h1, gf_ref[...])
    xh, xl = _split2(xn)
    xn_ref[...] = xn
    lg = _dot(xh, wrh_ref[...]) + _dot(xl, wrh_ref[...]) + _dot(xh, wrl_ref[...]) + br_ref[...]
    lane_i = lax.broadcasted_iota(jnp.int32, (1, ROUTER_COLS), 1)
    lane = lane_i.astype(F32)
    ninf = -jnp.inf
    big = float(ROUTER_COLS)
    is_grp = lane_i < N_EXPERT_GROUPS
    lgm = jnp.where(is_grp, lg, ninf)
    mx = jnp.max(lgm, axis=-1, keepdims=True)
    grp = jnp.min(jnp.where(lgm == mx, lane, big), axis=-1, keepdims=True)
    p_grp = 1.0 / jnp.sum(jnp.where(is_grp, jnp.exp(lgm - mx), 0.0), axis=-1, keepdims=True)
    first = ROUTER_EXP_OFF + EXPERTS_PER_GROUP * grp
    lem = jnp.where((lane >= first) & (lane < first + EXPERTS_PER_GROUP), lg, ninf)
    v1 = jnp.max(lem, axis=-1, keepdims=True)
    i1 = jnp.min(jnp.where(lem == v1, lane, big), axis=-1, keepdims=True)
    lem2 = jnp.where(lane == i1, ninf, lem)
    v2 = jnp.max(lem2, axis=-1, keepdims=True)
    i2 = jnp.min(jnp.where(lem2 == v2, lane, big), axis=-1, keepdims=True)
    e2 = jnp.exp(v2 - v1)
    w1 = (1.0 / (1.0 + e2)) * p_grp
    w2 = (e2 / (1.0 + e2)) * p_grp
    rid_ref[...] = jnp.where(lane_i == 0, i1 - ROUTER_EXP_OFF,
                             jnp.where(lane_i == 1, i2 - ROUTER_EXP_OFF, 0.0)).astype(jnp.int32)
    rw_ref[...] = jnp.where(lane_i == 0, w1, jnp.where(lane_i == 1, w2, 0.0))


def _tail1(o_nsa, o_ssm, h, g_nsa, g_ssm, wout, g_ffn, router, tm_pref=256):
    m, d = h.shape
    tm = _row_tile(m, tm_pref)
    wrh, wrl, br = router
    row = lambda c: pl.BlockSpec((tm, c), lambda i: (i, 0))
    full = lambda a: pl.BlockSpec(a.shape, lambda i: (0,) * a.ndim)
    params = [g_nsa, g_ssm, wout, g_ffn, wrh, wrl, br]
    return pl.pallas_call(
        _tail1_kernel,
        grid=(m // tm,),
        in_specs=[row(o_nsa.shape[1]), row(o_ssm.shape[1]), row(d)] + [full(a) for a in params],
        out_specs=[row(d), row(d), row(ROUTER_COLS), row(ROUTER_COLS)],
        out_shape=[jax.ShapeDtypeStruct((m, d), F32), jax.ShapeDtypeStruct((m, d), F32),
                   jax.ShapeDtypeStruct((m, ROUTER_COLS), jnp.int32), jax.ShapeDtypeStruct((m, ROUTER_COLS), F32)],
        compiler_params=_cparams(("parallel",)),
        name="tail1",
    )(o_nsa, o_ssm, h, *params)


MOE_TILE = 256


def _moe_dispatch(rid):
    t = rid.shape[0]
    pairs = 2 * t
    e = rid.reshape(-1)
    onehot = (e[:, None] == jnp.arange(N_EXPERTS, dtype=jnp.int32)[None, :]).astype(jnp.int32)
    csum = jnp.cumsum(onehot, axis=0)
    rank = jnp.take_along_axis(csum, e[:, None], axis=1)[:, 0] - 1
    tiles = (csum[-1] + MOE_TILE - 1) // MOE_TILE
    tile_end = jnp.cumsum(tiles)
    pos = (tile_end - tiles)[e] * MOE_TILE + rank
    nt_max = -(-pairs // MOE_TILE) + N_EXPERTS
    n_tiles = tile_end[-1]
    tile_ids = jnp.arange(nt_max, dtype=jnp.int32)
    tile_expert = jnp.minimum(jnp.sum((tile_end[None, :] <= tile_ids[:, None]).astype(jnp.int32), axis=1), N_EXPERTS - 1)
    tile_expert = jnp.where(tile_ids < n_tiles, tile_expert, tile_expert[n_tiles - 1])
    row_token = jnp.zeros((nt_max * MOE_TILE,), jnp.int32).at[pos].set(jnp.arange(pairs, dtype=jnp.int32) // 2)
    return pos.astype(jnp.int32), tile_expert, row_token, n_tiles.astype(jnp.int32).reshape(1)


def _moe_sparse_kernel(te_ref, rt_ref, nt_ref, x_hbm, w1_ref, w3_ref, w2_ref, y_ref, xbuf, w1b, w3b, w2b, sem):
    t = pl.program_id(0)
    n = nt_ref[0]

    def row_copy(tile, slot, r):
        tok = rt_ref[tile * MOE_TILE + r]
        return pltpu.make_async_copy(x_hbm.at[pl.ds(tok, 1), :], xbuf.at[slot, pl.ds(r, 1), :], sem.at[slot])

    def start_tile(tile, slot):
        def body(r, c):
            row_copy(tile, slot, r).start()
            return c
        lax.fori_loop(0, MOE_TILE, body, 0, unroll=8)

    def wait_tile(tile, slot):
        def body(r, c):
            row_copy(tile, slot, r).wait()
            return c
        lax.fori_loop(0, MOE_TILE, body, 0, unroll=8)

    @pl.when(t == 0)
    def _():
        start_tile(0, 0)

    @pl.when(t + 1 < n)
    def _():
        start_tile(t + 1, (t + 1) % 2)

    @pl.when(t < n)
    def _():
        slot = t % 2
        wait_tile(t, slot)

        @pl.when((t == 0) | (te_ref[t] != te_ref[jnp.maximum(t - 1, 0)]))
        def _():
            w1b[...] = w1_ref[...].astype(BF16)
            w3b[...] = w3_ref[...].astype(BF16)
            w2b[...] = w2_ref[...].astype(BF16)

        x = xbuf[slot].astype(BF16)
        a = _dot(x, w1b[...])
        hid = (a * jax.nn.sigmoid(a)) * _dot(x, w3b[...])
        y_ref[...] = _dot(hid.astype(BF16), w2b[...])

    @pl.when(t >= n)
    def _():
        y_ref[...] = jnp.zeros_like(y_ref)


def _moe_sparse(xn, tile_expert, row_token, n_tiles, w1, w3, w2):
    _, d = xn.shape
    _, _, hdim = w1.shape
    nt_max = tile_expert.shape[0]
    wsp = lambda r, c: pl.BlockSpec((None, r, c), lambda t, te, rt, nt: (te[t], 0, 0))
    return pl.pallas_call(
        _moe_sparse_kernel,
        grid_spec=pltpu.PrefetchScalarGridSpec(
            num_scalar_prefetch=3,
            grid=(nt_max,),
            in_specs=[pl.BlockSpec(memory_space=pl.ANY), wsp(d, hdim), wsp(d, hdim), wsp(hdim, d)],
            out_specs=pl.BlockSpec((MOE_TILE, d), lambda t, te, rt, nt: (t, 0)),
            scratch_shapes=[pltpu.VMEM((2, MOE_TILE, d), F32), pltpu.VMEM((d, hdim), BF16),
                            pltpu.VMEM((d, hdim), BF16), pltpu.VMEM((hdim, d), BF16),
                            pltpu.SemaphoreType.DMA((2,))]),
        out_shape=jax.ShapeDtypeStruct((nt_max * MOE_TILE, d), F32),
        compiler_params=_cparams(("arbitrary",)),
        name="moe_sparse",
    )(tile_expert, row_token, n_tiles, xn, w1, w3, w2)


def _tail2_kernel(pos_ref, h1_ref, rw_ref, y_hbm, p_ref, gp_ref, wg_ref, bg_ref, wp_ref, gfin_ref, o_ref, ybuf, sem,
                  *, tm, tok_off):
    base = (tok_off + pl.program_id(0) * tm) * 2

    def row_copy(r, k):
        return pltpu.make_async_copy(y_hbm.at[pl.ds(pos_ref[base + 2 * r + k], 1), :],
                                     ybuf.at[k, pl.ds(r, 1), :], sem.at[0])

    def each_row(fn):
        def body(r, c):
            fn(row_copy(r, 0))
            fn(row_copy(r, 1))
            return c
        lax.fori_loop(0, tm, body, 0, unroll=8)

    each_row(lambda cp: cp.start())
    emb = _dot(p_ref[...].astype(BF16), wp_ref[...])
    each_row(lambda cp: cp.wait())
    rw = rw_ref[...]
    h = h1_ref[...] + (rw[:, 0:1] * ybuf[0] + rw[:, 1:2] * ybuf[1])
    gate = jax.nn.sigmoid(_dot(_rms(h, gp_ref[...]).astype(BF16), wg_ref[...]) + bg_ref[...])
    h = h + gate * emb
    o_ref[...] = _rms(h, gfin_ref[...])


def _tail2(pos, tok_off, h1, rw, y, p, g_ple, wg, bg, wp, g_fin, tm_pref=256):
    m, d = h1.shape
    tm = _row_tile(m, tm_pref)
    row = lambda c: pl.BlockSpec((tm, c), lambda i, ps: (i, 0))
    full = lambda a: pl.BlockSpec(a.shape, lambda i, ps: (0,) * a.ndim)
    params = [g_ple, wg, bg, wp, g_fin]
    return pl.pallas_call(
        functools.partial(_tail2_kernel, tm=tm, tok_off=tok_off),
        grid_spec=pltpu.PrefetchScalarGridSpec(
            num_scalar_prefetch=1,
            grid=(m // tm,),
            in_specs=[row(d), row(ROUTER_COLS), pl.BlockSpec(memory_space=pl.ANY), row(p.shape[1])]
            + [full(a) for a in params],
            out_specs=row(d),
            scratch_shapes=[pltpu.VMEM((2, tm, d), F32), pltpu.SemaphoreType.DMA((1,))]),
        out_shape=jax.ShapeDtypeStruct((m, d), F32),
        compiler_params=_cparams(("arbitrary",)),
        name="tail2",
    )(pos, h1, rw, y, p, *params)


PAGES_PER_STEP = 16
CMP_PAGES_PER_STEP = 32
PAGE_ROWS_PER_TOKEN = KV_GROUPS


def _cmp_paged_kernel(pt_ref, *refs, pages):
    kp = refs[:pages]
    vp = refs[pages:2 * pages]
    wk_ref, wv_ref, ok_ref, ov_ref = refs[2 * pages:]
    chunks = kp[0].shape[0] // (PAGE_ROWS_PER_TOKEN * CMP_STRIDE)
    stride = PAGE_ROWS_PER_TOKEN * CMP_STRIDE
    for src, w, dst in ((kp, wk_ref, ok_ref), (vp, wv_ref, ov_ref)):
        for g in range(KV_GROUPS):
            acc = jnp.zeros((pages * chunks, 2 * HEAD_DIM), F32)
            for j in range(CMP_STRIDE):
                xj = jnp.concatenate(
                    [p[pl.ds(PAGE_ROWS_PER_TOKEN * j + g, chunks, stride=stride), :] for p in src], axis=0)
                acc = acc + _dot(xj.astype(BF16), w[j])
            dst[g] = acc


def _page_specs(page_rows, pages):
    return [pl.BlockSpec((page_rows, HEAD_DIM), functools.partial(lambda i, b, s, pt: (pt[b, s * pages + i], 0), i))
            for i in range(pages)]


def _cmp_paged(pool_k, pool_v, page_table, wk, wv, page_size):
    b, n_pages = page_table.shape
    pages = CMP_PAGES_PER_STEP if n_pages % CMP_PAGES_PER_STEP == 0 else n_pages
    page_rows = page_size * PAGE_ROWS_PER_TOKEN
    chunks = page_size // CMP_STRIDE
    wsp = pl.BlockSpec((CMP_STRIDE, HEAD_DIM, 2 * HEAD_DIM), lambda i, s, pt: (0, 0, 0))
    osp = pl.BlockSpec((None, KV_GROUPS, pages * chunks, 2 * HEAD_DIM), lambda i, s, pt: (i, 0, s, 0))
    osh = jax.ShapeDtypeStruct((b, KV_GROUPS, n_pages * chunks, 2 * HEAD_DIM), F32)
    return pl.pallas_call(
        functools.partial(_cmp_paged_kernel, pages=pages),
        grid_spec=pltpu.PrefetchScalarGridSpec(
            num_scalar_prefetch=1,
            grid=(b, n_pages // pages),
            in_specs=_page_specs(page_rows, pages) * 2 + [wsp, wsp],
            out_specs=[osp, osp]),
        out_shape=[osh, osh],
        compiler_params=_cparams(("parallel", "arbitrary")),
        name="cmp_paged",
    )(page_table, *([pool_k] * pages), *([pool_v] * pages), wk, wv)


def _nsa_sample_kernel(pt_ref, q_ref, ck_ref, cv_ref, a_ref, ksn_ref, vsn_ref, wk_ref, wv_ref, kwn_ref, vwn_ref,
                       gt_ref, *rest, pages, page_size, n_blk, n_cmp, n_pick, past_len):
    kp = rest[:pages]
    vp = rest[pages:2 * pages]
    o_ref, wko_ref, wvo_ref, sel_scr, oc_scr, m_scr, l_scr, acc_scr, q_scr, sel_smem, sel_sem = rest[2 * pages:]
    step = pl.program_id(1)
    last = pl.num_programs(1) - 1
    qpos = past_len
    nbp = sel_scr.shape[1]
    rowi = lax.broadcasted_iota(jnp.int32, (SUBLANES, 1), 0)
    lane = lax.broadcasted_iota(jnp.int32, (1, LANES), 1)
    blk = lax.broadcasted_iota(jnp.int32, (1, nbp), 1)

    def slopes(g):
        sl = jnp.zeros((SUBLANES, 1), F32)
        for h in range(HEADS_PER_GROUP):
            sl = jnp.where(rowi == h, 2.0 ** -(HEADS_PER_GROUP * g + h + 1), sl)
        return sl

    @pl.when(step == 0)
    def _():
        qrow = q_ref[...]
        ncp = ck_ref.shape[1]
        cidx = lax.broadcasted_iota(jnp.int32, (1, ncp), 1)
        dist = jnp.broadcast_to(qpos - (cidx * CMP_STRIDE + (CMP_BLOCK - 1)), (SUBLANES, ncp))
        mask = (dist >= 0) & (cidx < n_cmp)
        cur = qpos // SEL_BLOCK
        valid = blk <= cur
        forced = (blk == 0) | (blk == cur) | (blk == cur - 1)
        for g in range(KV_GROUPS):
            heads = [qrow[:, (HEADS_PER_GROUP * g + h) * HEAD_DIM:(HEADS_PER_GROUP * g + h + 1) * HEAD_DIM]
                     for h in range(HEADS_PER_GROUP)]
            q8 = jnp.concatenate(heads + [jnp.zeros((SUBLANES - HEADS_PER_GROUP, HEAD_DIM), BF16)], axis=0)
            q_scr[g] = q8
            p = _masked_softmax(_dot_nt(q8, ck_ref[g]) - slopes(g) * dist.astype(F32), mask)
            oc_scr[g] = _dot(p.astype(BF16), cv_ref[g])
            p_sum = p[0:1]
            for h in range(1, HEADS_PER_GROUP):
                p_sum = p_sum + p[h:h + 1]
            p_sum = jnp.broadcast_to(p_sum, (SUBLANES, ncp))
            imp = sum(_dot(t, a_ref[...]) for t in _split3(p_sum))
            score = jnp.where(forced & valid, FORCE_SCORE, jnp.where(valid, imp, -FORCE_SCORE))
            score = jnp.where(blk < n_blk, score, -2.0 * FORCE_SCORE)

            def rank_body(j, cnt):
                sj = jnp.max(jnp.where(blk == j, score, -jnp.inf), axis=-1, keepdims=True)
                return cnt + jnp.where((sj > score) | ((sj == score) & (j < blk)), 1.0, 0.0)

            rank = lax.fori_loop(0, n_blk, rank_body, jnp.zeros((SUBLANES, nbp), F32))
            sel_scr[g * SUBLANES:(g + 1) * SUBLANES, :] = jnp.where((rank < n_pick) & (blk < n_blk), 1, 0)
            m_scr[g] = jnp.full((SUBLANES, LANES), NEG_INF, F32)
            l_scr[g] = jnp.zeros((SUBLANES, LANES), F32)
            acc_scr[g] = jnp.zeros((SUBLANES, HEAD_DIM), F32)
        flags = pltpu.make_async_copy(sel_scr, sel_smem, sel_sem.at[0])
        flags.start()
        flags.wait()

    blocks_per_page = page_size // SEL_BLOCK
    for g in range(KV_GROUPS):
        for i in range(pages):
            pg = step * pages + i
            first_blk = blocks_per_page * pg
            flags = [sel_smem[g * SUBLANES, first_blk + r] for r in range(blocks_per_page)]

            @pl.when(sum(flags) > 0)
            def _(g=g, i=i, pg=pg, flags=flags):
                k = kp[i][pl.ds(g, page_size, stride=PAGE_ROWS_PER_TOKEN), :].astype(BF16)
                v = vp[i][pl.ds(g, page_size, stride=PAGE_ROWS_PER_TOKEN), :].astype(BF16)
                dist = qpos - (pg * page_size + lane)
                picked = jnp.zeros((1, LANES), jnp.int32)
                for r in range(blocks_per_page):
                    picked = jnp.where(lane // SEL_BLOCK == r, flags[r], picked)
                mask = jnp.broadcast_to((dist >= 0) & (picked > 0), (SUBLANES, LANES))
                s = jnp.where(mask, _dot_nt(q_scr[g], k) - slopes(g) * dist.astype(F32), NEG_INF)
                m_old = m_scr[g][:, 0:1]
                m_new = jnp.maximum(m_old, jnp.max(s, axis=-1, keepdims=True))
                p = jnp.where(mask, jnp.exp(s - m_new), 0.0)
                alpha = jnp.exp(m_old - m_new)
                acc_scr[g] = alpha * acc_scr[g] + _dot(p.astype(BF16), v)
                l_scr[g] = jnp.broadcast_to(alpha * l_scr[g][:, 0:1] + jnp.sum(p, axis=-1, keepdims=True),
                                            (SUBLANES, LANES))
                m_scr[g] = jnp.broadcast_to(m_new, (SUBLANES, LANES))

    @pl.when(step == last)
    def _():
        wb = wk_ref.shape[0]
        wrow = lax.broadcasted_iota(jnp.int32, (wb, 1), 0)
        wk_new = jnp.where(wrow == wb - 1, kwn_ref[...], pltpu.roll(wk_ref[...], wb - 1, 0))
        wv_new = jnp.where(wrow == wb - 1, vwn_ref[...], pltpu.roll(wv_ref[...], wb - 1, 0))
        wko_ref[...] = wk_new
        wvo_ref[...] = wv_new
        widx = lax.broadcasted_iota(jnp.int32, (1, wb), 1)
        wdist = jnp.broadcast_to(wb - 1 - widx, (SUBLANES, wb))
        wmask = (wdist >= 0) & (wdist < WINDOW) & (qpos - wdist >= 0)
        for g in range(KV_GROUPS):
            gs = slice(g * HEAD_DIM, (g + 1) * HEAD_DIM)
            q8 = q_scr[g]
            sl = slopes(g)
            kn = ksn_ref[:, gs].astype(BF16).astype(F32)
            vn = vsn_ref[:, gs].astype(BF16).astype(F32)
            s_n = jnp.sum(q8.astype(F32) * kn, axis=-1, keepdims=True)
            sel_n = sel_smem[g * SUBLANES, n_blk - 1] > 0
            m_old = m_scr[g][:, 0:1]
            m_new = jnp.maximum(m_old, jnp.where(sel_n, s_n, NEG_INF))
            p_n = jnp.where(sel_n, jnp.exp(s_n - m_new), 0.0)
            alpha = jnp.exp(m_old - m_new)
            acc = alpha * acc_scr[g] + p_n.astype(BF16).astype(F32) * vn
            l = alpha * l_scr[g][:, 0:1] + p_n
            o_s = acc / jnp.maximum(l, TINY)
            p = _masked_softmax(_dot_nt(q8, wk_new[:, gs].astype(BF16)) - sl * wdist.astype(F32), wmask)
            o_w = _dot(p.astype(BF16), wv_new[:, gs].astype(BF16))
            gt = jnp.broadcast_to(gt_ref[:, g * LANES:(g + 1) * LANES], (SUBLANES, LANES))
            gcol = [jnp.sum(jnp.where(lane == k * HEADS_PER_GROUP + rowi, gt, 0.0), axis=-1, keepdims=True)
                    for k in range(3)]
            o = gcol[0] * oc_scr[g] + gcol[1] * o_s + gcol[2] * o_w
            for h in range(HEADS_PER_GROUP):
                hd = HEADS_PER_GROUP * g + h
                o_ref[:, hd * HEAD_DIM:(hd + 1) * HEAD_DIM] = o[h:h + 1]


def _nsa_sample(qbf, ck, cv, ks_new, vs_new, win_k, win_v, kw_new, vw_new, gates, pool_k, pool_v, page_table,
                page_size):
    b, n_pages = page_table.shape
    past_len = n_pages * page_size
    ncp = ck.shape[2]
    n_cmp = (past_len + 1 + CMP_STRIDE - 1) // CMP_STRIDE - 1
    n_blk = (past_len + 1 + SEL_BLOCK - 1) // SEL_BLOCK
    nbp = -(-n_blk // LANES) * LANES
    wb = win_k.shape[1]
    assert wb == WINDOW and n_blk >= TOP_N and ncp == n_cmp and page_size % SEL_BLOCK == 0
    pages = PAGES_PER_STEP if n_pages % PAGES_PER_STEP == 0 else n_pages
    page_rows = page_size * PAGE_ROWS_PER_TOKEN
    a = _imp_matrix(ncp, nbp)
    tok = lambda c: pl.BlockSpec((None, 1, c), lambda i, s, pt: (i, 0, 0))
    csp = pl.BlockSpec((None, KV_GROUPS, ncp, HEAD_DIM), lambda i, s, pt: (i, 0, 0, 0))
    wsp = pl.BlockSpec((None, wb, KV_WIDTH), lambda i, s, pt: (i, 0, 0))
    asp = pl.BlockSpec(a.shape, lambda i, s, pt: (0, 0))
    grp_scr = lambda c, dt: pltpu.VMEM((KV_GROUPS, SUBLANES, c), dt)
    return pl.pallas_call(
        functools.partial(_nsa_sample_kernel, pages=pages, page_size=page_size, n_blk=n_blk, n_cmp=n_cmp,
                          n_pick=TOP_N, past_len=past_len),
        grid_spec=pltpu.PrefetchScalarGridSpec(
            num_scalar_prefetch=1,
            grid=(b, n_pages // pages),
            in_specs=[tok(NSA_WIDTH), csp, csp, asp, tok(KV_WIDTH), tok(KV_WIDTH), wsp, wsp, tok(KV_WIDTH),
                      tok(KV_WIDTH), tok(GATE_COLS)] + _page_specs(page_rows, pages) * 2,
            out_specs=[tok(NSA_WIDTH), wsp, wsp],
            scratch_shapes=[pltpu.VMEM((KV_GROUPS * SUBLANES, nbp), jnp.int32), grp_scr(HEAD_DIM, F32),
                            grp_scr(LANES, F32), grp_scr(LANES, F32), grp_scr(HEAD_DIM, F32), grp_scr(HEAD_DIM, BF16),
                            pltpu.SMEM((KV_GROUPS * SUBLANES, nbp), jnp.int32), pltpu.SemaphoreType.DMA((1,))]),
        out_shape=[jax.ShapeDtypeStruct((b, 1, NSA_WIDTH), F32), jax.ShapeDtypeStruct((b, wb, KV_WIDTH), F32),
                   jax.ShapeDtypeStruct((b, wb, KV_WIDTH), F32)],
        compiler_params=_cparams(("parallel", "arbitrary")),
        name="nsa_sample",
    )(page_table, qbf, ck, cv, a, ks_new, vs_new, win_k, win_v, kw_new, vw_new, gates,
      *([pool_k] * pages), *([pool_v] * pages))


def kernel(x_prompt, x_sample, p_prompt, p_sample, cache_cmp_k, cache_cmp_v, cache_sel_k, cache_sel_v, cache_win_k, cache_win_v, state_ssm_re, state_ssm_im, page_table, norm_mix, w_in, cmp_pe_k, cmp_w1_k, cmp_w2_k, cmp_pe_v, cmp_w1_v, cmp_w2_v, ssm_a_re, ssm_a_im, ssm_log_dt, ssm_b_re, ssm_b_im, ssm_c_re, ssm_c_im, ssm_d, w_glu, b_glu, norm_nsa_out, norm_ssm_out, w_out, norm_ffn, router_grp_w, router_grp_b, router_exp_w, router_exp_b, exp_w1, exp_w3, exp_w2, norm_ple, w_ple_gate, b_ple_gate, w_ple_proj, norm_final):
    depth = w_in.shape[0]
    assert depth == 1, "one layer per step"
    i = 0
    b, s, d = x_prompt.shape
    db, t, _ = x_sample.shape
    assert t == 1, "the sample group decodes one token per sequence"
    ssm_ch = w_glu.shape[1]
    page_size = cache_cmp_k.shape[2]
    n_phys = cache_cmp_k.shape[1]
    wp = min(WINDOW, s)
    row1 = lambda a: a.reshape(1, -1)

    w_inp = _prep_w_in(w_in[i], ssm_ch)
    wk1 = _prep_cmp_w1(cmp_w1_k[i])
    wv1 = _prep_cmp_w1(cmp_w1_v[i])
    s5p = _prep_s5(ssm_a_re[i], ssm_a_im[i], ssm_log_dt[i], ssm_b_re[i], ssm_b_im[i], ssm_c_re[i], ssm_c_im[i])
    router = _prep_router(router_grp_w[i], router_grp_b[i], router_exp_w[i], router_exp_b[i])
    wglu = w_glu[i].astype(BF16)
    wout = w_out[i].astype(BF16)
    wpg = w_ple_gate[i].astype(BF16)
    wpp = w_ple_proj[i].astype(BF16)
    cmp_par = (cmp_pe_k[i], cmp_pe_v[i], cmp_w1_k[i], cmp_w1_v[i], cmp_w2_k[i], cmp_w2_v[i])

    def tail1(h, o_nsa, o_ssm):
        return _tail1(o_nsa, o_ssm, h, row1(norm_nsa_out[i]), row1(norm_ssm_out[i]), wout, row1(norm_ffn[i]), router)

    def tail2(pos, tok_off, h1, rw, y, p):
        return _tail2(pos, tok_off, h1, rw, y, p, row1(norm_ple[i]), wpg, row1(b_ple_gate[i]), wpp, row1(norm_final))

    m = b * s
    hp = x_prompt.reshape(m, d)
    qbf, kc, vc, ks, vs, kw, vw, ksb, vsb, kwb, vwb, u, gates = _inproj(hp, row1(norm_mix[i]), w_inp, ssm_ch, 512)
    seq = lambda a: a.reshape(b, s, a.shape[-1])
    lk, lv = _cmp_prompt(seq(kc), seq(vc), wk1, wv1)
    ck, cv = _cmp_finish(lk, lv, *cmp_par)
    o_nsa = _nsa_prompt(seq(qbf), ck, cv, seq(ksb), seq(vsb), seq(kwb), seq(vwb), seq(gates))
    ns = s5p[0].shape[1]
    zero = jnp.zeros((b, 1, ns), F32)
    o_ssm, hre_p, him_p = _s5(seq(u), zero, zero, s5p, row1(ssm_d[i]), wglu, row1(b_glu[i]), scan=True, precise=False)
    h1_p, xn_p, rid_p, rw_p = tail1(hp, o_nsa.reshape(m, -1), o_ssm.reshape(m, -1))
    kv5 = lambda a: a.reshape(1, b, s, KV_GROUPS, HEAD_DIM)
    win5 = lambda a: a.reshape(b, s, KV_GROUPS, HEAD_DIM)[None, :, s - wp:]
    st4 = lambda a, n: a.reshape(1, n, ns // SSM_STATE, SSM_STATE)
    prompt_state = (kv5(kc), kv5(vc), kv5(ks), kv5(vs), win5(kw), win5(vw), st4(hre_p, b), st4(him_p, b))

    hs = x_sample.reshape(db, d)
    qbf, kc, vc, ks, vs, kw, vw, _, _, _, _, u, gates = _inproj(hs, row1(norm_mix[i]), w_inp, ssm_ch, 512)
    pool = lambda c: c[i].reshape(n_phys * page_size * KV_GROUPS, HEAD_DIM)
    one = lambda a: a.reshape(db, 1, a.shape[-1])
    lk, lv = _cmp_paged(pool(cache_cmp_k), pool(cache_cmp_v), page_table, wk1, wv1, page_size)
    ck, cv = _cmp_finish(lk, lv, *cmp_par, kn=one(kc), vn=one(vc))
    wb = cache_win_k.shape[2]
    o_nsa, win_k, win_v = _nsa_sample(one(qbf), ck, cv, one(ks), one(vs), cache_win_k[i].reshape(db, wb, KV_WIDTH),
                                      cache_win_v[i].reshape(db, wb, KV_WIDTH), one(kw), one(vw), one(gates),
                                      pool(cache_sel_k), pool(cache_sel_v), page_table, page_size)
    o_ssm, hre_s, him_s = _s5(u, state_ssm_re[i].reshape(db, ns), state_ssm_im[i].reshape(db, ns), s5p,
                              row1(ssm_d[i]), wglu, row1(b_glu[i]), scan=False, precise=True)
    h1_s, xn_s, rid_s, rw_s = tail1(hs, o_nsa.reshape(db, -1), o_ssm)

    rid = jnp.concatenate([rid_p[:, :2], rid_s[:, :2]], axis=0)
    pos, tile_expert, row_token, n_tiles = _moe_dispatch(rid)
    y = _moe_sparse(jnp.concatenate([xn_p, xn_s], axis=0), tile_expert, row_token, n_tiles,
                    exp_w1[i], exp_w3[i], exp_w2[i])
    y_prompt = tail2(pos, 0, h1_p, rw_p, y, p_prompt[i].reshape(m, -1)).reshape(b, s, d)
    y_sample = tail2(pos, m, h1_s, rw_s, y, p_sample[i].reshape(db, -1)).reshape(db, 1, d)
    new5 = lambda a: a.reshape(1, db, 1, KV_GROUPS, HEAD_DIM)
    buf5 = lambda a: a.reshape(1, db, wb, KV_GROUPS, HEAD_DIM)
    sample_state = (new5(kc), new5(vc), new5(ks), new5(vs), buf5(win_k), buf5(win_v), st4(hre_s, db), st4(him_s, db))
    return (y_prompt, y_sample) + prompt_state + sample_state
```

```python
import functools

import jax
import jax.numpy as jnp
import numpy as np
from jax import lax
from jax.experimental import pallas as pl
from jax.experimental.pallas import tpu as pltpu

F32 = jnp.float32
BF16 = jnp.bfloat16

HEAD_DIM = 128
N_HEADS = 8
KV_GROUPS = 2
HEADS_PER_GROUP = N_HEADS // KV_GROUPS
NSA_WIDTH = N_HEADS * HEAD_DIM
KV_WIDTH = KV_GROUPS * HEAD_DIM
CMP_STRIDE = 16
CMP_BLOCK = 2 * CMP_STRIDE
SEL_BLOCK = 64
TOP_N = 16
WINDOW = 512
Q_BLOCK = 128
SSM_GROUP_CH = 16
SSM_STATE = 64
N_EXPERT_GROUPS = 4
EXPERTS_PER_GROUP = 8
N_EXPERTS = N_EXPERT_GROUPS * EXPERTS_PER_GROUP
RMS_EPS = 1e-6
NEG_INF = -1e30
FORCE_SCORE = 1e4
TINY = 1e-30

LANES = 128
SUBLANES = 8
GATE_COLS = KV_GROUPS * LANES
ROUTER_COLS = LANES
ROUTER_EXP_OFF = N_EXPERT_GROUPS
SEL_KV_TILE = 512
VMEM_LIMIT_BYTES = 56 * 1024 * 1024


def _cparams(sem):
    return pltpu.CompilerParams(dimension_semantics=sem, vmem_limit_bytes=VMEM_LIMIT_BYTES)


def _rms(x, g):
    return x * lax.rsqrt(jnp.mean(x * x, axis=-1, keepdims=True) + RMS_EPS) * g


def _gelu(x):
    return x * (0.5 * (1.0 + jnp.tanh(0.7978845608028654 * (x + 0.044715 * (x * x * x)))))


def _dot(a, b):
    return jnp.dot(a, b, preferred_element_type=F32)


def _dot_nt(a, b):
    return lax.dot_general(a, b, (((1,), (1,)), ((), ())), preferred_element_type=F32)


def _split2(x):
    hi = x.astype(BF16)
    lo = (x - hi.astype(F32)).astype(BF16)
    return hi, lo


def _split3(x):
    hi = x.astype(BF16)
    r = x - hi.astype(F32)
    mid = r.astype(BF16)
    lo = (r - mid.astype(F32)).astype(BF16)
    return hi, mid, lo


def _masked_softmax(s, mask):
    s = jnp.where(mask, s, NEG_INF)
    m = jnp.max(s, axis=-1, keepdims=True)
    p = jnp.where(mask, jnp.exp(s - m), 0.0)
    return p / jnp.maximum(jnp.sum(p, axis=-1, keepdims=True), TINY)


def _row_tile(m, pref):
    return pref if m % pref == 0 else m


Q_END = NSA_WIDTH
KV_END = Q_END + 6 * KV_WIDTH
U_OFF = KV_END


def _inproj_kernel(x_ref, g_ref, w_ref, qbf, kc, vc, ks, vs, kw, vw, ksb, vsb, kwb, vwb, u, gates, *, ssm_ch):
    xn = _rms(x_ref[...], g_ref[...]).astype(BF16)

    def mm(lo, hi):
        return _dot(xn, w_ref[:, lo:hi])

    qbf[...] = (mm(0, Q_END) * (HEAD_DIM ** -0.5)).astype(BF16)
    for i, (f, b) in enumerate(((kc, None), (vc, None), (ks, ksb), (vs, vsb), (kw, kwb), (vw, vwb))):
        z = mm(Q_END + KV_WIDTH * i, Q_END + KV_WIDTH * (i + 1))
        f[...] = z
        if b is not None:
            b[...] = z.astype(BF16)
    u[...] = mm(U_OFF, U_OFF + ssm_ch)
    gates[...] = jax.nn.sigmoid(mm(U_OFF + ssm_ch, U_OFF + ssm_ch + GATE_COLS))


def _prep_w_in(w_in, ssm_ch):
    wq = w_in[:, :KV_END]
    wg = w_in[:, KV_END:KV_END + 3 * N_HEADS]
    wu = w_in[:, KV_END + 3 * N_HEADS:]
    d = w_in.shape[0]
    wg = wg.reshape(d, KV_GROUPS, HEADS_PER_GROUP, 3).transpose(0, 1, 3, 2).reshape(d, KV_GROUPS, 3 * HEADS_PER_GROUP)
    wg = jnp.pad(wg, ((0, 0), (0, 0), (0, LANES - 3 * HEADS_PER_GROUP))).reshape(d, GATE_COLS)
    return jnp.concatenate([wq, wu, wg], axis=1).astype(BF16)


def _inproj(x, g, w, ssm_ch, tm_pref):
    m, d = x.shape
    tm = _row_tile(m, tm_pref)
    ncols = w.shape[1]
    row = lambda c: pl.BlockSpec((tm, c), lambda i: (i, 0))
    f32s = lambda c: jax.ShapeDtypeStruct((m, c), F32)
    bfs = lambda c: jax.ShapeDtypeStruct((m, c), BF16)
    out_shape = ([bfs(NSA_WIDTH)] + [f32s(KV_WIDTH)] * 6 + [bfs(KV_WIDTH)] * 4 + [f32s(ssm_ch), f32s(GATE_COLS)])
    out_specs = ([row(NSA_WIDTH)] + [row(KV_WIDTH)] * 10 + [row(ssm_ch), row(GATE_COLS)])
    return pl.pallas_call(
        functools.partial(_inproj_kernel, ssm_ch=ssm_ch),
        grid=(m // tm,),
        in_specs=[row(d), pl.BlockSpec((1, d), lambda i: (0, 0)),
                  pl.BlockSpec((d, ncols), lambda i: (0, 0), pipeline_mode=pl.Buffered(1))],
        out_specs=out_specs,
        out_shape=out_shape,
        compiler_params=_cparams(("parallel",)),
        name="inproj",
    )(x, g, w)


CMP_PAIRS = CMP_STRIDE // 2


def _prep_cmp_w1(w1):
    w = jnp.concatenate([w1[:CMP_STRIDE], w1[CMP_STRIDE:]], axis=-1).astype(BF16)
    return w.reshape(CMP_PAIRS, 2 * HEAD_DIM, 2 * HEAD_DIM)


def _cmp_prompt_kernel(kc_ref, vc_ref, wk_ref, wv_ref, ok_ref, ov_ref, *, n_ch):
    for src, w, dst in ((kc_ref, wk_ref, ok_ref), (vc_ref, wv_ref, ov_ref)):
        acc = jnp.zeros((n_ch, 2 * HEAD_DIM), F32)
        for jj in range(CMP_PAIRS):
            xj = jnp.concatenate([src[pl.ds(2 * jj + r, n_ch, stride=CMP_STRIDE), :] for r in range(2)], axis=1)
            acc = acc + _dot(xj.astype(BF16), w[jj])
        dst[...] = acc


def _cmp_prompt(kc, vc, wk, wv):
    b, s, _ = kc.shape
    n_ch = s // CMP_STRIDE
    tok = pl.BlockSpec((None, s, HEAD_DIM), lambda i, g: (i, 0, g))
    wsp = pl.BlockSpec((CMP_PAIRS, 2 * HEAD_DIM, 2 * HEAD_DIM), lambda i, g: (0, 0, 0))
    osp = pl.BlockSpec((None, None, n_ch, 2 * HEAD_DIM), lambda i, g: (i, g, 0, 0))
    osh = jax.ShapeDtypeStruct((b, KV_GROUPS, n_ch, 2 * HEAD_DIM), F32)
    return pl.pallas_call(
        functools.partial(_cmp_prompt_kernel, n_ch=n_ch),
        grid=(b, KV_GROUPS),
        in_specs=[tok, tok, wsp, wsp],
        out_specs=[osp, osp],
        out_shape=[osh, osh],
        compiler_params=_cparams(("parallel", "parallel")),
        name="cmp_prompt",
    )(kc, vc, wk, wv)


def _cmp_finish_kernel(*refs, nch, has_tail):
    if has_tail:
        lk, lv, kn, vn, pek, pev, w1k, w1v, w2k, w2v, ock, ocv = refs
    else:
        lk, lv, pek, pev, w1k, w1v, w2k, w2v, ock, ocv = refs
        kn = vn = None
    row = lax.broadcasted_iota(jnp.int32, (nch, 1), 0)
    half = CMP_STRIDE * HEAD_DIM
    for l_ref, n_ref, pe_ref, w1_ref, w2_ref, o_ref in ((lk, kn, pek, w1k, w2k, ock), (lv, vn, pev, w1v, w2v, ocv)):
        lohi = l_ref[...]
        lo = lohi[:, :HEAD_DIM]
        hi = lohi[:, HEAD_DIM:]
        pe = jnp.broadcast_to(pe_ref[0:1, :], (SUBLANES, half)).astype(BF16)
        b_lo = _dot(pe, w1_ref[0:half, :].astype(BF16))[0:1]
        pe = jnp.broadcast_to(pe_ref[1:2, :], (SUBLANES, half)).astype(BF16)
        b_hi = _dot(pe, w1_ref[half:2 * half, :].astype(BF16))[0:1]
        hi_next = pltpu.roll(hi, nch - 1, 0)
        if has_tail:
            new = jnp.broadcast_to(n_ref[...], (SUBLANES, HEAD_DIM)).astype(BF16)
            tail = _dot(new, w1_ref[half:half + HEAD_DIM, :].astype(BF16))[0:1]
            hi_next = jnp.where(row == nch - 1, tail, hi_next)
        hid = _gelu(lo + hi_next + (b_lo + b_hi))
        o_ref[...] = _dot(hid.astype(BF16), w2_ref[...].astype(BF16)).astype(BF16)


def _cmp_finish(lk, lv, pek, pev, w1k, w1v, w2k, w2v, kn=None, vn=None):
    b, _, nch, _ = lk.shape
    has_tail = kn is not None
    lsp = pl.BlockSpec((None, None, nch, 2 * HEAD_DIM), lambda i, g: (i, g, 0, 0))
    full = lambda a: pl.BlockSpec(a.shape, lambda i, g: (0,) * a.ndim)
    pek2 = pek.reshape(2, CMP_STRIDE * HEAD_DIM)
    pev2 = pev.reshape(2, CMP_STRIDE * HEAD_DIM)
    w1k2 = w1k.reshape(CMP_BLOCK * HEAD_DIM, HEAD_DIM)
    w1v2 = w1v.reshape(CMP_BLOCK * HEAD_DIM, HEAD_DIM)
    args = [lk, lv]
    specs = [lsp, lsp]
    if has_tail:
        nsp = pl.BlockSpec((None, 1, HEAD_DIM), lambda i, g: (i, 0, g))
        args += [kn, vn]
        specs += [nsp, nsp]
    params = [pek2, pev2, w1k2, w1v2, w2k, w2v]
    args += params
    specs += [full(a) for a in params]
    osp = pl.BlockSpec((None, None, nch, HEAD_DIM), lambda i, g: (i, g, 0, 0))
    osh = jax.ShapeDtypeStruct((b, KV_GROUPS, nch, HEAD_DIM), BF16)
    return pl.pallas_call(
        functools.partial(_cmp_finish_kernel, nch=nch, has_tail=has_tail),
        grid=(b, KV_GROUPS),
        in_specs=specs,
        out_specs=[osp, osp],
        out_shape=[osh, osh],
        compiler_params=_cparams(("parallel", "parallel")),
        name="cmp_finish",
    )(*args)


def _imp_matrix(n_cmp_pad, n_blk_pad):
    ratio = SEL_BLOCK // CMP_STRIDE
    c = np.arange(n_cmp_pad)[:, None]
    j = np.arange(n_blk_pad)[None, :]
    return jnp.asarray(((c >= ratio * j - 1) & (c <= ratio * j + ratio - 1)).astype(np.float32), dtype=BF16)


def _expand_matrix(n_blk_pad, n_keys):
    j = np.arange(n_blk_pad)[:, None]
    k = np.arange(n_keys)[None, :]
    return jnp.asarray((k // SEL_BLOCK == j).astype(np.float32), dtype=BF16)


def _group_slopes(g):
    sg = jnp.where(g == 0, 1.0, 2.0 ** -HEADS_PER_GROUP).astype(F32)
    return [jnp.full((Q_BLOCK, 1), 2.0 ** -(h + 1), F32) * sg for h in range(HEADS_PER_GROUP)]


def _nsa_prompt_kernel(q_ref, ck_ref, cv_ref, ks_ref, vs_ref, kw_ref, vw_ref, gt_ref, a_ref, e_ref, o_ref, sk_ref,
                       *, seq, n_cmp, n_blk, n_pick):
    g = pl.program_id(1)
    n = pl.program_id(2)
    rows = HEADS_PER_GROUP * Q_BLOCK
    qb = q_ref[...]
    q = jnp.concatenate([qb[:, h * HEAD_DIM:(h + 1) * HEAD_DIM] for h in range(HEADS_PER_GROUP)], axis=0)
    qpos1 = n * Q_BLOCK + lax.broadcasted_iota(jnp.int32, (Q_BLOCK, 1), 0)
    qpos = jnp.concatenate([qpos1] * HEADS_PER_GROUP, axis=0)
    slope = jnp.concatenate(_group_slopes(g), axis=0)

    ncp = ck_ref.shape[0]
    cidx = lax.broadcasted_iota(jnp.int32, (1, ncp), 1)
    dist = qpos - (cidx * CMP_STRIDE + (CMP_BLOCK - 1))
    mask = (dist >= 0) & (cidx < n_cmp)
    p = _masked_softmax(_dot_nt(q, ck_ref[...]) - slope * dist.astype(F32), mask)
    o_c = _dot(p.astype(BF16), cv_ref[...])
    p_sum = p[0:Q_BLOCK]
    for h in range(1, HEADS_PER_GROUP):
        p_sum = p_sum + p[h * Q_BLOCK:(h + 1) * Q_BLOCK]

    imp = sum(_dot(t, a_ref[...]) for t in _split3(p_sum))
    blk = lax.broadcasted_iota(jnp.int32, (1, LANES), 1)
    cur = qpos1 // SEL_BLOCK
    valid = blk <= cur
    forced = (blk == 0) | (blk == cur) | (blk == cur - 1)
    score = jnp.where(forced & valid, FORCE_SCORE, jnp.where(valid, imp, -FORCE_SCORE))
    score_t = score.T[:n_blk]
    bidx = lax.broadcasted_iota(jnp.int32, (n_blk, 1), 0)
    rank = jnp.zeros((n_blk, Q_BLOCK), F32)
    for j in range(n_blk):
        r = score_t[j:j + 1, :]
        rank = rank + jnp.where((r > score_t) | ((r == score_t) & (j < bidx)), 1.0, 0.0)
    sel_t = jnp.where(rank < n_pick, 1.0, 0.0)
    if n_blk < LANES:
        sel_t = jnp.concatenate([sel_t, jnp.zeros((LANES - n_blk, Q_BLOCK), F32)], axis=0)
    sk_ref[...] = _dot(sel_t.T.astype(BF16), e_ref[...])

    tk = min(SEL_KV_TILE, seq)
    n_tiles = (n * Q_BLOCK + Q_BLOCK + tk - 1) // tk
    col = lax.broadcasted_iota(jnp.int32, (1, HEAD_DIM), 1)
    q_aug = jnp.where(col == 0, slope * float(SEL_BLOCK), jnp.where(col == 1, slope, 0.0)).astype(BF16)
    q2 = jnp.concatenate([q, q_aug], axis=1)
    krow = lax.broadcasted_iota(jnp.int32, (tk, 1), 0)

    def body(t, carry):
        m, l, acc = carry
        k0 = pl.multiple_of(t * tk, tk)
        kpos = k0 + lax.broadcasted_iota(jnp.int32, (1, tk), 1)
        kp_col = k0 + krow
        k_aug = jnp.where(col == 0, kp_col // SEL_BLOCK, jnp.where(col == 1, kp_col % SEL_BLOCK, 0))
        k2 = jnp.concatenate([ks_ref[pl.ds(k0, tk), :], k_aug.astype(F32).astype(BF16)], axis=1)
        picked = sk_ref[:, pl.ds(k0, tk)] > 0.5
        bias = jnp.where((kpos <= qpos1) & picked, 0.0, NEG_INF)
        s = _dot_nt(q2, k2) + jnp.concatenate([bias] * HEADS_PER_GROUP, axis=0)
        m_new = jnp.maximum(m, jnp.max(s, axis=-1, keepdims=True))
        p = jnp.exp(s - m_new)
        alpha = jnp.exp(m - m_new)
        l = alpha * l + jnp.sum(p, axis=-1, keepdims=True)
        acc = alpha * acc + _dot(p.astype(BF16), vs_ref[pl.ds(k0, tk), :])
        return m_new, l, acc

    init = (jnp.full((rows, 1), NEG_INF, F32), jnp.zeros((rows, 1), F32), jnp.zeros((rows, HEAD_DIM), F32))
    _, l, acc = lax.fori_loop(0, n_tiles, body, init)
    o_s = acc / jnp.maximum(l, TINY)

    band = min(WINDOW + Q_BLOCK, seq)
    st = pl.multiple_of(jnp.minimum(jnp.maximum(n * Q_BLOCK - WINDOW, 0), seq - band), Q_BLOCK)
    kpos = st + lax.broadcasted_iota(jnp.int32, (1, band), 1)
    dist = qpos - kpos
    mask = (dist >= 0) & (dist < WINDOW)
    p = _masked_softmax(_dot_nt(q, kw_ref[pl.ds(st, band), :]) - slope * dist.astype(F32), mask)
    o_w = _dot(p.astype(BF16), vw_ref[pl.ds(st, band), :])

    gt = gt_ref[...]
    for h in range(HEADS_PER_GROUP):
        sl = slice(h * Q_BLOCK, (h + 1) * Q_BLOCK)
        o_ref[:, h * HEAD_DIM:(h + 1) * HEAD_DIM] = (
            gt[:, h:h + 1] * o_c[sl]
            + gt[:, HEADS_PER_GROUP + h:HEADS_PER_GROUP + h + 1] * o_s[sl]
            + gt[:, 2 * HEADS_PER_GROUP + h:2 * HEADS_PER_GROUP + h + 1] * o_w[sl])


def _nsa_prompt(qbf, ck, cv, ksb, vsb, kwb, vwb, gates):
    b, s, _ = qbf.shape
    ncp = ck.shape[2]
    n_cmp = s // CMP_STRIDE - 1
    n_blk = s // SEL_BLOCK
    assert s % Q_BLOCK == 0 and TOP_N <= n_blk <= LANES
    a = _imp_matrix(ncp, LANES)
    e = _expand_matrix(LANES, s)
    gw = HEADS_PER_GROUP * HEAD_DIM
    qsp = pl.BlockSpec((None, Q_BLOCK, gw), lambda i, g, n: (i, n, g))
    csp = pl.BlockSpec((None, None, ncp, HEAD_DIM), lambda i, g, n: (i, g, 0, 0))
    ksp = pl.BlockSpec((None, s, HEAD_DIM), lambda i, g, n: (i, 0, g))
    gsp = pl.BlockSpec((None, Q_BLOCK, LANES), lambda i, g, n: (i, n, g))
    full = lambda x: pl.BlockSpec(x.shape, lambda i, g, n: (0,) * x.ndim)
    return pl.pallas_call(
        functools.partial(_nsa_prompt_kernel, seq=s, n_cmp=n_cmp, n_blk=n_blk, n_pick=TOP_N),
        grid=(b, KV_GROUPS, s // Q_BLOCK),
        in_specs=[qsp, csp, csp, ksp, ksp, ksp, ksp, gsp, full(a), full(e)],
        out_specs=qsp,
        out_shape=jax.ShapeDtypeStruct((b, s, NSA_WIDTH), F32),
        scratch_shapes=[pltpu.VMEM((Q_BLOCK, s), F32)],
        compiler_params=_cparams(("parallel", "parallel", "arbitrary")),
        name="nsa_prompt",
    )(qbf, ck, cv, ksb, vsb, kwb, vwb, gates, a, e)


SSM_LANE_GROUPS = LANES // SSM_GROUP_CH
SSM_STATE_TILE = SSM_LANE_GROUPS * SSM_STATE


def _prep_s5(a_re, a_im, log_dt, b_re, b_im, c_re, c_im):
    ng = a_re.shape[0]
    nk = ng // SSM_LANE_GROUPS
    dt = jnp.exp(log_dt)[:, None]
    mag = jnp.exp(a_re * dt)
    lb_re = mag * jnp.cos(a_im * dt)
    lb_im = mag * jnp.sin(a_im * dt)
    den = a_re * a_re + a_im * a_im
    nr = lb_re - 1.0
    f_re = (nr * a_re + lb_im * a_im) / den
    f_im = (lb_im * a_re - nr * a_im) / den
    bb_re = f_re[..., None] * b_re - f_im[..., None] * b_im
    bb_im = f_re[..., None] * b_im + f_im[..., None] * b_re
    eye = jnp.eye(SSM_LANE_GROUPS, dtype=F32)

    def in_map(bb):
        t = bb.reshape(nk, SSM_LANE_GROUPS, SSM_STATE, SSM_GROUP_CH)
        t = jnp.einsum('kgnc,gh->kgchn', t, eye)
        return t.reshape(nk, LANES, SSM_STATE_TILE)

    def out_map(c):
        t = c.reshape(nk, SSM_LANE_GROUPS, SSM_GROUP_CH, SSM_STATE)
        t = jnp.einsum('kgcn,gh->kgnhc', t, eye)
        return t.reshape(nk, SSM_STATE_TILE, LANES)

    bb = jnp.concatenate([in_map(bb_re), in_map(bb_im)], axis=-1)
    cm = jnp.concatenate([out_map(c_re), -out_map(c_im)], axis=1)
    bb_hi = bb.astype(BF16)
    bb_lo = (bb - bb_hi.astype(F32)).astype(BF16)
    lam = jnp.stack([lb_re.reshape(-1), lb_im.reshape(-1)])
    return lam, bb_hi, bb_lo, cm.astype(BF16)


def _s5_kernel(u_ref, h0r_ref, h0i_ref, lam_ref, bbh_ref, bbl_ref, cm_ref, d_ref, wglu_ref, bglu_ref,
               o_ref, hr_ref, hi_ref, *scratch, rows, scan, precise):
    nk = bbh_ref.shape[0]
    st = SSM_STATE_TILE
    if scan:
        cr_scr, ci_scr = scratch
        t = pl.program_id(1)

        @pl.when(t == 0)
        def _():
            cr_scr[...] = h0r_ref[...]
            ci_scr[...] = h0i_ref[...]

        row = lax.broadcasted_iota(jnp.int32, (rows, 1), 0)
    u = u_ref[...]
    ys = []
    for k in range(nk):
        uk = u[:, k * LANES:(k + 1) * LANES]
        if precise:
            uh, ul = _split2(uk)
            bu = _dot(uh, bbh_ref[k]) + _dot(ul, bbh_ref[k]) + _dot(uh, bbl_ref[k])
        else:
            bu = _dot(uk.astype(BF16), bbh_ref[k])
        br = bu[:, :st]
        bi = bu[:, st:]
        ks = slice(k * st, (k + 1) * st)
        lr = lam_ref[0:1, ks]
        li = lam_ref[1:2, ks]
        if scan:
            cr = cr_scr[:, ks]
            ci = ci_scr[:, ks]
            first = row == 0
            br = br + jnp.where(first, lr * cr - li * ci, 0.0)
            bi = bi + jnp.where(first, lr * ci + li * cr, 0.0)
            pr, pi = lr, li
            sh = 1
            while sh < rows:
                sr = jnp.where(row >= sh, pltpu.roll(br, sh, 0), 0.0)
                si = jnp.where(row >= sh, pltpu.roll(bi, sh, 0), 0.0)
                br, bi = br + (pr * sr - pi * si), bi + (pr * si + pi * sr)
                pr, pi = pr * pr - pi * pi, 2.0 * (pr * pi)
                sh *= 2
            cr_scr[:, ks] = br[rows - 1:rows]
            ci_scr[:, ks] = bi[rows - 1:rows]
        else:
            h0r = h0r_ref[:, ks]
            h0i = h0i_ref[:, ks]
            br, bi = br + (lr * h0r - li * h0i), bi + (lr * h0i + li * h0r)
            hr_ref[:, ks] = br
            hi_ref[:, ks] = bi
        hcat = jnp.concatenate([br, bi], axis=1).astype(BF16)
        ys.append(_dot(hcat, cm_ref[k]) + d_ref[:, k * LANES:(k + 1) * LANES] * uk)
    v = _gelu(jnp.concatenate(ys, axis=1))
    o_ref[...] = v * jax.nn.sigmoid(_dot(v.astype(BF16), wglu_ref[...]) + bglu_ref[...])
    if scan:
        hr_ref[...] = cr_scr[...]
        hi_ref[...] = ci_scr[...]


def _s5(u, h0r, h0i, prep, d, wglu, bglu, *, scan, precise, tc_pref=256):
    lam, bbh, bbl, cm = prep
    ns = lam.shape[1]
    full = lambda a: pl.BlockSpec(a.shape, lambda *idx: (0,) * a.ndim)
    params = [lam, bbh, bbl, cm, d, wglu, bglu]
    if scan:
        b, t, c = u.shape
        tc = _row_tile(t, tc_pref)
        grid = (b, t // tc)
        usp = pl.BlockSpec((None, tc, c), lambda i, j: (i, j, 0))
        hsp = pl.BlockSpec((None, 1, ns), lambda i, j: (i, 0, 0))
        rows = tc
        scratch = [pltpu.VMEM((1, ns), F32), pltpu.VMEM((1, ns), F32)]
        sem = ("parallel", "arbitrary")
        osh = [jax.ShapeDtypeStruct((b, t, c), F32)] + [jax.ShapeDtypeStruct((b, 1, ns), F32)] * 2
    else:
        b, c = u.shape
        grid = (1,)
        usp = pl.BlockSpec((b, c), lambda i: (0, 0))
        hsp = pl.BlockSpec((b, ns), lambda i: (0, 0))
        rows = b
        scratch = []
        sem = ("arbitrary",)
        osh = [jax.ShapeDtypeStruct((b, c), F32)] + [jax.ShapeDtypeStruct((b, ns), F32)] * 2
    return pl.pallas_call(
        functools.partial(_s5_kernel, rows=rows, scan=scan, precise=precise),
        grid=grid,
        in_specs=[usp, hsp, hsp] + [full(a) for a in params],
        out_specs=[usp, hsp, hsp],
        out_shape=osh,
        scratch_shapes=scratch,
        compiler_params=_cparams(sem),
        name="s5_scan" if scan else "s5_step",
    )(u, h0r, h0i, *params)


def _prep_router(rg_w, rg_b, re_w, re_b):
    d = rg_w.shape[0]
    w = jnp.concatenate([rg_w, re_w.transpose(1, 0, 2).reshape(d, N_EXPERTS)], axis=1)
    w = jnp.pad(w, ((0, 0), (0, ROUTER_COLS - w.shape[1])))
    b = jnp.pad(jnp.concatenate([rg_b, re_b.reshape(-1)]), (0, ROUTER_COLS - N_EXPERT_GROUPS - N_EXPERTS))
    hi = w.astype(BF16)
    lo = (w - hi.astype(F32)).astype(BF16)
    return hi, lo, b.reshape(1, ROUTER_COLS)


def _tail1_kernel(on_ref, os_ref, h_ref, gn_ref, gs_ref, wout_ref, gf_ref, wrh_ref, wrl_ref, br_ref,
                  h1_ref, xn_ref, rid_ref, rw_ref):
    cat = jnp.concatenate([_rms(on_ref[...], gn_ref[...]), _rms(os_ref[...], gs_ref[...])], axis=1)
    h1 = h_ref[...] + _dot(cat.astype(BF16), wout_ref[...])
    h1_ref[...] = h1
    xn = _rms(h1, gf_ref[...])
    xh, xl = _split2(xn)
    xn_ref[...] = xn
    lg = _dot(xh, wrh_ref[...]) + _dot(xl, wrh_ref[...]) + _dot(xh, wrl_ref[...]) + br_ref[...]
    lane_i = lax.broadcasted_iota(jnp.int32, (1, ROUTER_COLS), 1)
    lane = lane_i.astype(F32)
    ninf = -jnp.inf
    big = float(ROUTER_COLS)
    is_grp = lane_i < N_EXPERT_GROUPS
    lgm = jnp.where(is_grp, lg, ninf)
    mx = jnp.max(lgm, axis=-1, keepdims=True)
    grp = jnp.min(jnp.where(lgm == mx, lane, big), axis=-1, keepdims=True)
    p_grp = 1.0 / jnp.sum(jnp.where(is_grp, jnp.exp(lgm - mx), 0.0), axis=-1, keepdims=True)
    first = ROUTER_EXP_OFF + EXPERTS_PER_GROUP * grp
    lem = jnp.where((lane >= first) & (lane < first + EXPERTS_PER_GROUP), lg, ninf)
    v1 = jnp.max(lem, axis=-1, keepdims=True)
    i1 = jnp.min(jnp.where(lem == v1, lane, big), axis=-1, keepdims=True)
    lem2 = jnp.where(lane == i1, ninf, lem)
    v2 = jnp.max(lem2, axis=-1, keepdims=True)
    i2 = jnp.min(jnp.where(lem2 == v2, lane, big), axis=-1, keepdims=True)
    e2 = jnp.exp(v2 - v1)
    w1 = (1.0 / (1.0 + e2)) * p_grp
    w2 = (e2 / (1.0 + e2)) * p_grp
    rid_ref[...] = jnp.where(lane_i == 0, i1 - ROUTER_EXP_OFF,
                             jnp.where(lane_i == 1, i2 - ROUTER_EXP_OFF, 0.0)).astype(jnp.int32)
    rw_ref[...] = jnp.where(lane_i == 0, w1, jnp.where(lane_i == 1, w2, 0.0))


def _tail1(o_nsa, o_ssm, h, g_nsa, g_ssm, wout, g_ffn, router, tm_pref=256):
    m, d = h.shape
    tm = _row_tile(m, tm_pref)
    wrh, wrl, br = router
    row = lambda c: pl.BlockSpec((tm, c), lambda i: (i, 0))
    full = lambda a: pl.BlockSpec(a.shape, lambda i: (0,) * a.ndim)
    params = [g_nsa, g_ssm, wout, g_ffn, wrh, wrl, br]
    return pl.pallas_call(
        _tail1_kernel,
        grid=(m // tm,),
        in_specs=[row(o_nsa.shape[1]), row(o_ssm.shape[1]), row(d)] + [full(a) for a in params],
        out_specs=[row(d), row(d), row(ROUTER_COLS), row(ROUTER_COLS)],
        out_shape=[jax.ShapeDtypeStruct((m, d), F32), jax.ShapeDtypeStruct((m, d), F32),
                   jax.ShapeDtypeStruct((m, ROUTER_COLS), jnp.int32), jax.ShapeDtypeStruct((m, ROUTER_COLS), F32)],
        compiler_params=_cparams(("parallel",)),
        name="tail1",
    )(o_nsa, o_ssm, h, *params)


MOE_TILE = 256


def _moe_dispatch(rid):
    t = rid.shape[0]
    pairs = 2 * t
    e = rid.reshape(-1)
    onehot = (e[:, None] == jnp.arange(N_EXPERTS, dtype=jnp.int32)[None, :]).astype(jnp.int32)
    csum = jnp.cumsum(onehot, axis=0)
    rank = jnp.take_along_axis(csum, e[:, None], axis=1)[:, 0] - 1
    tiles = (csum[-1] + MOE_TILE - 1) // MOE_TILE
    tile_end = jnp.cumsum(tiles)
    pos = (tile_end - tiles)[e] * MOE_TILE + rank
    nt_max = -(-pairs // MOE_TILE) + N_EXPERTS
    n_tiles = tile_end[-1]
    tile_ids = jnp.arange(nt_max, dtype=jnp.int32)
    tile_expert = jnp.minimum(jnp.sum((tile_end[None, :] <= tile_ids[:, None]).astype(jnp.int32), axis=1), N_EXPERTS - 1)
    tile_expert = jnp.where(tile_ids < n_tiles, tile_expert, tile_expert[n_tiles - 1])
    row_token = jnp.zeros((nt_max * MOE_TILE,), jnp.int32).at[pos].set(jnp.arange(pairs, dtype=jnp.int32) // 2)
    return pos.astype(jnp.int32), tile_expert, row_token, n_tiles.astype(jnp.int32).reshape(1)


def _moe_sparse_kernel(te_ref, rt_ref, nt_ref, x_hbm, w1_ref, w3_ref, w2_ref, y_ref, xbuf, w1b, w3b, w2b, sem):
    t = pl.program_id(0)
    n = nt_ref[0]

    def row_copy(tile, slot, r):
        tok = rt_ref[tile * MOE_TILE + r]
        return pltpu.make_async_copy(x_hbm.at[pl.ds(tok, 1), :], xbuf.at[slot, pl.ds(r, 1), :], sem.at[slot])

    def start_tile(tile, slot):
        def body(r, c):
            row_copy(tile, slot, r).start()
            return c
        lax.fori_loop(0, MOE_TILE, body, 0, unroll=8)

    def wait_tile(tile, slot):
        def body(r, c):
            row_copy(tile, slot, r).wait()
            return c
        lax.fori_loop(0, MOE_TILE, body, 0, unroll=8)

    @pl.when(t == 0)
    def _():
        start_tile(0, 0)

    @pl.when(t + 1 < n)
    def _():
        start_tile(t + 1, (t + 1) % 2)

    @pl.when(t < n)
    def _():
        slot = t % 2
        wait_tile(t, slot)

        @pl.when((t == 0) | (te_ref[t] != te_ref[jnp.maximum(t - 1, 0)]))
        def _():
            w1b[...] = w1_ref[...].astype(BF16)
            w3b[...] = w3_ref[...].astype(BF16)
            w2b[...] = w2_ref[...].astype(BF16)

        x = xbuf[slot].astype(BF16)
        a = _dot(x, w1b[...])
        hid = (a * jax.nn.sigmoid(a)) * _dot(x, w3b[...])
        y_ref[...] = _dot(hid.astype(BF16), w2b[...])

    @pl.when(t >= n)
    def _():
        y_ref[...] = jnp.zeros_like(y_ref)


def _moe_sparse(xn, tile_expert, row_token, n_tiles, w1, w3, w2):
    _, d = xn.shape
    _, _, hdim = w1.shape
    nt_max = tile_expert.shape[0]
    wsp = lambda r, c: pl.BlockSpec((None, r, c), lambda t, te, rt, nt: (te[t], 0, 0))
    return pl.pallas_call(
        _moe_sparse_kernel,
        grid_spec=pltpu.PrefetchScalarGridSpec(
            num_scalar_prefetch=3,
            grid=(nt_max,),
            in_specs=[pl.BlockSpec(memory_space=pl.ANY), wsp(d, hdim), wsp(d, hdim), wsp(hdim, d)],
            out_specs=pl.BlockSpec((MOE_TILE, d), lambda t, te, rt, nt: (t, 0)),
            scratch_shapes=[pltpu.VMEM((2, MOE_TILE, d), F32), pltpu.VMEM((d, hdim), BF16),
                            pltpu.VMEM((d, hdim), BF16), pltpu.VMEM((hdim, d), BF16),
                            pltpu.SemaphoreType.DMA((2,))]),
        out_shape=jax.ShapeDtypeStruct((nt_max * MOE_TILE, d), F32),
        compiler_params=_cparams(("arbitrary",)),
        name="moe_sparse",
    )(tile_expert, row_token, n_tiles, xn, w1, w3, w2)


def _tail2_kernel(pos_ref, h1_ref, rw_ref, y_hbm, p_ref, gp_ref, wg_ref, bg_ref, wp_ref, gfin_ref, o_ref, ybuf, sem,
                  *, tm, tok_off):
    base = (tok_off + pl.program_id(0) * tm) * 2

    def row_copy(r, k):
        return pltpu.make_async_copy(y_hbm.at[pl.ds(pos_ref[base + 2 * r + k], 1), :],
                                     ybuf.at[k, pl.ds(r, 1), :], sem.at[0])

    def each_row(fn):
        def body(r, c):
            fn(row_copy(r, 0))
            fn(row_copy(r, 1))
            return c
        lax.fori_loop(0, tm, body, 0, unroll=8)

    each_row(lambda cp: cp.start())
    emb = _dot(p_ref[...].astype(BF16), wp_ref[...])
    each_row(lambda cp: cp.wait())
    rw = rw_ref[...]
    h = h1_ref[...] + (rw[:, 0:1] * ybuf[0] + rw[:, 1:2] * ybuf[1])
    gate = jax.nn.sigmoid(_dot(_rms(h, gp_ref[...]).astype(BF16), wg_ref[...]) + bg_ref[...])
    h = h + gate * emb
    o_ref[...] = _rms(h, gfin_ref[...])


def _tail2(pos, tok_off, h1, rw, y, p, g_ple, wg, bg, wp, g_fin, tm_pref=256):
    m, d = h1.shape
    tm = _row_tile(m, tm_pref)
    row = lambda c: pl.BlockSpec((tm, c), lambda i, ps: (i, 0))
    full = lambda a: pl.BlockSpec(a.shape, lambda i, ps: (0,) * a.ndim)
    params = [g_ple, wg, bg, wp, g_fin]
    return pl.pallas_call(
        functools.partial(_tail2_kernel, tm=tm, tok_off=tok_off),
        grid_spec=pltpu.PrefetchScalarGridSpec(
            num_scalar_prefetch=1,
            grid=(m // tm,),
            in_specs=[row(d), row(ROUTER_COLS), pl.BlockSpec(memory_space=pl.ANY), row(p.shape[1])]
            + [full(a) for a in params],
            out_specs=row(d),
            scratch_shapes=[pltpu.VMEM((2, tm, d), F32), pltpu.SemaphoreType.DMA((1,))]),
        out_shape=jax.ShapeDtypeStruct((m, d), F32),
        compiler_params=_cparams(("arbitrary",)),
        name="tail2",
    )(pos, h1, rw, y, p, *params)


PAGES_PER_STEP = 16
CMP_PAGES_PER_STEP = 32
PAGE_ROWS_PER_TOKEN = KV_GROUPS


def _cmp_paged_kernel(pt_ref, *refs, pages):
    kp = refs[:pages]
    vp = refs[pages:2 * pages]
    wk_ref, wv_ref, ok_ref, ov_ref = refs[2 * pages:]
    chunks = kp[0].shape[0] // (PAGE_ROWS_PER_TOKEN * CMP_STRIDE)
    stride = PAGE_ROWS_PER_TOKEN * CMP_STRIDE
    for src, w, dst in ((kp, wk_ref, ok_ref), (vp, wv_ref, ov_ref)):
        for g in range(KV_GROUPS):
            acc = jnp.zeros((pages * chunks, 2 * HEAD_DIM), F32)
            for jj in range(CMP_PAIRS):
                xj = jnp.concatenate(
                    [jnp.concatenate([p[pl.ds(PAGE_ROWS_PER_TOKEN * (2 * jj + r) + g, chunks, stride=stride), :]
                                      for p in src], axis=0) for r in range(2)], axis=1)
                acc = acc + _dot(xj.astype(BF16), w[jj])
            dst[g] = acc


def _page_specs(page_rows, pages):
    return [pl.BlockSpec((page_rows, HEAD_DIM), functools.partial(lambda i, b, s, pt: (pt[b, s * pages + i], 0), i))
            for i in range(pages)]


def _cmp_paged(pool_k, pool_v, page_table, wk, wv, page_size):
    b, n_pages = page_table.shape
    pages = CMP_PAGES_PER_STEP if n_pages % CMP_PAGES_PER_STEP == 0 else n_pages
    page_rows = page_size * PAGE_ROWS_PER_TOKEN
    chunks = page_size // CMP_STRIDE
    wsp = pl.BlockSpec((CMP_PAIRS, 2 * HEAD_DIM, 2 * HEAD_DIM), lambda i, s, pt: (0, 0, 0))
    osp = pl.BlockSpec((None, KV_GROUPS, pages * chunks, 2 * HEAD_DIM), lambda i, s, pt: (i, 0, s, 0))
    osh = jax.ShapeDtypeStruct((b, KV_GROUPS, n_pages * chunks, 2 * HEAD_DIM), F32)
    return pl.pallas_call(
        functools.partial(_cmp_paged_kernel, pages=pages),
        grid_spec=pltpu.PrefetchScalarGridSpec(
            num_scalar_prefetch=1,
            grid=(b, n_pages // pages),
            in_specs=_page_specs(page_rows, pages) * 2 + [wsp, wsp],
            out_specs=[osp, osp]),
        out_shape=[osh, osh],
        compiler_params=_cparams(("parallel", "arbitrary")),
        name="cmp_paged",
    )(page_table, *([pool_k] * pages), *([pool_v] * pages), wk, wv)


def _nsa_sample_kernel(pt_ref, q_ref, ck_ref, cv_ref, a_ref, ksn_ref, vsn_ref, wk_ref, wv_ref, kwn_ref, vwn_ref,
                       gt_ref, *rest, pages, page_size, n_blk, n_cmp, n_pick, past_len):
    kp = rest[:pages]
    vp = rest[pages:2 * pages]
    o_ref, wko_ref, wvo_ref, sel_scr, oc_scr, m_scr, l_scr, acc_scr, q_scr, sel_smem, sel_sem = rest[2 * pages:]
    step = pl.program_id(1)
    last = pl.num_programs(1) - 1
    qpos = past_len
    nbp = sel_scr.shape[1]
    rowi = lax.broadcasted_iota(jnp.int32, (SUBLANES, 1), 0)
    lane = lax.broadcasted_iota(jnp.int32, (1, LANES), 1)
    blk = lax.broadcasted_iota(jnp.int32, (1, nbp), 1)

    def slopes(g):
        sl = jnp.zeros((SUBLANES, 1), F32)
        for h in range(HEADS_PER_GROUP):
            sl = jnp.where(rowi == h, 2.0 ** -(HEADS_PER_GROUP * g + h + 1), sl)
        return sl

    @pl.when(step == 0)
    def _():
        qrow = q_ref[...]
        ncp = ck_ref.shape[1]
        cidx = lax.broadcasted_iota(jnp.int32, (1, ncp), 1)
        dist = jnp.broadcast_to(qpos - (cidx * CMP_STRIDE + (CMP_BLOCK - 1)), (SUBLANES, ncp))
        mask = (dist >= 0) & (cidx < n_cmp)
        cur = qpos // SEL_BLOCK
        valid = blk <= cur
        forced = (blk == 0) | (blk == cur) | (blk == cur - 1)
        for g in range(KV_GROUPS):
            heads = [qrow[:, (HEADS_PER_GROUP * g + h) * HEAD_DIM:(HEADS_PER_GROUP * g + h + 1) * HEAD_DIM]
                     for h in range(HEADS_PER_GROUP)]
            q8 = jnp.concatenate(heads + [jnp.zeros((SUBLANES - HEADS_PER_GROUP, HEAD_DIM), BF16)], axis=0)
            q_scr[g] = q8
            p = _masked_softmax(_dot_nt(q8, ck_ref[g]) - slopes(g) * dist.astype(F32), mask)
            oc_scr[g] = _dot(p.astype(BF16), cv_ref[g])
            p_sum = p[0:1]
            for h in range(1, HEADS_PER_GROUP):
                p_sum = p_sum + p[h:h + 1]
            p_sum = jnp.broadcast_to(p_sum, (SUBLANES, ncp))
            imp = sum(_dot(t, a_ref[...]) for t in _split3(p_sum))
            score = jnp.where(forced & valid, FORCE_SCORE, jnp.where(valid, imp, -FORCE_SCORE))
            score = jnp.where(blk < n_blk, score, -2.0 * FORCE_SCORE)

            cnt = jnp.zeros((SUBLANES, nbp), F32)
            for c in range(-(-n_blk // SUBLANES)):
                j = SUBLANES * c + rowi
                sj = jnp.max(jnp.where(blk == j, score, -jnp.inf), axis=-1, keepdims=True)
                before = ((sj > score) | ((sj == score) & (j < blk))) & (j < n_blk)
                cnt = cnt + jnp.where(before, 1.0, 0.0)
            rank = jnp.broadcast_to(jnp.sum(cnt, axis=0, keepdims=True), (SUBLANES, nbp))
            sel_scr[g * SUBLANES:(g + 1) * SUBLANES, :] = jnp.where((rank < n_pick) & (blk < n_blk), 1, 0)
            m_scr[g] = jnp.full((SUBLANES, LANES), NEG_INF, F32)
            l_scr[g] = jnp.zeros((SUBLANES, LANES), F32)
            acc_scr[g] = jnp.zeros((SUBLANES, HEAD_DIM), F32)
        flags = pltpu.make_async_copy(sel_scr, sel_smem, sel_sem.at[0])
        flags.start()
        flags.wait()

    blocks_per_page = page_size // SEL_BLOCK
    for g in range(KV_GROUPS):
        for i in range(pages):
            pg = step * pages + i
            first_blk = blocks_per_page * pg
            flags = [sel_smem[g * SUBLANES, first_blk + r] for r in range(blocks_per_page)]

            @pl.when(sum(flags) > 0)
            def _(g=g, i=i, pg=pg, flags=flags):
                k = kp[i][pl.ds(g, page_size, stride=PAGE_ROWS_PER_TOKEN), :].astype(BF16)
                v = vp[i][pl.ds(g, page_size, stride=PAGE_ROWS_PER_TOKEN), :].astype(BF16)
                dist = qpos - (pg * page_size + lane)
                picked = jnp.zeros((1, LANES), jnp.int32)
                for r in range(blocks_per_page):
                    picked = jnp.where(lane // SEL_BLOCK == r, flags[r], picked)
                mask = jnp.broadcast_to((dist >= 0) & (picked > 0), (SUBLANES, LANES))
                s = jnp.where(mask, _dot_nt(q_scr[g], k) - slopes(g) * dist.astype(F32), NEG_INF)
                m_old = m_scr[g][:, 0:1]
                m_new = jnp.maximum(m_old, jnp.max(s, axis=-1, keepdims=True))
                p = jnp.where(mask, jnp.exp(s - m_new), 0.0)
                alpha = jnp.exp(m_old - m_new)
                acc_scr[g] = alpha * acc_scr[g] + _dot(p.astype(BF16), v)
                l_scr[g] = jnp.broadcast_to(alpha * l_scr[g][:, 0:1] + jnp.sum(p, axis=-1, keepdims=True),
                                            (SUBLANES, LANES))
                m_scr[g] = jnp.broadcast_to(m_new, (SUBLANES, LANES))

    @pl.when(step == last)
    def _():
        wb = wk_ref.shape[0]
        wrow = lax.broadcasted_iota(jnp.int32, (wb, 1), 0)
        wk_new = jnp.where(wrow == wb - 1, kwn_ref[...], pltpu.roll(wk_ref[...], wb - 1, 0))
        wv_new = jnp.where(wrow == wb - 1, vwn_ref[...], pltpu.roll(wv_ref[...], wb - 1, 0))
        wko_ref[...] = wk_new
        wvo_ref[...] = wv_new
        widx = lax.broadcasted_iota(jnp.int32, (1, wb), 1)
        wdist = jnp.broadcast_to(wb - 1 - widx, (SUBLANES, wb))
        wmask = (wdist >= 0) & (wdist < WINDOW) & (qpos - wdist >= 0)
        for g in range(KV_GROUPS):
            gs = slice(g * HEAD_DIM, (g + 1) * HEAD_DIM)
            q8 = q_scr[g]
            sl = slopes(g)
            kn = ksn_ref[:, gs].astype(BF16).astype(F32)
            vn = vsn_ref[:, gs].astype(BF16).astype(F32)
            s_n = jnp.sum(q8.astype(F32) * kn, axis=-1, keepdims=True)
            sel_n = sel_smem[g * SUBLANES, n_blk - 1] > 0
            m_old = m_scr[g][:, 0:1]
            m_new = jnp.maximum(m_old, jnp.where(sel_n, s_n, NEG_INF))
            p_n = jnp.where(sel_n, jnp.exp(s_n - m_new), 0.0)
            alpha = jnp.exp(m_old - m_new)
            acc = alpha * acc_scr[g] + p_n.astype(BF16).astype(F32) * vn
            l = alpha * l_scr[g][:, 0:1] + p_n
            o_s = acc / jnp.maximum(l, TINY)
            p = _masked_softmax(_dot_nt(q8, wk_new[:, gs].astype(BF16)) - sl * wdist.astype(F32), wmask)
            o_w = _dot(p.astype(BF16), wv_new[:, gs].astype(BF16))
            gt = jnp.broadcast_to(gt_ref[:, g * LANES:(g + 1) * LANES], (SUBLANES, LANES))
            gcol = [jnp.sum(jnp.where(lane == k * HEADS_PER_GROUP + rowi, gt, 0.0), axis=-1, keepdims=True)
                    for k in range(3)]
            o = gcol[0] * oc_scr[g] + gcol[1] * o_s + gcol[2] * o_w
            for h in range(HEADS_PER_GROUP):
                hd = HEADS_PER_GROUP * g + h
                o_ref[:, hd * HEAD_DIM:(hd + 1) * HEAD_DIM] = o[h:h + 1]


def _nsa_sample(qbf, ck, cv, ks_new, vs_new, win_k, win_v, kw_new, vw_new, gates, pool_k, pool_v, page_table,
                page_size):
    b, n_pages = page_table.shape
    past_len = n_pages * page_size
    ncp = ck.shape[2]
    n_cmp = (past_len + 1 + CMP_STRIDE - 1) // CMP_STRIDE - 1
    n_blk = (past_len + 1 + SEL_BLOCK - 1) // SEL_BLOCK
    nbp = -(-n_blk // LANES) * LANES
    wb = win_k.shape[1]
    assert wb == WINDOW and n_blk >= TOP_N and ncp == n_cmp and page_size % SEL_BLOCK == 0
    pages = PAGES_PER_STEP if n_pages % PAGES_PER_STEP == 0 else n_pages
    page_rows = page_size * PAGE_ROWS_PER_TOKEN
    a = _imp_matrix(ncp, nbp)
    tok = lambda c: pl.BlockSpec((None, 1, c), lambda i, s, pt: (i, 0, 0))
    csp = pl.BlockSpec((None, KV_GROUPS, ncp, HEAD_DIM), lambda i, s, pt: (i, 0, 0, 0))
    wsp = pl.BlockSpec((None, wb, KV_WIDTH), lambda i, s, pt: (i, 0, 0))
    asp = pl.BlockSpec(a.shape, lambda i, s, pt: (0, 0))
    grp_scr = lambda c, dt: pltpu.VMEM((KV_GROUPS, SUBLANES, c), dt)
    return pl.pallas_call(
        functools.partial(_nsa_sample_kernel, pages=pages, page_size=page_size, n_blk=n_blk, n_cmp=n_cmp,
                          n_pick=TOP_N, past_len=past_len),
        grid_spec=pltpu.PrefetchScalarGridSpec(
            num_scalar_prefetch=1,
            grid=(b, n_pages // pages),
            in_specs=[tok(NSA_WIDTH), csp, csp, asp, tok(KV_WIDTH), tok(KV_WIDTH), wsp, wsp, tok(KV_WIDTH),
                      tok(KV_WIDTH), tok(GATE_COLS)] + _page_specs(page_rows, pages) * 2,
            out_specs=[tok(NSA_WIDTH), wsp, wsp],
            scratch_shapes=[pltpu.VMEM((KV_GROUPS * SUBLANES, nbp), jnp.int32), grp_scr(HEAD_DIM, F32),
                            grp_scr(LANES, F32), grp_scr(LANES, F32), grp_scr(HEAD_DIM, F32), grp_scr(HEAD_DIM, BF16),
                            pltpu.SMEM((KV_GROUPS * SUBLANES, nbp), jnp.int32), pltpu.SemaphoreType.DMA((1,))]),
        out_shape=[jax.ShapeDtypeStruct((b, 1, NSA_WIDTH), F32), jax.ShapeDtypeStruct((b, wb, KV_WIDTH), F32),
                   jax.ShapeDtypeStruct((b, wb, KV_WIDTH), F32)],
        compiler_params=_cparams(("parallel", "arbitrary")),
        name="nsa_sample",
    )(page_table, qbf, ck, cv, a, ks_new, vs_new, win_k, win_v, kw_new, vw_new, gates,
      *([pool_k] * pages), *([pool_v] * pages))


def kernel(x_prompt, x_sample, p_prompt, p_sample, cache_cmp_k, cache_cmp_v, cache_sel_k, cache_sel_v, cache_win_k, cache_win_v, state_ssm_re, state_ssm_im, page_table, norm_mix, w_in, cmp_pe_k, cmp_w1_k, cmp_w2_k, cmp_pe_v, cmp_w1_v, cmp_w2_v, ssm_a_re, ssm_a_im, ssm_log_dt, ssm_b_re, ssm_b_im, ssm_c_re, ssm_c_im, ssm_d, w_glu, b_glu, norm_nsa_out, norm_ssm_out, w_out, norm_ffn, router_grp_w, router_grp_b, router_exp_w, router_exp_b, exp_w1, exp_w3, exp_w2, norm_ple, w_ple_gate, b_ple_gate, w_ple_proj, norm_final):
    depth = w_in.shape[0]
    assert depth == 1, "one layer per step"
    i = 0
    b, s, d = x_prompt.shape
    db, t, _ = x_sample.shape
    assert t == 1, "the sample group decodes one token per sequence"
    ssm_ch = w_glu.shape[1]
    page_size = cache_cmp_k.shape[2]
    n_phys = cache_cmp_k.shape[1]
    wp = min(WINDOW, s)
    row1 = lambda a: a.reshape(1, -1)

    w_inp = _prep_w_in(w_in[i], ssm_ch)
    wk1 = _prep_cmp_w1(cmp_w1_k[i])
    wv1 = _prep_cmp_w1(cmp_w1_v[i])
    s5p = _prep_s5(ssm_a_re[i], ssm_a_im[i], ssm_log_dt[i], ssm_b_re[i], ssm_b_im[i], ssm_c_re[i], ssm_c_im[i])
    router = _prep_router(router_grp_w[i], router_grp_b[i], router_exp_w[i], router_exp_b[i])
    wglu = w_glu[i].astype(BF16)
    wout = w_out[i].astype(BF16)
    wpg = w_ple_gate[i].astype(BF16)
    wpp = w_ple_proj[i].astype(BF16)
    cmp_par = (cmp_pe_k[i], cmp_pe_v[i], cmp_w1_k[i], cmp_w1_v[i], cmp_w2_k[i], cmp_w2_v[i])

    def tail1(h, o_nsa, o_ssm):
        return _tail1(o_nsa, o_ssm, h, row1(norm_nsa_out[i]), row1(norm_ssm_out[i]), wout, row1(norm_ffn[i]), router)

    def tail2(pos, tok_off, h1, rw, y, p):
        return _tail2(pos, tok_off, h1, rw, y, p, row1(norm_ple[i]), wpg, row1(b_ple_gate[i]), wpp, row1(norm_final))

    m = b * s
    hp = x_prompt.reshape(m, d)
    qbf, kc, vc, ks, vs, kw, vw, ksb, vsb, kwb, vwb, u, gates = _inproj(hp, row1(norm_mix[i]), w_inp, ssm_ch, 512)
    seq = lambda a: a.reshape(b, s, a.shape[-1])
    lk, lv = _cmp_prompt(seq(kc), seq(vc), wk1, wv1)
    ck, cv = _cmp_finish(lk, lv, *cmp_par)
    o_nsa = _nsa_prompt(seq(qbf), ck, cv, seq(ksb), seq(vsb), seq(kwb), seq(vwb), seq(gates))
    ns = s5p[0].shape[1]
    zero = jnp.zeros((b, 1, ns), F32)
    o_ssm, hre_p, him_p = _s5(seq(u), zero, zero, s5p, row1(ssm_d[i]), wglu, row1(b_glu[i]), scan=True, precise=False)
    h1_p, xn_p, rid_p, rw_p = tail1(hp, o_nsa.reshape(m, -1), o_ssm.reshape(m, -1))
    kv5 = lambda a: a.reshape(1, b, s, KV_GROUPS, HEAD_DIM)
    win5 = lambda a: a.reshape(b, s, KV_GROUPS, HEAD_DIM)[None, :, s - wp:]
    st4 = lambda a, n: a.reshape(1, n, ns // SSM_STATE, SSM_STATE)
    prompt_state = (kv5(kc), kv5(vc), kv5(ks), kv5(vs), win5(kw), win5(vw), st4(hre_p, b), st4(him_p, b))

    hs = x_sample.reshape(db, d)
    qbf, kc, vc, ks, vs, kw, vw, _, _, _, _, u, gates = _inproj(hs, row1(norm_mix[i]), w_inp, ssm_ch, 512)
    pool = lambda c: c[i].reshape(n_phys * page_size * KV_GROUPS, HEAD_DIM)
    one = lambda a: a.reshape(db, 1, a.shape[-1])
    lk, lv = _cmp_paged(pool(cache_cmp_k), pool(cache_cmp_v), page_table, wk1, wv1, page_size)
    ck, cv = _cmp_finish(lk, lv, *cmp_par, kn=one(kc), vn=one(vc))
    wb = cache_win_k.shape[2]
    o_nsa, win_k, win_v = _nsa_sample(one(qbf), ck, cv, one(ks), one(vs), cache_win_k[i].reshape(db, wb, KV_WIDTH),
                                      cache_win_v[i].reshape(db, wb, KV_WIDTH), one(kw), one(vw), one(gates),
                                      pool(cache_sel_k), pool(cache_sel_v), page_table, page_size)
    o_ssm, hre_s, him_s = _s5(u, state_ssm_re[i].reshape(db, ns), state_ssm_im[i].reshape(db, ns), s5p,
                              row1(ssm_d[i]), wglu, row1(b_glu[i]), scan=False, precise=True)
    h1_s, xn_s, rid_s, rw_s = tail1(hs, o_nsa.reshape(db, -1), o_ssm)

    rid = jnp.concatenate([rid_p[:, :2], rid_s[:, :2]], axis=0)
    pos, tile_expert, row_token, n_tiles = _moe_dispatch(rid)
    y = _moe_sparse(jnp.concatenate([xn_p, xn_s], axis=0), tile_expert, row_token, n_tiles,
                    exp_w1[i], exp_w3[i], exp_w2[i])
    y_prompt = tail2(pos, 0, h1_p, rw_p, y, p_prompt[i].reshape(m, -1)).reshape(b, s, d)
    y_sample = tail2(pos, m, h1_s, rw_s, y, p_sample[i].reshape(db, -1)).reshape(db, 1, d)
    new5 = lambda a: a.reshape(1, db, 1, KV_GROUPS, HEAD_DIM)
    buf5 = lambda a: a.reshape(1, db, wb, KV_GROUPS, HEAD_DIM)
    sample_state = (new5(kc), new5(vc), new5(ks), new5(vs), buf5(win_k), buf5(win_v), st4(hre_s, db), st4(him_s, db))
    return (y_prompt, y_sample) + prompt_state + sample_state
```

```python
import functools

import jax
import jax.numpy as jnp
import numpy as np
from jax import lax
from jax.experimental import pallas as pl
from jax.experimental.pallas import tpu as pltpu

F32 = jnp.float32
BF16 = jnp.bfloat16

HEAD_DIM = 128
N_HEADS = 8
KV_GROUPS = 2
HEADS_PER_GROUP = N_HEADS // KV_GROUPS
NSA_WIDTH = N_HEADS * HEAD_DIM
KV_WIDTH = KV_GROUPS * HEAD_DIM
CMP_STRIDE = 16
CMP_BLOCK = 2 * CMP_STRIDE
SEL_BLOCK = 64
TOP_N = 16
WINDOW = 512
Q_BLOCK = 128
SSM_GROUP_CH = 16
SSM_STATE = 64
N_EXPERT_GROUPS = 4
EXPERTS_PER_GROUP = 8
N_EXPERTS = N_EXPERT_GROUPS * EXPERTS_PER_GROUP
RMS_EPS = 1e-6
NEG_INF = -1e30
FORCE_SCORE = 1e4
TINY = 1e-30

LANES = 128
SUBLANES = 8
GATE_COLS = KV_GROUPS * LANES
ROUTER_COLS = LANES
ROUTER_EXP_OFF = N_EXPERT_GROUPS
SEL_KV_TILE = 512
VMEM_LIMIT_BYTES = 56 * 1024 * 1024


def _cparams(sem):
    return pltpu.CompilerParams(dimension_semantics=sem, vmem_limit_bytes=VMEM_LIMIT_BYTES)


def _rms(x, g):
    return x * lax.rsqrt(jnp.mean(x * x, axis=-1, keepdims=True) + RMS_EPS) * g


def _gelu(x):
    return x * (0.5 * (1.0 + jnp.tanh(0.7978845608028654 * (x + 0.044715 * (x * x * x)))))


def _dot(a, b):
    return jnp.dot(a, b, preferred_element_type=F32)


def _dot_nt(a, b):
    return lax.dot_general(a, b, (((1,), (1,)), ((), ())), preferred_element_type=F32)


def _split2(x):
    hi = x.astype(BF16)
    lo = (x - hi.astype(F32)).astype(BF16)
    return hi, lo


def _split3(x):
    hi = x.astype(BF16)
    r = x - hi.astype(F32)
    mid = r.astype(BF16)
    lo = (r - mid.astype(F32)).astype(BF16)
    return hi, mid, lo


def _masked_softmax(s, mask):
    s = jnp.where(mask, s, NEG_INF)
    m = jnp.max(s, axis=-1, keepdims=True)
    p = jnp.where(mask, jnp.exp(s - m), 0.0)
    return p / jnp.maximum(jnp.sum(p, axis=-1, keepdims=True), TINY)


def _row_tile(m, pref):
    return pref if m % pref == 0 else m


Q_END = NSA_WIDTH
KV_END = Q_END + 6 * KV_WIDTH
U_OFF = KV_END


def _inproj_kernel(x_ref, g_ref, w_ref, qbf, kc, vc, ks, vs, kw, vw, ksb, vsb, kwb, vwb, u, gates, *, ssm_ch):
    xn = _rms(x_ref[...], g_ref[...]).astype(BF16)

    def mm(lo, hi):
        return _dot(xn, w_ref[:, lo:hi])

    qbf[...] = (mm(0, Q_END) * (HEAD_DIM ** -0.5)).astype(BF16)
    for i, (f, b) in enumerate(((kc, None), (vc, None), (ks, ksb), (vs, vsb), (kw, kwb), (vw, vwb))):
        z = mm(Q_END + KV_WIDTH * i, Q_END + KV_WIDTH * (i + 1))
        f[...] = z
        if b is not None:
            b[...] = z.astype(BF16)
    u[...] = mm(U_OFF, U_OFF + ssm_ch)
    gates[...] = jax.nn.sigmoid(mm(U_OFF + ssm_ch, U_OFF + ssm_ch + GATE_COLS))


def _prep_w_in(w_in, ssm_ch):
    wq = w_in[:, :KV_END]
    wg = w_in[:, KV_END:KV_END + 3 * N_HEADS]
    wu = w_in[:, KV_END + 3 * N_HEADS:]
    d = w_in.shape[0]
    wg = wg.reshape(d, KV_GROUPS, HEADS_PER_GROUP, 3).transpose(0, 1, 3, 2).reshape(d, KV_GROUPS, 3 * HEADS_PER_GROUP)
    wg = jnp.pad(wg, ((0, 0), (0, 0), (0, LANES - 3 * HEADS_PER_GROUP))).reshape(d, GATE_COLS)
    return jnp.concatenate([wq, wu, wg], axis=1).astype(BF16)


def _inproj(x, g, w, ssm_ch, tm_pref):
    m, d = x.shape
    tm = _row_tile(m, tm_pref)
    ncols = w.shape[1]
    row = lambda c: pl.BlockSpec((tm, c), lambda i: (i, 0))
    f32s = lambda c: jax.ShapeDtypeStruct((m, c), F32)
    bfs = lambda c: jax.ShapeDtypeStruct((m, c), BF16)
    out_shape = ([bfs(NSA_WIDTH)] + [f32s(KV_WIDTH)] * 6 + [bfs(KV_WIDTH)] * 4 + [f32s(ssm_ch), f32s(GATE_COLS)])
    out_specs = ([row(NSA_WIDTH)] + [row(KV_WIDTH)] * 10 + [row(ssm_ch), row(GATE_COLS)])
    return pl.pallas_call(
        functools.partial(_inproj_kernel, ssm_ch=ssm_ch),
        grid=(m // tm,),
        in_specs=[row(d), pl.BlockSpec((1, d), lambda i: (0, 0)),
                  pl.BlockSpec((d, ncols), lambda i: (0, 0), pipeline_mode=pl.Buffered(1))],
        out_specs=out_specs,
        out_shape=out_shape,
        compiler_params=_cparams(("parallel",)),
        name="inproj",
    )(x, g, w)


CMP_PAIRS = CMP_STRIDE // 2


def _prep_cmp_w1(w1):
    w = jnp.concatenate([w1[:CMP_STRIDE], w1[CMP_STRIDE:]], axis=-1).astype(BF16)
    return w.reshape(CMP_PAIRS, 2 * HEAD_DIM, 2 * HEAD_DIM)


def _cmp_prompt_kernel(kc_ref, vc_ref, wk_ref, wv_ref, ok_ref, ov_ref, *, n_ch):
    for src, w, dst in ((kc_ref, wk_ref, ok_ref), (vc_ref, wv_ref, ov_ref)):
        acc = jnp.zeros((n_ch, 2 * HEAD_DIM), F32)
        for jj in range(CMP_PAIRS):
            xj = jnp.concatenate([src[pl.ds(2 * jj + r, n_ch, stride=CMP_STRIDE), :] for r in range(2)], axis=1)
            acc = acc + _dot(xj.astype(BF16), w[jj])
        dst[...] = acc


def _cmp_prompt(kc, vc, wk, wv):
    b, s, _ = kc.shape
    n_ch = s // CMP_STRIDE
    tok = pl.BlockSpec((None, s, HEAD_DIM), lambda i, g: (i, 0, g))
    wsp = pl.BlockSpec((CMP_PAIRS, 2 * HEAD_DIM, 2 * HEAD_DIM), lambda i, g: (0, 0, 0))
    osp = pl.BlockSpec((None, None, n_ch, 2 * HEAD_DIM), lambda i, g: (i, g, 0, 0))
    osh = jax.ShapeDtypeStruct((b, KV_GROUPS, n_ch, 2 * HEAD_DIM), F32)
    return pl.pallas_call(
        functools.partial(_cmp_prompt_kernel, n_ch=n_ch),
        grid=(b, KV_GROUPS),
        in_specs=[tok, tok, wsp, wsp],
        out_specs=[osp, osp],
        out_shape=[osh, osh],
        compiler_params=_cparams(("parallel", "parallel")),
        name="cmp_prompt",
    )(kc, vc, wk, wv)


def _cmp_finish_kernel(*refs, nch, has_tail):
    if has_tail:
        lk, lv, kn, vn, pek, pev, w1k, w1v, w2k, w2v, ock, ocv = refs
    else:
        lk, lv, pek, pev, w1k, w1v, w2k, w2v, ock, ocv = refs
        kn = vn = None
    row = lax.broadcasted_iota(jnp.int32, (nch, 1), 0)
    half = CMP_STRIDE * HEAD_DIM
    for l_ref, n_ref, pe_ref, w1_ref, w2_ref, o_ref in ((lk, kn, pek, w1k, w2k, ock), (lv, vn, pev, w1v, w2v, ocv)):
        lohi = l_ref[...]
        lo = lohi[:, :HEAD_DIM]
        hi = lohi[:, HEAD_DIM:]
        pe = jnp.broadcast_to(pe_ref[0:1, :], (SUBLANES, half)).astype(BF16)
        b_lo = _dot(pe, w1_ref[0:half, :].astype(BF16))[0:1]
        pe = jnp.broadcast_to(pe_ref[1:2, :], (SUBLANES, half)).astype(BF16)
        b_hi = _dot(pe, w1_ref[half:2 * half, :].astype(BF16))[0:1]
        hi_next = pltpu.roll(hi, nch - 1, 0)
        if has_tail:
            new = jnp.broadcast_to(n_ref[...], (SUBLANES, HEAD_DIM)).astype(BF16)
            tail = _dot(new, w1_ref[half:half + HEAD_DIM, :].astype(BF16))[0:1]
            hi_next = jnp.where(row == nch - 1, tail, hi_next)
        hid = _gelu(lo + hi_next + (b_lo + b_hi))
        o_ref[...] = _dot(hid.astype(BF16), w2_ref[...].astype(BF16)).astype(BF16)


def _cmp_finish(lk, lv, pek, pev, w1k, w1v, w2k, w2v, kn=None, vn=None):
    b, _, nch, _ = lk.shape
    has_tail = kn is not None
    lsp = pl.BlockSpec((None, None, nch, 2 * HEAD_DIM), lambda i, g: (i, g, 0, 0))
    full = lambda a: pl.BlockSpec(a.shape, lambda i, g: (0,) * a.ndim)
    pek2 = pek.reshape(2, CMP_STRIDE * HEAD_DIM)
    pev2 = pev.reshape(2, CMP_STRIDE * HEAD_DIM)
    w1k2 = w1k.reshape(CMP_BLOCK * HEAD_DIM, HEAD_DIM)
    w1v2 = w1v.reshape(CMP_BLOCK * HEAD_DIM, HEAD_DIM)
    args = [lk, lv]
    specs = [lsp, lsp]
    if has_tail:
        nsp = pl.BlockSpec((None, 1, HEAD_DIM), lambda i, g: (i, 0, g))
        args += [kn, vn]
        specs += [nsp, nsp]
    params = [pek2, pev2, w1k2, w1v2, w2k, w2v]
    args += params
    specs += [full(a) for a in params]
    osp = pl.BlockSpec((None, None, nch, HEAD_DIM), lambda i, g: (i, g, 0, 0))
    osh = jax.ShapeDtypeStruct((b, KV_GROUPS, nch, HEAD_DIM), BF16)
    return pl.pallas_call(
        functools.partial(_cmp_finish_kernel, nch=nch, has_tail=has_tail),
        grid=(b, KV_GROUPS),
        in_specs=specs,
        out_specs=[osp, osp],
        out_shape=[osh, osh],
        compiler_params=_cparams(("parallel", "parallel")),
        name="cmp_finish",
    )(*args)


def _imp_matrix(n_cmp_pad, n_blk_pad):
    ratio = SEL_BLOCK // CMP_STRIDE
    c = np.arange(n_cmp_pad)[:, None]
    j = np.arange(n_blk_pad)[None, :]
    return jnp.asarray(((c >= ratio * j - 1) & (c <= ratio * j + ratio - 1)).astype(np.float32), dtype=BF16)


def _expand_matrix(n_blk_pad, n_keys):
    j = np.arange(n_blk_pad)[:, None]
    k = np.arange(n_keys)[None, :]
    return jnp.asarray((k // SEL_BLOCK == j).astype(np.float32), dtype=BF16)


def _group_slopes(g):
    sg = jnp.where(g == 0, 1.0, 2.0 ** -HEADS_PER_GROUP).astype(F32)
    return [jnp.full((Q_BLOCK, 1), 2.0 ** -(h + 1), F32) * sg for h in range(HEADS_PER_GROUP)]


def _nsa_prompt_kernel(q_ref, ck_ref, cv_ref, ks_ref, vs_ref, kw_ref, vw_ref, gt_ref, a_ref, e_ref, o_ref, sk_ref,
                       *, seq, n_cmp, n_blk, n_pick):
    g = pl.program_id(1)
    n = pl.program_id(2)
    rows = HEADS_PER_GROUP * Q_BLOCK
    qb = q_ref[...]
    q = jnp.concatenate([qb[:, h * HEAD_DIM:(h + 1) * HEAD_DIM] for h in range(HEADS_PER_GROUP)], axis=0)
    qpos1 = n * Q_BLOCK + lax.broadcasted_iota(jnp.int32, (Q_BLOCK, 1), 0)
    qpos = jnp.concatenate([qpos1] * HEADS_PER_GROUP, axis=0)
    slope = jnp.concatenate(_group_slopes(g), axis=0)

    ncp = ck_ref.shape[0]
    cidx = lax.broadcasted_iota(jnp.int32, (1, ncp), 1)
    dist = qpos - (cidx * CMP_STRIDE + (CMP_BLOCK - 1))
    mask = (dist >= 0) & (cidx < n_cmp)
    p = _masked_softmax(_dot_nt(q, ck_ref[...]) - slope * dist.astype(F32), mask)
    o_c = _dot(p.astype(BF16), cv_ref[...])
    p_sum = p[0:Q_BLOCK]
    for h in range(1, HEADS_PER_GROUP):
        p_sum = p_sum + p[h * Q_BLOCK:(h + 1) * Q_BLOCK]

    imp = sum(_dot(t, a_ref[...]) for t in _split3(p_sum))
    blk = lax.broadcasted_iota(jnp.int32, (1, LANES), 1)
    cur = qpos1 // SEL_BLOCK
    valid = blk <= cur
    forced = (blk == 0) | (blk == cur) | (blk == cur - 1)
    score = jnp.where(forced & valid, FORCE_SCORE, jnp.where(valid, imp, -FORCE_SCORE))
    score_t = score.T[:n_blk]
    bidx = lax.broadcasted_iota(jnp.int32, (n_blk, 1), 0)
    rank = jnp.zeros((n_blk, Q_BLOCK), F32)
    for j in range(n_blk):
        r = score_t[j:j + 1, :]
        rank = rank + jnp.where((r > score_t) | ((r == score_t) & (j < bidx)), 1.0, 0.0)
    sel_t = jnp.where(rank < n_pick, 1.0, 0.0)
    if n_blk < LANES:
        sel_t = jnp.concatenate([sel_t, jnp.zeros((LANES - n_blk, Q_BLOCK), F32)], axis=0)
    sk_ref[...] = _dot(sel_t.T.astype(BF16), e_ref[...])

    tk = min(SEL_KV_TILE, seq)
    n_tiles = (n * Q_BLOCK + Q_BLOCK + tk - 1) // tk
    col = lax.broadcasted_iota(jnp.int32, (1, HEAD_DIM), 1)
    q_aug = jnp.where(col == 0, slope * float(SEL_BLOCK), jnp.where(col == 1, slope, 0.0)).astype(BF16)
    q2 = jnp.concatenate([q, q_aug], axis=1)
    krow = lax.broadcasted_iota(jnp.int32, (tk, 1), 0)

    def body(t, carry):
        m, l, acc = carry
        k0 = pl.multiple_of(t * tk, tk)
        kpos = k0 + lax.broadcasted_iota(jnp.int32, (1, tk), 1)
        kp_col = k0 + krow
        k_aug = jnp.where(col == 0, kp_col // SEL_BLOCK, jnp.where(col == 1, kp_col % SEL_BLOCK, 0))
        k2 = jnp.concatenate([ks_ref[pl.ds(k0, tk), :], k_aug.astype(F32).astype(BF16)], axis=1)
        picked = sk_ref[:, pl.ds(k0, tk)] > 0.5
        bias = jnp.where((kpos <= qpos1) & picked, 0.0, NEG_INF)
        s = _dot_nt(q2, k2) + jnp.concatenate([bias] * HEADS_PER_GROUP, axis=0)
        m_new = jnp.maximum(m, jnp.max(s, axis=-1, keepdims=True))
        p = jnp.exp(s - m_new)
        alpha = jnp.exp(m - m_new)
        l = alpha * l + jnp.sum(p, axis=-1, keepdims=True)
        acc = alpha * acc + _dot(p.astype(BF16), vs_ref[pl.ds(k0, tk), :])
        return m_new, l, acc

    init = (jnp.full((rows, 1), NEG_INF, F32), jnp.zeros((rows, 1), F32), jnp.zeros((rows, HEAD_DIM), F32))
    _, l, acc = lax.fori_loop(0, n_tiles, body, init)
    o_s = acc / jnp.maximum(l, TINY)

    band = min(WINDOW + Q_BLOCK, seq)
    st = pl.multiple_of(jnp.minimum(jnp.maximum(n * Q_BLOCK - WINDOW, 0), seq - band), Q_BLOCK)
    kpos = st + lax.broadcasted_iota(jnp.int32, (1, band), 1)
    dist = qpos - kpos
    mask = (dist >= 0) & (dist < WINDOW)
    p = _masked_softmax(_dot_nt(q, kw_ref[pl.ds(st, band), :]) - slope * dist.astype(F32), mask)
    o_w = _dot(p.astype(BF16), vw_ref[pl.ds(st, band), :])

    gt = gt_ref[...]
    for h in range(HEADS_PER_GROUP):
        sl = slice(h * Q_BLOCK, (h + 1) * Q_BLOCK)
        o_ref[:, h * HEAD_DIM:(h + 1) * HEAD_DIM] = (
            gt[:, h:h + 1] * o_c[sl]
            + gt[:, HEADS_PER_GROUP + h:HEADS_PER_GROUP + h + 1] * o_s[sl]
            + gt[:, 2 * HEADS_PER_GROUP + h:2 * HEADS_PER_GROUP + h + 1] * o_w[sl])


def _nsa_prompt(qbf, ck, cv, ksb, vsb, kwb, vwb, gates):
    b, s, _ = qbf.shape
    ncp = ck.shape[2]
    n_cmp = s // CMP_STRIDE - 1
    n_blk = s // SEL_BLOCK
    assert s % Q_BLOCK == 0 and TOP_N <= n_blk <= LANES
    a = _imp_matrix(ncp, LANES)
    e = _expand_matrix(LANES, s)
    gw = HEADS_PER_GROUP * HEAD_DIM
    qsp = pl.BlockSpec((None, Q_BLOCK, gw), lambda i, g, n: (i, n, g))
    csp = pl.BlockSpec((None, None, ncp, HEAD_DIM), lambda i, g, n: (i, g, 0, 0))
    ksp = pl.BlockSpec((None, s, HEAD_DIM), lambda i, g, n: (i, 0, g))
    gsp = pl.BlockSpec((None, Q_BLOCK, LANES), lambda i, g, n: (i, n, g))
    full = lambda x: pl.BlockSpec(x.shape, lambda i, g, n: (0,) * x.ndim)
    return pl.pallas_call(
        functools.partial(_nsa_prompt_kernel, seq=s, n_cmp=n_cmp, n_blk=n_blk, n_pick=TOP_N),
        grid=(b, KV_GROUPS, s // Q_BLOCK),
        in_specs=[qsp, csp, csp, ksp, ksp, ksp, ksp, gsp, full(a), full(e)],
        out_specs=qsp,
        out_shape=jax.ShapeDtypeStruct((b, s, NSA_WIDTH), F32),
        scratch_shapes=[pltpu.VMEM((Q_BLOCK, s), F32)],
        compiler_params=_cparams(("parallel", "parallel", "arbitrary")),
        name="nsa_prompt",
    )(qbf, ck, cv, ksb, vsb, kwb, vwb, gates, a, e)


SSM_LANE_GROUPS = LANES // SSM_GROUP_CH
SSM_STATE_TILE = SSM_LANE_GROUPS * SSM_STATE


def _prep_s5(a_re, a_im, log_dt, b_re, b_im, c_re, c_im):
    ng = a_re.shape[0]
    nk = ng // SSM_LANE_GROUPS
    dt = jnp.exp(log_dt)[:, None]
    mag = jnp.exp(a_re * dt)
    lb_re = mag * jnp.cos(a_im * dt)
    lb_im = mag * jnp.sin(a_im * dt)
    den = a_re * a_re + a_im * a_im
    nr = lb_re - 1.0
    f_re = (nr * a_re + lb_im * a_im) / den
    f_im = (lb_im * a_re - nr * a_im) / den
    bb_re = f_re[..., None] * b_re - f_im[..., None] * b_im
    bb_im = f_re[..., None] * b_im + f_im[..., None] * b_re
    eye = jnp.eye(SSM_LANE_GROUPS, dtype=F32)

    def in_map(bb):
        t = bb.reshape(nk, SSM_LANE_GROUPS, SSM_STATE, SSM_GROUP_CH)
        t = jnp.einsum('kgnc,gh->kgchn', t, eye)
        return t.reshape(nk, LANES, SSM_STATE_TILE)

    def out_map(c):
        t = c.reshape(nk, SSM_LANE_GROUPS, SSM_GROUP_CH, SSM_STATE)
        t = jnp.einsum('kgcn,gh->kgnhc', t, eye)
        return t.reshape(nk, SSM_STATE_TILE, LANES)

    bb = jnp.concatenate([in_map(bb_re), in_map(bb_im)], axis=-1)
    cm = jnp.concatenate([out_map(c_re), -out_map(c_im)], axis=1)
    bb_hi = bb.astype(BF16)
    bb_lo = (bb - bb_hi.astype(F32)).astype(BF16)
    lam = jnp.stack([lb_re.reshape(-1), lb_im.reshape(-1)])
    return lam, bb_hi, bb_lo, cm.astype(BF16)


def _s5_kernel(u_ref, h0r_ref, h0i_ref, lam_ref, bbh_ref, bbl_ref, cm_ref, d_ref, wglu_ref, bglu_ref,
               o_ref, hr_ref, hi_ref, *scratch, rows, scan, precise):
    nk = bbh_ref.shape[0]
    st = SSM_STATE_TILE
    if scan:
        cr_scr, ci_scr = scratch
        t = pl.program_id(1)

        @pl.when(t == 0)
        def _():
            cr_scr[...] = h0r_ref[...]
            ci_scr[...] = h0i_ref[...]

        row8 = lax.broadcasted_iota(jnp.int32, (rows, 1), 0) % SUBLANES
    u = u_ref[...]
    ys = []
    for k in range(nk):
        uk = u[:, k * LANES:(k + 1) * LANES]
        if precise:
            uh, ul = _split2(uk)
            bu = _dot(uh, bbh_ref[k]) + _dot(ul, bbh_ref[k]) + _dot(uh, bbl_ref[k])
        else:
            bu = _dot(uk.astype(BF16), bbh_ref[k])
        br = bu[:, :st]
        bi = bu[:, st:]
        ks = slice(k * st, (k + 1) * st)
        lr = lam_ref[0:1, ks]
        li = lam_ref[1:2, ks]
        if scan:
            pr, pi = lr, li
            sh = 1
            while sh < min(SUBLANES, rows):
                sr = jnp.where(row8 >= sh, pltpu.roll(br, sh, 0), 0.0)
                si = jnp.where(row8 >= sh, pltpu.roll(bi, sh, 0), 0.0)
                br, bi = br + (pr * sr - pi * si), bi + (pr * si + pi * sr)
                pr, pi = pr * pr - pi * pi, 2.0 * (pr * pi)
                sh *= 2
            tr, ti = [lr], [li]
            for _ in range(min(SUBLANES, rows) - 1):
                tr, ti = tr + [tr[-1] * lr - ti[-1] * li], ti + [tr[-1] * li + ti[-1] * lr]
            tab_r = jnp.concatenate(tr, axis=0)
            tab_i = jnp.concatenate(ti, axis=0)
            c_r = cr_scr[:, ks]
            c_i = ci_scr[:, ks]
            out_r, out_i = [], []
            tile = min(SUBLANES, rows)
            for i0 in range(0, rows, tile):
                h_r = br[i0:i0 + tile] + (tab_r * c_r - tab_i * c_i)
                h_i = bi[i0:i0 + tile] + (tab_r * c_i + tab_i * c_r)
                out_r.append(h_r)
                out_i.append(h_i)
                c_r, c_i = h_r[tile - 1:tile], h_i[tile - 1:tile]
            br = jnp.concatenate(out_r, axis=0)
            bi = jnp.concatenate(out_i, axis=0)
            cr_scr[:, ks] = c_r
            ci_scr[:, ks] = c_i
        else:
            h0r = h0r_ref[:, ks]
            h0i = h0i_ref[:, ks]
            br, bi = br + (lr * h0r - li * h0i), bi + (lr * h0i + li * h0r)
            hr_ref[:, ks] = br
            hi_ref[:, ks] = bi
        hcat = jnp.concatenate([br, bi], axis=1).astype(BF16)
        ys.append(_dot(hcat, cm_ref[k]) + d_ref[:, k * LANES:(k + 1) * LANES] * uk)
    v = _gelu(jnp.concatenate(ys, axis=1))
    o_ref[...] = v * jax.nn.sigmoid(_dot(v.astype(BF16), wglu_ref[...]) + bglu_ref[...])
    if scan:
        hr_ref[...] = cr_scr[...]
        hi_ref[...] = ci_scr[...]


def _s5(u, h0r, h0i, prep, d, wglu, bglu, *, scan, precise, tc_pref=256):
    lam, bbh, bbl, cm = prep
    ns = lam.shape[1]
    full = lambda a: pl.BlockSpec(a.shape, lambda *idx: (0,) * a.ndim)
    params = [lam, bbh, bbl, cm, d, wglu, bglu]
    if scan:
        b, t, c = u.shape
        tc = _row_tile(t, tc_pref)
        grid = (b, t // tc)
        usp = pl.BlockSpec((None, tc, c), lambda i, j: (i, j, 0))
        hsp = pl.BlockSpec((None, 1, ns), lambda i, j: (i, 0, 0))
        rows = tc
        scratch = [pltpu.VMEM((1, ns), F32), pltpu.VMEM((1, ns), F32)]
        sem = ("parallel", "arbitrary")
        osh = [jax.ShapeDtypeStruct((b, t, c), F32)] + [jax.ShapeDtypeStruct((b, 1, ns), F32)] * 2
    else:
        b, c = u.shape
        grid = (1,)
        usp = pl.BlockSpec((b, c), lambda i: (0, 0))
        hsp = pl.BlockSpec((b, ns), lambda i: (0, 0))
        rows = b
        scratch = []
        sem = ("arbitrary",)
        osh = [jax.ShapeDtypeStruct((b, c), F32)] + [jax.ShapeDtypeStruct((b, ns), F32)] * 2
    return pl.pallas_call(
        functools.partial(_s5_kernel, rows=rows, scan=scan, precise=precise),
        grid=grid,
        in_specs=[usp, hsp, hsp] + [full(a) for a in params],
        out_specs=[usp, hsp, hsp],
        out_shape=osh,
        scratch_shapes=scratch,
        compiler_params=_cparams(sem),
        name="s5_scan" if scan else "s5_step",
    )(u, h0r, h0i, *params)


def _prep_router(rg_w, rg_b, re_w, re_b):
    d = rg_w.shape[0]
    w = jnp.concatenate([rg_w, re_w.transpose(1, 0, 2).reshape(d, N_EXPERTS)], axis=1)
    w = jnp.pad(w, ((0, 0), (0, ROUTER_COLS - w.shape[1])))
    b = jnp.pad(jnp.concatenate([rg_b, re_b.reshape(-1)]), (0, ROUTER_COLS - N_EXPERT_GROUPS - N_EXPERTS))
    hi = w.astype(BF16)
    lo = (w - hi.astype(F32)).astype(BF16)
    return hi, lo, b.reshape(1, ROUTER_COLS)


def _tail1_kernel(on_ref, os_ref, h_ref, gn_ref, gs_ref, wout_ref, gf_ref, wrh_ref, wrl_ref, br_ref,
                  h1_ref, xn_ref, rid_ref, rw_ref):
    cat = jnp.concatenate([_rms(on_ref[...], gn_ref[...]), _rms(os_ref[...], gs_ref[...])], axis=1)
    h1 = h_ref[...] + _dot(cat.astype(BF16), wout_ref[...])
    h1_ref[...] = h1
    xn = _rms(h1, gf_ref[...])
    xh, xl = _split2(xn)
    xn_ref[...] = xn
    lg = _dot(xh, wrh_ref[...]) + _dot(xl, wrh_ref[...]) + _dot(xh, wrl_ref[...]) + br_ref[...]
    lane_i = lax.broadcasted_iota(jnp.int32, (1, ROUTER_COLS), 1)
    lane = lane_i.astype(F32)
    ninf = -jnp.inf
    big = float(ROUTER_COLS)
    is_grp = lane_i < N_EXPERT_GROUPS
    lgm = jnp.where(is_grp, lg, ninf)
    mx = jnp.max(lgm, axis=-1, keepdims=True)
    grp = jnp.min(jnp.where(lgm == mx, lane, big), axis=-1, keepdims=True)
    p_grp = 1.0 / jnp.sum(jnp.where(is_grp, jnp.exp(lgm - mx), 0.0), axis=-1, keepdims=True)
    first = ROUTER_EXP_OFF + EXPERTS_PER_GROUP * grp
    lem = jnp.where((lane >= first) & (lane < first + EXPERTS_PER_GROUP), lg, ninf)
    v1 = jnp.max(lem, axis=-1, keepdims=True)
    i1 = jnp.min(jnp.where(lem == v1, lane, big), axis=-1, keepdims=True)
    lem2 = jnp.where(lane == i1, ninf, lem)
    v2 = jnp.max(lem2, axis=-1, keepdims=True)
    i2 = jnp.min(jnp.where(lem2 == v2, lane, big), axis=-1, keepdims=True)
    e2 = jnp.exp(v2 - v1)
    w1 = (1.0 / (1.0 + e2)) * p_grp
    w2 = (e2 / (1.0 + e2)) * p_grp
    rid_ref[...] = jnp.where(lane_i == 0, i1 - ROUTER_EXP_OFF,
                             jnp.where(lane_i == 1, i2 - ROUTER_EXP_OFF, 0.0)).astype(jnp.int32)
    rw_ref[...] = jnp.where(lane_i == 0, w1, jnp.where(lane_i == 1, w2, 0.0))


def _tail1(o_nsa, o_ssm, h, g_nsa, g_ssm, wout, g_ffn, router, tm_pref=256):
    m, d = h.shape
    tm = _row_tile(m, tm_pref)
    wrh, wrl, br = router
    row = lambda c: pl.BlockSpec((tm, c), lambda i: (i, 0))
    full = lambda a: pl.BlockSpec(a.shape, lambda i: (0,) * a.ndim)
    params = [g_nsa, g_ssm, wout, g_ffn, wrh, wrl, br]
    return pl.pallas_call(
        _tail1_kernel,
        grid=(m // tm,),
        in_specs=[row(o_nsa.shape[1]), row(o_ssm.shape[1]), row(d)] + [full(a) for a in params],
        out_specs=[row(d), row(d), row(ROUTER_COLS), row(ROUTER_COLS)],
        out_shape=[jax.ShapeDtypeStruct((m, d), F32), jax.ShapeDtypeStruct((m, d), F32),
                   jax.ShapeDtypeStruct((m, ROUTER_COLS), jnp.int32), jax.ShapeDtypeStruct((m, ROUTER_COLS), F32)],
        compiler_params=_cparams(("parallel",)),
        name="tail1",
    )(o_nsa, o_ssm, h, *params)


MOE_TILE = 256
MOE_GATHER_DEPTH = 3


def _moe_dispatch(rid):
    t = rid.shape[0]
    pairs = 2 * t
    e = rid.reshape(-1)
    onehot = (e[:, None] == jnp.arange(N_EXPERTS, dtype=jnp.int32)[None, :]).astype(jnp.int32)
    csum = jnp.cumsum(onehot, axis=0)
    rank = jnp.take_along_axis(csum, e[:, None], axis=1)[:, 0] - 1
    tiles = (csum[-1] + MOE_TILE - 1) // MOE_TILE
    tile_end = jnp.cumsum(tiles)
    pos = (tile_end - tiles)[e] * MOE_TILE + rank
    nt_max = -(-pairs // MOE_TILE) + N_EXPERTS
    n_tiles = tile_end[-1]
    tile_ids = jnp.arange(nt_max, dtype=jnp.int32)
    tile_expert = jnp.minimum(jnp.sum((tile_end[None, :] <= tile_ids[:, None]).astype(jnp.int32), axis=1), N_EXPERTS - 1)
    tile_expert = jnp.where(tile_ids < n_tiles, tile_expert, tile_expert[n_tiles - 1])
    row_token = jnp.zeros((nt_max * MOE_TILE,), jnp.int32).at[pos].set(jnp.arange(pairs, dtype=jnp.int32) // 2)
    return pos.astype(jnp.int32), tile_expert, row_token, n_tiles.astype(jnp.int32).reshape(1)


def _moe_sparse_kernel(te_ref, rt_ref, nt_ref, x_hbm, w1_ref, w3_ref, w2_ref, y_ref, xbuf, w1b, w3b, w2b, sem):
    t = pl.program_id(0)
    n = nt_ref[0]

    def row_copy(tile, slot, r):
        tok = rt_ref[tile * MOE_TILE + r]
        return pltpu.make_async_copy(x_hbm.at[pl.ds(tok, 1), :], xbuf.at[slot, pl.ds(r, 1), :], sem.at[slot])

    def start_tile(tile, slot):
        def body(r, c):
            row_copy(tile, slot, r).start()
            return c
        lax.fori_loop(0, MOE_TILE, body, 0, unroll=8)

    def wait_tile(tile, slot):
        def body(r, c):
            row_copy(tile, slot, r).wait()
            return c
        lax.fori_loop(0, MOE_TILE, body, 0, unroll=8)

    @pl.when(t == 0)
    def _():
        for t0 in range(MOE_GATHER_DEPTH - 1):
            @pl.when(t0 < n)
            def _(t0=t0):
                start_tile(t0, t0)

    ahead = t + (MOE_GATHER_DEPTH - 1)

    @pl.when(ahead < n)
    def _():
        start_tile(ahead, ahead % MOE_GATHER_DEPTH)

    @pl.when(t < n)
    def _():
        slot = t % MOE_GATHER_DEPTH
        wait_tile(t, slot)

        @pl.when((t == 0) | (te_ref[t] != te_ref[jnp.maximum(t - 1, 0)]))
        def _():
            w1b[...] = w1_ref[...].astype(BF16)
            w3b[...] = w3_ref[...].astype(BF16)
            w2b[...] = w2_ref[...].astype(BF16)

        x = xbuf[slot].astype(BF16)
        a = _dot(x, w1b[...])
        hid = (a * jax.nn.sigmoid(a)) * _dot(x, w3b[...])
        y_ref[...] = _dot(hid.astype(BF16), w2b[...])

    @pl.when(t >= n)
    def _():
        y_ref[...] = jnp.zeros_like(y_ref)


def _moe_sparse(xn, tile_expert, row_token, n_tiles, w1, w3, w2):
    _, d = xn.shape
    _, _, hdim = w1.shape
    nt_max = tile_expert.shape[0]
    wsp = lambda r, c: pl.BlockSpec((None, r, c), lambda t, te, rt, nt: (te[t], 0, 0))
    return pl.pallas_call(
        _moe_sparse_kernel,
        grid_spec=pltpu.PrefetchScalarGridSpec(
            num_scalar_prefetch=3,
            grid=(nt_max,),
            in_specs=[pl.BlockSpec(memory_space=pl.ANY), wsp(d, hdim), wsp(d, hdim), wsp(hdim, d)],
            out_specs=pl.BlockSpec((MOE_TILE, d), lambda t, te, rt, nt: (t, 0)),
            scratch_shapes=[pltpu.VMEM((MOE_GATHER_DEPTH, MOE_TILE, d), F32), pltpu.VMEM((d, hdim), BF16),
                            pltpu.VMEM((d, hdim), BF16), pltpu.VMEM((hdim, d), BF16),
                            pltpu.SemaphoreType.DMA((MOE_GATHER_DEPTH,))]),
        out_shape=jax.ShapeDtypeStruct((nt_max * MOE_TILE, d), F32),
        compiler_params=_cparams(("arbitrary",)),
        name="moe_sparse",
    )(tile_expert, row_token, n_tiles, xn, w1, w3, w2)


def _tail2_kernel(pos_ref, h1_ref, rw_ref, y_hbm, p_ref, gp_ref, wg_ref, bg_ref, wp_ref, gfin_ref, o_ref, ybuf, sem,
                  *, tm, tok_off):
    base = (tok_off + pl.program_id(0) * tm) * 2

    def row_copy(r, k):
        return pltpu.make_async_copy(y_hbm.at[pl.ds(pos_ref[base + 2 * r + k], 1), :],
                                     ybuf.at[k, pl.ds(r, 1), :], sem.at[0])

    def each_row(fn):
        def body(r, c):
            fn(row_copy(r, 0))
            fn(row_copy(r, 1))
            return c
        lax.fori_loop(0, tm, body, 0, unroll=8)

    each_row(lambda cp: cp.start())
    emb = _dot(p_ref[...].astype(BF16), wp_ref[...])
    each_row(lambda cp: cp.wait())
    rw = rw_ref[...]
    h = h1_ref[...] + (rw[:, 0:1] * ybuf[0] + rw[:, 1:2] * ybuf[1])
    gate = jax.nn.sigmoid(_dot(_rms(h, gp_ref[...]).astype(BF16), wg_ref[...]) + bg_ref[...])
    h = h + gate * emb
    o_ref[...] = _rms(h, gfin_ref[...])


def _tail2(pos, tok_off, h1, rw, y, p, g_ple, wg, bg, wp, g_fin, tm_pref=256):
    m, d = h1.shape
    tm = _row_tile(m, tm_pref)
    row = lambda c: pl.BlockSpec((tm, c), lambda i, ps: (i, 0))
    full = lambda a: pl.BlockSpec(a.shape, lambda i, ps: (0,) * a.ndim)
    params = [g_ple, wg, bg, wp, g_fin]
    return pl.pallas_call(
        functools.partial(_tail2_kernel, tm=tm, tok_off=tok_off),
        grid_spec=pltpu.PrefetchScalarGridSpec(
            num_scalar_prefetch=1,
            grid=(m // tm,),
            in_specs=[row(d), row(ROUTER_COLS), pl.BlockSpec(memory_space=pl.ANY), row(p.shape[1])]
            + [full(a) for a in params],
            out_specs=row(d),
            scratch_shapes=[pltpu.VMEM((2, tm, d), F32), pltpu.SemaphoreType.DMA((1,))]),
        out_shape=jax.ShapeDtypeStruct((m, d), F32),
        compiler_params=_cparams(("arbitrary",)),
        name="tail2",
    )(pos, h1, rw, y, p, *params)


PAGES_PER_STEP = 16
CMP_PAGES_PER_STEP = 32
PAGE_ROWS_PER_TOKEN = KV_GROUPS


def _cmp_paged_kernel(pt_ref, *refs, pages):
    kp = refs[:pages]
    vp = refs[pages:2 * pages]
    wk_ref, wv_ref, ok_ref, ov_ref = refs[2 * pages:]
    chunks = kp[0].shape[0] // (PAGE_ROWS_PER_TOKEN * CMP_STRIDE)
    stride = PAGE_ROWS_PER_TOKEN * CMP_STRIDE
    for src, w, dst in ((kp, wk_ref, ok_ref), (vp, wv_ref, ov_ref)):
        for g in range(KV_GROUPS):
            acc = jnp.zeros((pages * chunks, 2 * HEAD_DIM), F32)
            for jj in range(CMP_PAIRS):
                xj = jnp.concatenate(
                    [jnp.concatenate([p[pl.ds(PAGE_ROWS_PER_TOKEN * (2 * jj + r) + g, chunks, stride=stride), :]
                                      for p in src], axis=0) for r in range(2)], axis=1)
                acc = acc + _dot(xj.astype(BF16), w[jj])
            dst[g] = acc


def _page_specs(page_rows, pages):
    return [pl.BlockSpec((page_rows, HEAD_DIM), functools.partial(lambda i, b, s, pt: (pt[b, s * pages + i], 0), i))
            for i in range(pages)]


def _cmp_paged(pool_k, pool_v, page_table, wk, wv, page_size):
    b, n_pages = page_table.shape
    pages = CMP_PAGES_PER_STEP if n_pages % CMP_PAGES_PER_STEP == 0 else n_pages
    page_rows = page_size * PAGE_ROWS_PER_TOKEN
    chunks = page_size // CMP_STRIDE
    wsp = pl.BlockSpec((CMP_PAIRS, 2 * HEAD_DIM, 2 * HEAD_DIM), lambda i, s, pt: (0, 0, 0))
    osp = pl.BlockSpec((None, KV_GROUPS, pages * chunks, 2 * HEAD_DIM), lambda i, s, pt: (i, 0, s, 0))
    osh = jax.ShapeDtypeStruct((b, KV_GROUPS, n_pages * chunks, 2 * HEAD_DIM), F32)
    return pl.pallas_call(
        functools.partial(_cmp_paged_kernel, pages=pages),
        grid_spec=pltpu.PrefetchScalarGridSpec(
            num_scalar_prefetch=1,
            grid=(b, n_pages // pages),
            in_specs=_page_specs(page_rows, pages) * 2 + [wsp, wsp],
            out_specs=[osp, osp]),
        out_shape=[osh, osh],
        compiler_params=_cparams(("parallel", "arbitrary")),
        name="cmp_paged",
    )(page_table, *([pool_k] * pages), *([pool_v] * pages), wk, wv)


def _nsa_sample_kernel(pt_ref, q_ref, ck_ref, cv_ref, a_ref, ksn_ref, vsn_ref, wk_ref, wv_ref, kwn_ref, vwn_ref,
                       gt_ref, *rest, pages, page_size, n_blk, n_cmp, n_pick, past_len):
    kp = rest[:pages]
    vp = rest[pages:2 * pages]
    o_ref, wko_ref, wvo_ref, sel_scr, oc_scr, m_scr, l_scr, acc_scr, q_scr, sel_smem, sel_sem = rest[2 * pages:]
    step = pl.program_id(1)
    last = pl.num_programs(1) - 1
    qpos = past_len
    nbp = sel_scr.shape[1]
    rowi = lax.broadcasted_iota(jnp.int32, (SUBLANES, 1), 0)
    lane = lax.broadcasted_iota(jnp.int32, (1, LANES), 1)
    blk = lax.broadcasted_iota(jnp.int32, (1, nbp), 1)

    def slopes(g):
        sl = jnp.zeros((SUBLANES, 1), F32)
        for h in range(HEADS_PER_GROUP):
            sl = jnp.where(rowi == h, 2.0 ** -(HEADS_PER_GROUP * g + h + 1), sl)
        return sl

    @pl.when(step == 0)
    def _():
        qrow = q_ref[...]
        ncp = ck_ref.shape[1]
        cidx = lax.broadcasted_iota(jnp.int32, (1, ncp), 1)
        dist = jnp.broadcast_to(qpos - (cidx * CMP_STRIDE + (CMP_BLOCK - 1)), (SUBLANES, ncp))
        mask = (dist >= 0) & (cidx < n_cmp)
        cur = qpos // SEL_BLOCK
        valid = blk <= cur
        forced = (blk == 0) | (blk == cur) | (blk == cur - 1)
        for g in range(KV_GROUPS):
            heads = [qrow[:, (HEADS_PER_GROUP * g + h) * HEAD_DIM:(HEADS_PER_GROUP * g + h + 1) * HEAD_DIM]
                     for h in range(HEADS_PER_GROUP)]
            q8 = jnp.concatenate(heads + [jnp.zeros((SUBLANES - HEADS_PER_GROUP, HEAD_DIM), BF16)], axis=0)
            q_scr[g] = q8
            p = _masked_softmax(_dot_nt(q8, ck_ref[g]) - slopes(g) * dist.astype(F32), mask)
            oc_scr[g] = _dot(p.astype(BF16), cv_ref[g])
            p_sum = p[0:1]
            for h in range(1, HEADS_PER_GROUP):
                p_sum = p_sum + p[h:h + 1]
            p_sum = jnp.broadcast_to(p_sum, (SUBLANES, ncp))
            imp = sum(_dot(t, a_ref[...]) for t in _split3(p_sum))
            score = jnp.where(forced & valid, FORCE_SCORE, jnp.where(valid, imp, -FORCE_SCORE))
            score = jnp.where(blk < n_blk, score, -2.0 * FORCE_SCORE)

            cnt = jnp.zeros((SUBLANES, nbp), F32)
            for c in range(-(-n_blk // SUBLANES)):
                j = SUBLANES * c + rowi
                sj = jnp.max(jnp.where(blk == j, score, -jnp.inf), axis=-1, keepdims=True)
                before = ((sj > score) | ((sj == score) & (j < blk))) & (j < n_blk)
                cnt = cnt + jnp.where(before, 1.0, 0.0)
            rank = jnp.broadcast_to(jnp.sum(cnt, axis=0, keepdims=True), (SUBLANES, nbp))
            sel_scr[g * SUBLANES:(g + 1) * SUBLANES, :] = jnp.where((rank < n_pick) & (blk < n_blk), 1, 0)
            m_scr[g] = jnp.full((SUBLANES, LANES), NEG_INF, F32)
            l_scr[g] = jnp.zeros((SUBLANES, LANES), F32)
            acc_scr[g] = jnp.zeros((SUBLANES, HEAD_DIM), F32)
        flags = pltpu.make_async_copy(sel_scr, sel_smem, sel_sem.at[0])
        flags.start()
        flags.wait()

    blocks_per_page = page_size // SEL_BLOCK
    for g in range(KV_GROUPS):
        for i in range(pages):
            pg = step * pages + i
            first_blk = blocks_per_page * pg
            flags = [sel_smem[g * SUBLANES, first_blk + r] for r in range(blocks_per_page)]

            @pl.when(sum(flags) > 0)
            def _(g=g, i=i, pg=pg, flags=flags):
                k = kp[i][pl.ds(g, page_size, stride=PAGE_ROWS_PER_TOKEN), :].astype(BF16)
                v = vp[i][pl.ds(g, page_size, stride=PAGE_ROWS_PER_TOKEN), :].astype(BF16)
                dist = qpos - (pg * page_size + lane)
                picked = jnp.zeros((1, LANES), jnp.int32)
                for r in range(blocks_per_page):
                    picked = jnp.where(lane // SEL_BLOCK == r, flags[r], picked)
                mask = jnp.broadcast_to((dist >= 0) & (picked > 0), (SUBLANES, LANES))
                s = jnp.where(mask, _dot_nt(q_scr[g], k) - slopes(g) * dist.astype(F32), NEG_INF)
                m_old = m_scr[g][:, 0:1]
                m_new = jnp.maximum(m_old, jnp.max(s, axis=-1, keepdims=True))
                p = jnp.where(mask, jnp.exp(s - m_new), 0.0)
                alpha = jnp.exp(m_old - m_new)
                acc_scr[g] = alpha * acc_scr[g] + _dot(p.astype(BF16), v)
                l_scr[g] = jnp.broadcast_to(alpha * l_scr[g][:, 0:1] + jnp.sum(p, axis=-1, keepdims=True),
                                            (SUBLANES, LANES))
                m_scr[g] = jnp.broadcast_to(m_new, (SUBLANES, LANES))

    @pl.when(step == last)
    def _():
        wb = wk_ref.shape[0]
        wrow = lax.broadcasted_iota(jnp.int32, (wb, 1), 0)
        wk_new = jnp.where(wrow == wb - 1, kwn_ref[...], pltpu.roll(wk_ref[...], wb - 1, 0))
        wv_new = jnp.where(wrow == wb - 1, vwn_ref[...], pltpu.roll(wv_ref[...], wb - 1, 0))
        wko_ref[...] = wk_new
        wvo_ref[...] = wv_new
        widx = lax.broadcasted_iota(jnp.int32, (1, wb), 1)
        wdist = jnp.broadcast_to(wb - 1 - widx, (SUBLANES, wb))
        wmask = (wdist >= 0) & (wdist < WINDOW) & (qpos - wdist >= 0)
        for g in range(KV_GROUPS):
            gs = slice(g * HEAD_DIM, (g + 1) * HEAD_DIM)
            q8 = q_scr[g]
            sl = slopes(g)
            kn = ksn_ref[:, gs].astype(BF16).astype(F32)
            vn = vsn_ref[:, gs].astype(BF16).astype(F32)
            s_n = jnp.sum(q8.astype(F32) * kn, axis=-1, keepdims=True)
            sel_n = sel_smem[g * SUBLANES, n_blk - 1] > 0
            m_old = m_scr[g][:, 0:1]
            m_new = jnp.maximum(m_old, jnp.where(sel_n, s_n, NEG_INF))
            p_n = jnp.where(sel_n, jnp.exp(s_n - m_new), 0.0)
            alpha = jnp.exp(m_old - m_new)
            acc = alpha * acc_scr[g] + p_n.astype(BF16).astype(F32) * vn
            l = alpha * l_scr[g][:, 0:1] + p_n
            o_s = acc / jnp.maximum(l, TINY)
            p = _masked_softmax(_dot_nt(q8, wk_new[:, gs].astype(BF16)) - sl * wdist.astype(F32), wmask)
            o_w = _dot(p.astype(BF16), wv_new[:, gs].astype(BF16))
            gt = jnp.broadcast_to(gt_ref[:, g * LANES:(g + 1) * LANES], (SUBLANES, LANES))
            gcol = [jnp.sum(jnp.where(lane == k * HEADS_PER_GROUP + rowi, gt, 0.0), axis=-1, keepdims=True)
                    for k in range(3)]
            o = gcol[0] * oc_scr[g] + gcol[1] * o_s + gcol[2] * o_w
            for h in range(HEADS_PER_GROUP):
                hd = HEADS_PER_GROUP * g + h
                o_ref[:, hd * HEAD_DIM:(hd + 1) * HEAD_DIM] = o[h:h + 1]


def _nsa_sample(qbf, ck, cv, ks_new, vs_new, win_k, win_v, kw_new, vw_new, gates, pool_k, pool_v, page_table,
                page_size):
    b, n_pages = page_table.shape
    past_len = n_pages * page_size
    ncp = ck.shape[2]
    n_cmp = (past_len + 1 + CMP_STRIDE - 1) // CMP_STRIDE - 1
    n_blk = (past_len + 1 + SEL_BLOCK - 1) // SEL_BLOCK
    nbp = -(-n_blk // LANES) * LANES
    wb = win_k.shape[1]
    assert wb == WINDOW and n_blk >= TOP_N and ncp == n_cmp and page_size % SEL_BLOCK == 0
    pages = PAGES_PER_STEP if n_pages % PAGES_PER_STEP == 0 else n_pages
    page_rows = page_size * PAGE_ROWS_PER_TOKEN
    a = _imp_matrix(ncp, nbp)
    tok = lambda c: pl.BlockSpec((None, 1, c), lambda i, s, pt: (i, 0, 0))
    csp = pl.BlockSpec((None, KV_GROUPS, ncp, HEAD_DIM), lambda i, s, pt: (i, 0, 0, 0))
    wsp = pl.BlockSpec((None, wb, KV_WIDTH), lambda i, s, pt: (i, 0, 0))
    asp = pl.BlockSpec(a.shape, lambda i, s, pt: (0, 0))
    grp_scr = lambda c, dt: pltpu.VMEM((KV_GROUPS, SUBLANES, c), dt)
    return pl.pallas_call(
        functools.partial(_nsa_sample_kernel, pages=pages, page_size=page_size, n_blk=n_blk, n_cmp=n_cmp,
                          n_pick=TOP_N, past_len=past_len),
        grid_spec=pltpu.PrefetchScalarGridSpec(
            num_scalar_prefetch=1,
            grid=(b, n_pages // pages),
            in_specs=[tok(NSA_WIDTH), csp, csp, asp, tok(KV_WIDTH), tok(KV_WIDTH), wsp, wsp, tok(KV_WIDTH),
                      tok(KV_WIDTH), tok(GATE_COLS)] + _page_specs(page_rows, pages) * 2,
            out_specs=[tok(NSA_WIDTH), wsp, wsp],
            scratch_shapes=[pltpu.VMEM((KV_GROUPS * SUBLANES, nbp), jnp.int32), grp_scr(HEAD_DIM, F32),
                            grp_scr(LANES, F32), grp_scr(LANES, F32), grp_scr(HEAD_DIM, F32), grp_scr(HEAD_DIM, BF16),
                            pltpu.SMEM((KV_GROUPS * SUBLANES, nbp), jnp.int32), pltpu.SemaphoreType.DMA((1,))]),
        out_shape=[jax.ShapeDtypeStruct((b, 1, NSA_WIDTH), F32), jax.ShapeDtypeStruct((b, wb, KV_WIDTH), F32),
                   jax.ShapeDtypeStruct((b, wb, KV_WIDTH), F32)],
        compiler_params=_cparams(("parallel", "arbitrary")),
        name="nsa_sample",
    )(page_table, qbf, ck, cv, a, ks_new, vs_new, win_k, win_v, kw_new, vw_new, gates,
      *([pool_k] * pages), *([pool_v] * pages))


def kernel(x_prompt, x_sample, p_prompt, p_sample, cache_cmp_k, cache_cmp_v, cache_sel_k, cache_sel_v, cache_win_k, cache_win_v, state_ssm_re, state_ssm_im, page_table, norm_mix, w_in, cmp_pe_k, cmp_w1_k, cmp_w2_k, cmp_pe_v, cmp_w1_v, cmp_w2_v, ssm_a_re, ssm_a_im, ssm_log_dt, ssm_b_re, ssm_b_im, ssm_c_re, ssm_c_im, ssm_d, w_glu, b_glu, norm_nsa_out, norm_ssm_out, w_out, norm_ffn, router_grp_w, router_grp_b, router_exp_w, router_exp_b, exp_w1, exp_w3, exp_w2, norm_ple, w_ple_gate, b_ple_gate, w_ple_proj, norm_final):
    depth = w_in.shape[0]
    assert depth == 1, "one layer per step"
    i = 0
    b, s, d = x_prompt.shape
    db, t, _ = x_sample.shape
    assert t == 1, "the sample group decodes one token per sequence"
    ssm_ch = w_glu.shape[1]
    page_size = cache_cmp_k.shape[2]
    n_phys = cache_cmp_k.shape[1]
    wp = min(WINDOW, s)
    row1 = lambda a: a.reshape(1, -1)

    w_inp = _prep_w_in(w_in[i], ssm_ch)
    wk1 = _prep_cmp_w1(cmp_w1_k[i])
    wv1 = _prep_cmp_w1(cmp_w1_v[i])
    s5p = _prep_s5(ssm_a_re[i], ssm_a_im[i], ssm_log_dt[i], ssm_b_re[i], ssm_b_im[i], ssm_c_re[i], ssm_c_im[i])
    router = _prep_router(router_grp_w[i], router_grp_b[i], router_exp_w[i], router_exp_b[i])
    wglu = w_glu[i].astype(BF16)
    wout = w_out[i].astype(BF16)
    wpg = w_ple_gate[i].astype(BF16)
    wpp = w_ple_proj[i].astype(BF16)
    cmp_par = (cmp_pe_k[i], cmp_pe_v[i], cmp_w1_k[i], cmp_w1_v[i], cmp_w2_k[i], cmp_w2_v[i])

    def tail1(h, o_nsa, o_ssm):
        return _tail1(o_nsa, o_ssm, h, row1(norm_nsa_out[i]), row1(norm_ssm_out[i]), wout, row1(norm_ffn[i]), router)

    def tail2(pos, tok_off, h1, rw, y, p):
        return _tail2(pos, tok_off, h1, rw, y, p, row1(norm_ple[i]), wpg, row1(b_ple_gate[i]), wpp, row1(norm_final))

    m = b * s
    hp = x_prompt.reshape(m, d)
    qbf, kc, vc, ks, vs, kw, vw, ksb, vsb, kwb, vwb, u, gates = _inproj(hp, row1(norm_mix[i]), w_inp, ssm_ch, 512)
    seq = lambda a: a.reshape(b, s, a.shape[-1])
    lk, lv = _cmp_prompt(seq(kc), seq(vc), wk1, wv1)
    ck, cv = _cmp_finish(lk, lv, *cmp_par)
    o_nsa = _nsa_prompt(seq(qbf), ck, cv, seq(ksb), seq(vsb), seq(kwb), seq(vwb), seq(gates))
    ns = s5p[0].shape[1]
    zero = jnp.zeros((b, 1, ns), F32)
    o_ssm, hre_p, him_p = _s5(seq(u), zero, zero, s5p, row1(ssm_d[i]), wglu, row1(b_glu[i]), scan=True, precise=False)
    h1_p, xn_p, rid_p, rw_p = tail1(hp, o_nsa.reshape(m, -1), o_ssm.reshape(m, -1))
    kv5 = lambda a: a.reshape(1, b, s, KV_GROUPS, HEAD_DIM)
    win5 = lambda a: a.reshape(b, s, KV_GROUPS, HEAD_DIM)[None, :, s - wp:]
    st4 = lambda a, n: a.reshape(1, n, ns // SSM_STATE, SSM_STATE)
    prompt_state = (kv5(kc), kv5(vc), kv5(ks), kv5(vs), win5(kw), win5(vw), st4(hre_p, b), st4(him_p, b))

    hs = x_sample.reshape(db, d)
    qbf, kc, vc, ks, vs, kw, vw, _, _, _, _, u, gates = _inproj(hs, row1(norm_mix[i]), w_inp, ssm_ch, 512)
    pool = lambda c: c[i].reshape(n_phys * page_size * KV_GROUPS, HEAD_DIM)
    one = lambda a: a.reshape(db, 1, a.shape[-1])
    lk, lv = _cmp_paged(pool(cache_cmp_k), pool(cache_cmp_v), page_table, wk1, wv1, page_size)
    ck, cv = _cmp_finish(lk, lv, *cmp_par, kn=one(kc), vn=one(vc))
    wb = cache_win_k.shape[2]
    o_nsa, win_k, win_v = _nsa_sample(one(qbf), ck, cv, one(ks), one(vs), cache_win_k[i].reshape(db, wb, KV_WIDTH),
                                      cache_win_v[i].reshape(db, wb, KV_WIDTH), one(kw), one(vw), one(gates),
                                      pool(cache_sel_k), pool(cache_sel_v), page_table, page_size)
    o_ssm, hre_s, him_s = _s5(u, state_ssm_re[i].reshape(db, ns), state_ssm_im[i].reshape(db, ns), s5p,
                              row1(ssm_d[i]), wglu, row1(b_glu[i]), scan=False, precise=True)
    h1_s, xn_s, rid_s, rw_s = tail1(hs, o_nsa.reshape(db, -1), o_ssm)

    rid = jnp.concatenate([rid_p[:, :2], rid_s[:, :2]], axis=0)
    pos, tile_expert, row_token, n_tiles = _moe_dispatch(rid)
    y = _moe_sparse(jnp.concatenate([xn_p, xn_s], axis=0), tile_expert, row_token, n_tiles,
                    exp_w1[i], exp_w3[i], exp_w2[i])
    y_prompt = tail2(pos, 0, h1_p, rw_p, y, p_prompt[i].reshape(m, -1)).reshape(b, s, d)
    y_sample = tail2(pos, m, h1_s, rw_s, y, p_sample[i].reshape(db, -1)).reshape(db, 1, d)
    new5 = lambda a: a.reshape(1, db, 1, KV_GROUPS, HEAD_DIM)
    buf5 = lambda a: a.reshape(1, db, wb, KV_GROUPS, HEAD_DIM)
    sample_state = (new5(kc), new5(vc), new5(ks), new5(vs), buf5(win_k), buf5(win_v), st4(hre_s, db), st4(him_s, db))
    return (y_prompt, y_sample) + prompt_state + sample_state
```

```python
import functools

import jax
import jax.numpy as jnp
import numpy as np
from jax import lax
from jax.experimental import pallas as pl
from jax.experimental.pallas import tpu as pltpu

F32 = jnp.float32
BF16 = jnp.bfloat16

HEAD_DIM = 128
N_HEADS = 8
KV_GROUPS = 2
HEADS_PER_GROUP = N_HEADS // KV_GROUPS
NSA_WIDTH = N_HEADS * HEAD_DIM
KV_WIDTH = KV_GROUPS * HEAD_DIM
CMP_STRIDE = 16
CMP_BLOCK = 2 * CMP_STRIDE
SEL_BLOCK = 64
TOP_N = 16
WINDOW = 512
Q_BLOCK = 128
SSM_GROUP_CH = 16
SSM_STATE = 64
N_EXPERT_GROUPS = 4
EXPERTS_PER_GROUP = 8
N_EXPERTS = N_EXPERT_GROUPS * EXPERTS_PER_GROUP
RMS_EPS = 1e-6
NEG_INF = -1e30
FORCE_SCORE = 1e4
TINY = 1e-30

LANES = 128
SUBLANES = 8
GATE_COLS = KV_GROUPS * LANES
ROUTER_COLS = LANES
ROUTER_EXP_OFF = N_EXPERT_GROUPS
SEL_KV_TILE = 512
VMEM_LIMIT_BYTES = 56 * 1024 * 1024


def _cparams(sem):
    return pltpu.CompilerParams(dimension_semantics=sem, vmem_limit_bytes=VMEM_LIMIT_BYTES)


def _rms(x, g):
    return x * lax.rsqrt(jnp.mean(x * x, axis=-1, keepdims=True) + RMS_EPS) * g


def _gelu(x):
    return x * (0.5 * (1.0 + jnp.tanh(0.7978845608028654 * (x + 0.044715 * (x * x * x)))))


def _dot(a, b):
    return jnp.dot(a, b, preferred_element_type=F32)


def _dot_nt(a, b):
    return lax.dot_general(a, b, (((1,), (1,)), ((), ())), preferred_element_type=F32)


def _split2(x):
    hi = x.astype(BF16)
    lo = (x - hi.astype(F32)).astype(BF16)
    return hi, lo


def _split3(x):
    hi = x.astype(BF16)
    r = x - hi.astype(F32)
    mid = r.astype(BF16)
    lo = (r - mid.astype(F32)).astype(BF16)
    return hi, mid, lo


def _masked_softmax(s, mask):
    s = jnp.where(mask, s, NEG_INF)
    m = jnp.max(s, axis=-1, keepdims=True)
    p = jnp.where(mask, jnp.exp(s - m), 0.0)
    return p / jnp.maximum(jnp.sum(p, axis=-1, keepdims=True), TINY)


def _row_tile(m, pref):
    return pref if m % pref == 0 else m


Q_END = NSA_WIDTH
KV_END = Q_END + 6 * KV_WIDTH
U_OFF = KV_END


def _inproj_kernel(x_ref, g_ref, w_ref, qbf, kc, vc, ks, vs, kw, vw, ksb, vsb, kwb, vwb, u, gates, *, ssm_ch):
    xn = _rms(x_ref[...], g_ref[...]).astype(BF16)

    def mm(lo, hi):
        return _dot(xn, w_ref[:, lo:hi])

    qbf[...] = (mm(0, Q_END) * (HEAD_DIM ** -0.5)).astype(BF16)
    for i, (f, b) in enumerate(((kc, None), (vc, None), (ks, ksb), (vs, vsb), (kw, kwb), (vw, vwb))):
        z = mm(Q_END + KV_WIDTH * i, Q_END + KV_WIDTH * (i + 1))
        f[...] = z
        if b is not None:
            b[...] = z.astype(BF16)
    u[...] = mm(U_OFF, U_OFF + ssm_ch)
    gates[...] = jax.nn.sigmoid(mm(U_OFF + ssm_ch, U_OFF + ssm_ch + GATE_COLS))


def _prep_w_in(w_in, ssm_ch):
    wq = w_in[:, :KV_END]
    wg = w_in[:, KV_END:KV_END + 3 * N_HEADS]
    wu = w_in[:, KV_END + 3 * N_HEADS:]
    d = w_in.shape[0]
    wg = wg.reshape(d, KV_GROUPS, HEADS_PER_GROUP, 3).transpose(0, 1, 3, 2).reshape(d, KV_GROUPS, 3 * HEADS_PER_GROUP)
    wg = jnp.pad(wg, ((0, 0), (0, 0), (0, LANES - 3 * HEADS_PER_GROUP))).reshape(d, GATE_COLS)
    return jnp.concatenate([wq, wu, wg], axis=1).astype(BF16)


def _inproj(x, g, w, ssm_ch, tm_pref):
    m, d = x.shape
    tm = _row_tile(m, tm_pref)
    ncols = w.shape[1]
    row = lambda c: pl.BlockSpec((tm, c), lambda i: (i, 0))
    f32s = lambda c: jax.ShapeDtypeStruct((m, c), F32)
    bfs = lambda c: jax.ShapeDtypeStruct((m, c), BF16)
    out_shape = ([bfs(NSA_WIDTH)] + [f32s(KV_WIDTH)] * 6 + [bfs(KV_WIDTH)] * 4 + [f32s(ssm_ch), f32s(GATE_COLS)])
    out_specs = ([row(NSA_WIDTH)] + [row(KV_WIDTH)] * 10 + [row(ssm_ch), row(GATE_COLS)])
    return pl.pallas_call(
        functools.partial(_inproj_kernel, ssm_ch=ssm_ch),
        grid=(m // tm,),
        in_specs=[row(d), pl.BlockSpec((1, d), lambda i: (0, 0)),
                  pl.BlockSpec((d, ncols), lambda i: (0, 0), pipeline_mode=pl.Buffered(1))],
        out_specs=out_specs,
        out_shape=out_shape,
        compiler_params=_cparams(("parallel",)),
        name="inproj",
    )(x, g, w)


CMP_PAIRS = CMP_STRIDE // 2


def _prep_cmp_w1(w1):
    w = jnp.concatenate([w1[:CMP_STRIDE], w1[CMP_STRIDE:]], axis=-1).astype(BF16)
    return w.reshape(CMP_PAIRS, 2 * HEAD_DIM, 2 * HEAD_DIM)


def _cmp_prompt_kernel(kc_ref, vc_ref, wk_ref, wv_ref, ok_ref, ov_ref, *, n_ch):
    for src, w, dst in ((kc_ref, wk_ref, ok_ref), (vc_ref, wv_ref, ov_ref)):
        acc = jnp.zeros((n_ch, 2 * HEAD_DIM), F32)
        for jj in range(CMP_PAIRS):
            xj = jnp.concatenate([src[pl.ds(2 * jj + r, n_ch, stride=CMP_STRIDE), :] for r in range(2)], axis=1)
            acc = acc + _dot(xj.astype(BF16), w[jj])
        dst[...] = acc


def _cmp_prompt(kc, vc, wk, wv):
    b, s, _ = kc.shape
    n_ch = s // CMP_STRIDE
    tok = pl.BlockSpec((None, s, HEAD_DIM), lambda i, g: (i, 0, g))
    wsp = pl.BlockSpec((CMP_PAIRS, 2 * HEAD_DIM, 2 * HEAD_DIM), lambda i, g: (0, 0, 0))
    osp = pl.BlockSpec((None, None, n_ch, 2 * HEAD_DIM), lambda i, g: (i, g, 0, 0))
    osh = jax.ShapeDtypeStruct((b, KV_GROUPS, n_ch, 2 * HEAD_DIM), F32)
    return pl.pallas_call(
        functools.partial(_cmp_prompt_kernel, n_ch=n_ch),
        grid=(b, KV_GROUPS),
        in_specs=[tok, tok, wsp, wsp],
        out_specs=[osp, osp],
        out_shape=[osh, osh],
        compiler_params=_cparams(("parallel", "parallel")),
        name="cmp_prompt",
    )(kc, vc, wk, wv)


def _cmp_finish_kernel(*refs, nch, has_tail):
    if has_tail:
        lk, lv, kn, vn, pek, pev, w1k, w1v, w2k, w2v, ock, ocv = refs
    else:
        lk, lv, pek, pev, w1k, w1v, w2k, w2v, ock, ocv = refs
        kn = vn = None
    row = lax.broadcasted_iota(jnp.int32, (nch, 1), 0)
    half = CMP_STRIDE * HEAD_DIM
    for l_ref, n_ref, pe_ref, w1_ref, w2_ref, o_ref in ((lk, kn, pek, w1k, w2k, ock), (lv, vn, pev, w1v, w2v, ocv)):
        lohi = l_ref[...]
        lo = lohi[:, :HEAD_DIM]
        hi = lohi[:, HEAD_DIM:]
        pe = jnp.broadcast_to(pe_ref[0:1, :], (SUBLANES, half)).astype(BF16)
        b_lo = _dot(pe, w1_ref[0:half, :].astype(BF16))[0:1]
        pe = jnp.broadcast_to(pe_ref[1:2, :], (SUBLANES, half)).astype(BF16)
        b_hi = _dot(pe, w1_ref[half:2 * half, :].astype(BF16))[0:1]
        hi_next = pltpu.roll(hi, nch - 1, 0)
        if has_tail:
            new = jnp.broadcast_to(n_ref[...], (SUBLANES, HEAD_DIM)).astype(BF16)
            tail = _dot(new, w1_ref[half:half + HEAD_DIM, :].astype(BF16))[0:1]
            hi_next = jnp.where(row == nch - 1, tail, hi_next)
        hid = _gelu(lo + hi_next + (b_lo + b_hi))
        o_ref[...] = _dot(hid.astype(BF16), w2_ref[...].astype(BF16)).astype(BF16)


def _cmp_finish(lk, lv, pek, pev, w1k, w1v, w2k, w2v, kn=None, vn=None):
    b, _, nch, _ = lk.shape
    has_tail = kn is not None
    lsp = pl.BlockSpec((None, None, nch, 2 * HEAD_DIM), lambda i, g: (i, g, 0, 0))
    full = lambda a: pl.BlockSpec(a.shape, lambda i, g: (0,) * a.ndim)
    pek2 = pek.reshape(2, CMP_STRIDE * HEAD_DIM)
    pev2 = pev.reshape(2, CMP_STRIDE * HEAD_DIM)
    w1k2 = w1k.reshape(CMP_BLOCK * HEAD_DIM, HEAD_DIM)
    w1v2 = w1v.reshape(CMP_BLOCK * HEAD_DIM, HEAD_DIM)
    args = [lk, lv]
    specs = [lsp, lsp]
    if has_tail:
        nsp = pl.BlockSpec((None, 1, HEAD_DIM), lambda i, g: (i, 0, g))
        args += [kn, vn]
        specs += [nsp, nsp]
    params = [pek2, pev2, w1k2, w1v2, w2k, w2v]
    args += params
    specs += [full(a) for a in params]
    osp = pl.BlockSpec((None, None, nch, HEAD_DIM), lambda i, g: (i, g, 0, 0))
    osh = jax.ShapeDtypeStruct((b, KV_GROUPS, nch, HEAD_DIM), BF16)
    return pl.pallas_call(
        functools.partial(_cmp_finish_kernel, nch=nch, has_tail=has_tail),
        grid=(b, KV_GROUPS),
        in_specs=specs,
        out_specs=[osp, osp],
        out_shape=[osh, osh],
        compiler_params=_cparams(("parallel", "parallel")),
        name="cmp_finish",
    )(*args)


def _imp_matrix(n_cmp_pad, n_blk_pad):
    ratio = SEL_BLOCK // CMP_STRIDE
    c = np.arange(n_cmp_pad)[:, None]
    j = np.arange(n_blk_pad)[None, :]
    return jnp.asarray(((c >= ratio * j - 1) & (c <= ratio * j + ratio - 1)).astype(np.float32), dtype=BF16)


def _expand_matrix(n_blk_pad, n_keys):
    j = np.arange(n_blk_pad)[:, None]
    k = np.arange(n_keys)[None, :]
    return jnp.asarray((k // SEL_BLOCK == j).astype(np.float32), dtype=BF16)


def _group_slopes(g):
    sg = jnp.where(g == 0, 1.0, 2.0 ** -HEADS_PER_GROUP).astype(F32)
    return [jnp.full((Q_BLOCK, 1), 2.0 ** -(h + 1), F32) * sg for h in range(HEADS_PER_GROUP)]


def _nsa_prompt_kernel(q_ref, ck_ref, cv_ref, ks_ref, vs_ref, kw_ref, vw_ref, gt_ref, a_ref, e_ref, o_ref, sk_ref,
                       *, seq, n_cmp, n_blk, n_pick):
    g = pl.program_id(1)
    n = pl.program_id(2)
    rows = HEADS_PER_GROUP * Q_BLOCK
    qb = q_ref[...]
    q = jnp.concatenate([qb[:, h * HEAD_DIM:(h + 1) * HEAD_DIM] for h in range(HEADS_PER_GROUP)], axis=0)
    qpos1 = n * Q_BLOCK + lax.broadcasted_iota(jnp.int32, (Q_BLOCK, 1), 0)
    qpos = jnp.concatenate([qpos1] * HEADS_PER_GROUP, axis=0)
    slope = jnp.concatenate(_group_slopes(g), axis=0)

    ncp = ck_ref.shape[0]
    cidx = lax.broadcasted_iota(jnp.int32, (1, ncp), 1)
    dist = qpos - (cidx * CMP_STRIDE + (CMP_BLOCK - 1))
    mask = (dist >= 0) & (cidx < n_cmp)
    p = _masked_softmax(_dot_nt(q, ck_ref[...]) - slope * dist.astype(F32), mask)
    o_c = _dot(p.astype(BF16), cv_ref[...])
    p_sum = p[0:Q_BLOCK]
    for h in range(1, HEADS_PER_GROUP):
        p_sum = p_sum + p[h * Q_BLOCK:(h + 1) * Q_BLOCK]

    imp = sum(_dot(t, a_ref[...]) for t in _split3(p_sum))
    blk = lax.broadcasted_iota(jnp.int32, (1, LANES), 1)
    cur = qpos1 // SEL_BLOCK
    valid = blk <= cur
    forced = (blk == 0) | (blk == cur) | (blk == cur - 1)
    score = jnp.where(forced & valid, FORCE_SCORE, jnp.where(valid, imp, -FORCE_SCORE))
    score_t = score.T[:n_blk]
    bidx = lax.broadcasted_iota(jnp.int32, (n_blk, 1), 0)
    rank = jnp.zeros((n_blk, Q_BLOCK), F32)
    for j in range(n_blk):
        r = score_t[j:j + 1, :]
        rank = rank + jnp.where((r > score_t) | ((r == score_t) & (j < bidx)), 1.0, 0.0)
    sel_t = jnp.where(rank < n_pick, 1.0, 0.0)
    if n_blk < LANES:
        sel_t = jnp.concatenate([sel_t, jnp.zeros((LANES - n_blk, Q_BLOCK), F32)], axis=0)
    sk_ref[...] = _dot(sel_t.T.astype(BF16), e_ref[...])

    tk = min(SEL_KV_TILE, seq)
    n_tiles = (n * Q_BLOCK + Q_BLOCK + tk - 1) // tk
    col = lax.broadcasted_iota(jnp.int32, (1, HEAD_DIM), 1)
    q_aug = jnp.where(col == 0, slope * float(SEL_BLOCK), jnp.where(col == 1, slope, 0.0)).astype(BF16)
    q2 = jnp.concatenate([q, q_aug], axis=1)
    krow = lax.broadcasted_iota(jnp.int32, (tk, 1), 0)

    def body(t, carry):
        m, l, acc = carry
        k0 = pl.multiple_of(t * tk, tk)
        kpos = k0 + lax.broadcasted_iota(jnp.int32, (1, tk), 1)
        kp_col = k0 + krow
        k_aug = jnp.where(col == 0, kp_col // SEL_BLOCK, jnp.where(col == 1, kp_col % SEL_BLOCK, 0))
        k2 = jnp.concatenate([ks_ref[pl.ds(k0, tk), :], k_aug.astype(F32).astype(BF16)], axis=1)
        picked = sk_ref[:, pl.ds(k0, tk)] > 0.5
        bias = jnp.where((kpos <= qpos1) & picked, 0.0, NEG_INF)
        s = _dot_nt(q2, k2) + jnp.concatenate([bias] * HEADS_PER_GROUP, axis=0)
        m_new = jnp.maximum(m, jnp.max(s, axis=-1, keepdims=True))
        p = jnp.exp(s - m_new)
        alpha = jnp.exp(m - m_new)
        l = alpha * l + jnp.sum(p, axis=-1, keepdims=True)
        acc = alpha * acc + _dot(p.astype(BF16), vs_ref[pl.ds(k0, tk), :])
        return m_new, l, acc

    init = (jnp.full((rows, 1), NEG_INF, F32), jnp.zeros((rows, 1), F32), jnp.zeros((rows, HEAD_DIM), F32))
    _, l, acc = lax.fori_loop(0, n_tiles, body, init)
    o_s = acc / jnp.maximum(l, TINY)

    band = min(WINDOW + Q_BLOCK, seq)
    st = pl.multiple_of(jnp.minimum(jnp.maximum(n * Q_BLOCK - WINDOW, 0), seq - band), Q_BLOCK)
    kpos = st + lax.broadcasted_iota(jnp.int32, (1, band), 1)
    dist = qpos - kpos
    mask = (dist >= 0) & (dist < WINDOW)
    p = _masked_softmax(_dot_nt(q, kw_ref[pl.ds(st, band), :]) - slope * dist.astype(F32), mask)
    o_w = _dot(p.astype(BF16), vw_ref[pl.ds(st, band), :])

    gt = gt_ref[...]
    for h in range(HEADS_PER_GROUP):
        sl = slice(h * Q_BLOCK, (h + 1) * Q_BLOCK)
        o_ref[:, h * HEAD_DIM:(h + 1) * HEAD_DIM] = (
            gt[:, h:h + 1] * o_c[sl]
            + gt[:, HEADS_PER_GROUP + h:HEADS_PER_GROUP + h + 1] * o_s[sl]
            + gt[:, 2 * HEADS_PER_GROUP + h:2 * HEADS_PER_GROUP + h + 1] * o_w[sl])


def _nsa_prompt(qbf, ck, cv, ksb, vsb, kwb, vwb, gates):
    b, s, _ = qbf.shape
    ncp = ck.shape[2]
    n_cmp = s // CMP_STRIDE - 1
    n_blk = s // SEL_BLOCK
    assert s % Q_BLOCK == 0 and TOP_N <= n_blk <= LANES
    a = _imp_matrix(ncp, LANES)
    e = _expand_matrix(LANES, s)
    gw = HEADS_PER_GROUP * HEAD_DIM
    qsp = pl.BlockSpec((None, Q_BLOCK, gw), lambda i, g, n: (i, n, g))
    csp = pl.BlockSpec((None, None, ncp, HEAD_DIM), lambda i, g, n: (i, g, 0, 0))
    ksp = pl.BlockSpec((None, s, HEAD_DIM), lambda i, g, n: (i, 0, g))
    gsp = pl.BlockSpec((None, Q_BLOCK, LANES), lambda i, g, n: (i, n, g))
    full = lambda x: pl.BlockSpec(x.shape, lambda i, g, n: (0,) * x.ndim)
    return pl.pallas_call(
        functools.partial(_nsa_prompt_kernel, seq=s, n_cmp=n_cmp, n_blk=n_blk, n_pick=TOP_N),
        grid=(b, KV_GROUPS, s // Q_BLOCK),
        in_specs=[qsp, csp, csp, ksp, ksp, ksp, ksp, gsp, full(a), full(e)],
        out_specs=qsp,
        out_shape=jax.ShapeDtypeStruct((b, s, NSA_WIDTH), F32),
        scratch_shapes=[pltpu.VMEM((Q_BLOCK, s), F32)],
        compiler_params=_cparams(("parallel", "parallel", "arbitrary")),
        name="nsa_prompt",
    )(qbf, ck, cv, ksb, vsb, kwb, vwb, gates, a, e)


SSM_LANE_GROUPS = LANES // SSM_GROUP_CH
SSM_STATE_TILE = SSM_LANE_GROUPS * SSM_STATE


def _prep_s5(a_re, a_im, log_dt, b_re, b_im, c_re, c_im):
    ng = a_re.shape[0]
    nk = ng // SSM_LANE_GROUPS
    dt = jnp.exp(log_dt)[:, None]
    mag = jnp.exp(a_re * dt)
    lb_re = mag * jnp.cos(a_im * dt)
    lb_im = mag * jnp.sin(a_im * dt)
    den = a_re * a_re + a_im * a_im
    nr = lb_re - 1.0
    f_re = (nr * a_re + lb_im * a_im) / den
    f_im = (lb_im * a_re - nr * a_im) / den
    bb_re = f_re[..., None] * b_re - f_im[..., None] * b_im
    bb_im = f_re[..., None] * b_im + f_im[..., None] * b_re
    eye = jnp.eye(SSM_LANE_GROUPS, dtype=F32)

    def in_map(bb):
        t = bb.reshape(nk, SSM_LANE_GROUPS, SSM_STATE, SSM_GROUP_CH)
        t = jnp.einsum('kgnc,gh->kgchn', t, eye)
        return t.reshape(nk, LANES, SSM_STATE_TILE)

    def out_map(c):
        t = c.reshape(nk, SSM_LANE_GROUPS, SSM_GROUP_CH, SSM_STATE)
        t = jnp.einsum('kgcn,gh->kgnhc', t, eye)
        return t.reshape(nk, SSM_STATE_TILE, LANES)

    bb = jnp.concatenate([in_map(bb_re), in_map(bb_im)], axis=-1)
    cm = jnp.concatenate([out_map(c_re), -out_map(c_im)], axis=1)
    bb_hi = bb.astype(BF16)
    bb_lo = (bb - bb_hi.astype(F32)).astype(BF16)
    lam = jnp.stack([lb_re.reshape(-1), lb_im.reshape(-1)])
    return lam, bb_hi, bb_lo, cm.astype(BF16)


def _s5_kernel(u_ref, h0r_ref, h0i_ref, lam_ref, bbh_ref, bbl_ref, cm_ref, d_ref, wglu_ref, bglu_ref,
               o_ref, hr_ref, hi_ref, *scratch, rows, scan, precise):
    nk = bbh_ref.shape[0]
    st = SSM_STATE_TILE
    if scan:
        cr_scr, ci_scr = scratch
        t = pl.program_id(1)

        @pl.when(t == 0)
        def _():
            cr_scr[...] = h0r_ref[...]
            ci_scr[...] = h0i_ref[...]

        row8 = lax.broadcasted_iota(jnp.int32, (rows, 1), 0) % SUBLANES
    u = u_ref[...]
    ys = []
    for k in range(nk):
        uk = u[:, k * LANES:(k + 1) * LANES]
        if precise:
            uh, ul = _split2(uk)
            bu = _dot(uh, bbh_ref[k]) + _dot(ul, bbh_ref[k]) + _dot(uh, bbl_ref[k])
        else:
            bu = _dot(uk.astype(BF16), bbh_ref[k])
        br = bu[:, :st]
        bi = bu[:, st:]
        ks = slice(k * st, (k + 1) * st)
        lr = lam_ref[0:1, ks]
        li = lam_ref[1:2, ks]
        if scan:
            pr, pi = lr, li
            sh = 1
            while sh < min(SUBLANES, rows):
                sr = jnp.where(row8 >= sh, pltpu.roll(br, sh, 0), 0.0)
                si = jnp.where(row8 >= sh, pltpu.roll(bi, sh, 0), 0.0)
                br, bi = br + (pr * sr - pi * si), bi + (pr * si + pi * sr)
                pr, pi = pr * pr - pi * pi, 2.0 * (pr * pi)
                sh *= 2
            tr, ti = [lr], [li]
            for _ in range(min(SUBLANES, rows) - 1):
                tr, ti = tr + [tr[-1] * lr - ti[-1] * li], ti + [tr[-1] * li + ti[-1] * lr]
            tab_r = jnp.concatenate(tr, axis=0)
            tab_i = jnp.concatenate(ti, axis=0)
            c_r = cr_scr[:, ks]
            c_i = ci_scr[:, ks]
            out_r, out_i = [], []
            tile = min(SUBLANES, rows)
            for i0 in range(0, rows, tile):
                h_r = br[i0:i0 + tile] + (tab_r * c_r - tab_i * c_i)
                h_i = bi[i0:i0 + tile] + (tab_r * c_i + tab_i * c_r)
                out_r.append(h_r)
                out_i.append(h_i)
                c_r, c_i = h_r[tile - 1:tile], h_i[tile - 1:tile]
            br = jnp.concatenate(out_r, axis=0)
            bi = jnp.concatenate(out_i, axis=0)
            cr_scr[:, ks] = c_r
            ci_scr[:, ks] = c_i
        else:
            h0r = h0r_ref[:, ks]
            h0i = h0i_ref[:, ks]
            br, bi = br + (lr * h0r - li * h0i), bi + (lr * h0i + li * h0r)
            hr_ref[:, ks] = br
            hi_ref[:, ks] = bi
        hcat = jnp.concatenate([br, bi], axis=1).astype(BF16)
        ys.append(_dot(hcat, cm_ref[k]) + d_ref[:, k * LANES:(k + 1) * LANES] * uk)
    v = _gelu(jnp.concatenate(ys, axis=1))
    o_ref[...] = v * jax.nn.sigmoid(_dot(v.astype(BF16), wglu_ref[...]) + bglu_ref[...])
    if scan:
        hr_ref[...] = cr_scr[...]
        hi_ref[...] = ci_scr[...]


def _s5(u, h0r, h0i, prep, d, wglu, bglu, *, scan, precise, tc_pref=256):
    lam, bbh, bbl, cm = prep
    ns = lam.shape[1]
    full = lambda a: pl.BlockSpec(a.shape, lambda *idx: (0,) * a.ndim)
    params = [lam, bbh, bbl, cm, d, wglu, bglu]
    if scan:
        b, t, c = u.shape
        tc = _row_tile(t, tc_pref)
        grid = (b, t // tc)
        usp = pl.BlockSpec((None, tc, c), lambda i, j: (i, j, 0))
        hsp = pl.BlockSpec((None, 1, ns), lambda i, j: (i, 0, 0))
        rows = tc
        scratch = [pltpu.VMEM((1, ns), F32), pltpu.VMEM((1, ns), F32)]
        sem = ("parallel", "arbitrary")
        osh = [jax.ShapeDtypeStruct((b, t, c), F32)] + [jax.ShapeDtypeStruct((b, 1, ns), F32)] * 2
    else:
        b, c = u.shape
        grid = (1,)
        usp = pl.BlockSpec((b, c), lambda i: (0, 0))
        hsp = pl.BlockSpec((b, ns), lambda i: (0, 0))
        rows = b
        scratch = []
        sem = ("arbitrary",)
        osh = [jax.ShapeDtypeStruct((b, c), F32)] + [jax.ShapeDtypeStruct((b, ns), F32)] * 2
    return pl.pallas_call(
        functools.partial(_s5_kernel, rows=rows, scan=scan, precise=precise),
        grid=grid,
        in_specs=[usp, hsp, hsp] + [full(a) for a in params],
        out_specs=[usp, hsp, hsp],
        out_shape=osh,
        scratch_shapes=scratch,
        compiler_params=_cparams(sem),
        name="s5_scan" if scan else "s5_step",
    )(u, h0r, h0i, *params)


def _prep_router(rg_w, rg_b, re_w, re_b):
    d = rg_w.shape[0]
    w = jnp.concatenate([rg_w, re_w.transpose(1, 0, 2).reshape(d, N_EXPERTS)], axis=1)
    w = jnp.pad(w, ((0, 0), (0, ROUTER_COLS - w.shape[1])))
    b = jnp.pad(jnp.concatenate([rg_b, re_b.reshape(-1)]), (0, ROUTER_COLS - N_EXPERT_GROUPS - N_EXPERTS))
    hi = w.astype(BF16)
    lo = (w - hi.astype(F32)).astype(BF16)
    return hi, lo, b.reshape(1, ROUTER_COLS)


def _tail1_kernel(on_ref, os_ref, h_ref, gn_ref, gs_ref, wout_ref, gf_ref, wrh_ref, wrl_ref, br_ref,
                  h1_ref, xn_ref, rid_ref, rw_ref):
    cat = jnp.concatenate([_rms(on_ref[...], gn_ref[...]), _rms(os_ref[...], gs_ref[...])], axis=1)
    h1 = h_ref[...] + _dot(cat.astype(BF16), wout_ref[...])
    h1_ref[...] = h1
    xn = _rms(h1, gf_ref[...])
    xh, xl = _split2(xn)
    xn_ref[...] = xn
    lg = _dot(xh, wrh_ref[...]) + _dot(xl, wrh_ref[...]) + _dot(xh, wrl_ref[...]) + br_ref[...]
    lane_i = lax.broadcasted_iota(jnp.int32, (1, ROUTER_COLS), 1)
    lane = lane_i.astype(F32)
    ninf = -jnp.inf
    big = float(ROUTER_COLS)
    is_grp = lane_i < N_EXPERT_GROUPS
    lgm = jnp.where(is_grp, lg, ninf)
    mx = jnp.max(lgm, axis=-1, keepdims=True)
    grp = jnp.min(jnp.where(lgm == mx, lane, big), axis=-1, keepdims=True)
    p_grp = 1.0 / jnp.sum(jnp.where(is_grp, jnp.exp(lgm - mx), 0.0), axis=-1, keepdims=True)
    first = ROUTER_EXP_OFF + EXPERTS_PER_GROUP * grp
    lem = jnp.where((lane >= first) & (lane < first + EXPERTS_PER_GROUP), lg, ninf)
    v1 = jnp.max(lem, axis=-1, keepdims=True)
    i1 = jnp.min(jnp.where(lem == v1, lane, big), axis=-1, keepdims=True)
    lem2 = jnp.where(lane == i1, ninf, lem)
    v2 = jnp.max(lem2, axis=-1, keepdims=True)
    i2 = jnp.min(jnp.where(lem2 == v2, lane, big), axis=-1, keepdims=True)
    e2 = jnp.exp(v2 - v1)
    w1 = (1.0 / (1.0 + e2)) * p_grp
    w2 = (e2 / (1.0 + e2)) * p_grp
    rid_ref[...] = jnp.where(lane_i == 0, i1 - ROUTER_EXP_OFF,
                             jnp.where(lane_i == 1, i2 - ROUTER_EXP_OFF, 0.0)).astype(jnp.int32)
    rw_ref[...] = jnp.where(lane_i == 0, w1, jnp.where(lane_i == 1, w2, 0.0))


def _tail1(o_nsa, o_ssm, h, g_nsa, g_ssm, wout, g_ffn, router, tm_pref=256):
    m, d = h.shape
    tm = _row_tile(m, tm_pref)
    wrh, wrl, br = router
    row = lambda c: pl.BlockSpec((tm, c), lambda i: (i, 0))
    full = lambda a: pl.BlockSpec(a.shape, lambda i: (0,) * a.ndim)
    params = [g_nsa, g_ssm, wout, g_ffn, wrh, wrl, br]
    return pl.pallas_call(
        _tail1_kernel,
        grid=(m // tm,),
        in_specs=[row(o_nsa.shape[1]), row(o_ssm.shape[1]), row(d)] + [full(a) for a in params],
        out_specs=[row(d), row(d), row(ROUTER_COLS), row(ROUTER_COLS)],
        out_shape=[jax.ShapeDtypeStruct((m, d), F32), jax.ShapeDtypeStruct((m, d), F32),
                   jax.ShapeDtypeStruct((m, ROUTER_COLS), jnp.int32), jax.ShapeDtypeStruct((m, ROUTER_COLS), F32)],
        compiler_params=_cparams(("parallel",)),
        name="tail1",
    )(o_nsa, o_ssm, h, *params)


MOE_TILE = 256
MOE_GATHER_DEPTH = 4


def _moe_dispatch(rid):
    t = rid.shape[0]
    pairs = 2 * t
    e = rid.reshape(-1)
    onehot = (e[:, None] == jnp.arange(N_EXPERTS, dtype=jnp.int32)[None, :]).astype(jnp.int32)
    csum = jnp.cumsum(onehot, axis=0)
    rank = jnp.take_along_axis(csum, e[:, None], axis=1)[:, 0] - 1
    tiles = (csum[-1] + MOE_TILE - 1) // MOE_TILE
    tile_end = jnp.cumsum(tiles)
    pos = (tile_end - tiles)[e] * MOE_TILE + rank
    nt_max = -(-pairs // MOE_TILE) + N_EXPERTS
    n_tiles = tile_end[-1]
    tile_ids = jnp.arange(nt_max, dtype=jnp.int32)
    tile_expert = jnp.minimum(jnp.sum((tile_end[None, :] <= tile_ids[:, None]).astype(jnp.int32), axis=1), N_EXPERTS - 1)
    tile_expert = jnp.where(tile_ids < n_tiles, tile_expert, tile_expert[n_tiles - 1])
    row_token = jnp.zeros((nt_max * MOE_TILE,), jnp.int32).at[pos].set(jnp.arange(pairs, dtype=jnp.int32) // 2)
    return pos.astype(jnp.int32), tile_expert, row_token, n_tiles.astype(jnp.int32).reshape(1)


def _moe_sparse_kernel(te_ref, rt_ref, nt_ref, x_hbm, w1_ref, w3_ref, w2_ref, y_ref, xbuf, w1b, w3b, w2b, sem):
    t = pl.program_id(0)
    n = nt_ref[0]

    def row_copy(tile, slot, r):
        tok = rt_ref[tile * MOE_TILE + r]
        return pltpu.make_async_copy(x_hbm.at[pl.ds(tok, 1), :], xbuf.at[slot, pl.ds(r, 1), :], sem.at[slot])

    def start_tile(tile, slot):
        def body(r, c):
            row_copy(tile, slot, r).start()
            return c
        lax.fori_loop(0, MOE_TILE, body, 0, unroll=8)

    def wait_tile(tile, slot):
        def body(r, c):
            row_copy(tile, slot, r).wait()
            return c
        lax.fori_loop(0, MOE_TILE, body, 0, unroll=8)

    @pl.when(t == 0)
    def _():
        for t0 in range(MOE_GATHER_DEPTH - 1):
            @pl.when(t0 < n)
            def _(t0=t0):
                start_tile(t0, t0)

    ahead = t + (MOE_GATHER_DEPTH - 1)

    @pl.when(ahead < n)
    def _():
        start_tile(ahead, ahead % MOE_GATHER_DEPTH)

    @pl.when(t < n)
    def _():
        slot = t % MOE_GATHER_DEPTH
        wait_tile(t, slot)

        @pl.when((t == 0) | (te_ref[t] != te_ref[jnp.maximum(t - 1, 0)]))
        def _():
            w1b[...] = w1_ref[...].astype(BF16)
            w3b[...] = w3_ref[...].astype(BF16)
            w2b[...] = w2_ref[...].astype(BF16)

        x = xbuf[slot].astype(BF16)
        a = _dot(x, w1b[...])
        hid = (a * jax.nn.sigmoid(a)) * _dot(x, w3b[...])
        y_ref[...] = _dot(hid.astype(BF16), w2b[...])

    @pl.when(t >= n)
    def _():
        y_ref[...] = jnp.zeros_like(y_ref)


def _moe_sparse(xn, tile_expert, row_token, n_tiles, w1, w3, w2):
    _, d = xn.shape
    _, _, hdim = w1.shape
    nt_max = tile_expert.shape[0]
    wsp = lambda r, c: pl.BlockSpec((None, r, c), lambda t, te, rt, nt: (te[t], 0, 0))
    return pl.pallas_call(
        _moe_sparse_kernel,
        grid_spec=pltpu.PrefetchScalarGridSpec(
            num_scalar_prefetch=3,
            grid=(nt_max,),
            in_specs=[pl.BlockSpec(memory_space=pl.ANY), wsp(d, hdim), wsp(d, hdim), wsp(hdim, d)],
            out_specs=pl.BlockSpec((MOE_TILE, d), lambda t, te, rt, nt: (t, 0)),
            scratch_shapes=[pltpu.VMEM((MOE_GATHER_DEPTH, MOE_TILE, d), F32), pltpu.VMEM((d, hdim), BF16),
                            pltpu.VMEM((d, hdim), BF16), pltpu.VMEM((hdim, d), BF16),
                            pltpu.SemaphoreType.DMA((MOE_GATHER_DEPTH,))]),
        out_shape=jax.ShapeDtypeStruct((nt_max * MOE_TILE, d), F32),
        compiler_params=_cparams(("arbitrary",)),
        name="moe_sparse",
    )(tile_expert, row_token, n_tiles, xn, w1, w3, w2)


def _tail2_kernel(pos_ref, h1_ref, rw_ref, y_hbm, p_ref, gp_ref, wg_ref, bg_ref, wp_ref, gfin_ref, o_ref, ybuf, sem,
                  *, tm, tok_off):
    i = pl.program_id(0)

    def row_copy(step, buf, r, k):
        base = (tok_off + step * tm) * 2
        return pltpu.make_async_copy(y_hbm.at[pl.ds(pos_ref[base + 2 * r + k], 1), :],
                                     ybuf.at[buf, k, pl.ds(r, 1), :], sem.at[buf])

    def each_row(step, buf, fn):
        def body(r, c):
            fn(row_copy(step, buf, r, 0))
            fn(row_copy(step, buf, r, 1))
            return c
        lax.fori_loop(0, tm, body, 0, unroll=8)

    @pl.when(i == 0)
    def _():
        each_row(0, 0, lambda cp: cp.start())

    @pl.when(i + 1 < pl.num_programs(0))
    def _():
        each_row(i + 1, (i + 1) % 2, lambda cp: cp.start())

    emb = _dot(p_ref[...].astype(BF16), wp_ref[...])
    buf = i % 2
    each_row(i, buf, lambda cp: cp.wait())
    rw = rw_ref[...]
    h = h1_ref[...] + (rw[:, 0:1] * ybuf[buf, 0] + rw[:, 1:2] * ybuf[buf, 1])
    gate = jax.nn.sigmoid(_dot(_rms(h, gp_ref[...]).astype(BF16), wg_ref[...]) + bg_ref[...])
    h = h + gate * emb
    o_ref[...] = _rms(h, gfin_ref[...])


def _tail2(pos, tok_off, h1, rw, y, p, g_ple, wg, bg, wp, g_fin, tm_pref=256):
    m, d = h1.shape
    tm = _row_tile(m, tm_pref)
    row = lambda c: pl.BlockSpec((tm, c), lambda i, ps: (i, 0))
    full = lambda a: pl.BlockSpec(a.shape, lambda i, ps: (0,) * a.ndim)
    params = [g_ple, wg, bg, wp, g_fin]
    return pl.pallas_call(
        functools.partial(_tail2_kernel, tm=tm, tok_off=tok_off),
        grid_spec=pltpu.PrefetchScalarGridSpec(
            num_scalar_prefetch=1,
            grid=(m // tm,),
            in_specs=[row(d), row(ROUTER_COLS), pl.BlockSpec(memory_space=pl.ANY), row(p.shape[1])]
            + [full(a) for a in params],
            out_specs=row(d),
            scratch_shapes=[pltpu.VMEM((2, 2, tm, d), F32), pltpu.SemaphoreType.DMA((2,))]),
        out_shape=jax.ShapeDtypeStruct((m, d), F32),
        compiler_params=_cparams(("arbitrary",)),
        name="tail2",
    )(pos, h1, rw, y, p, *params)


PAGES_PER_STEP = 16
CMP_PAGES_PER_STEP = 32
PAGE_ROWS_PER_TOKEN = KV_GROUPS


def _cmp_paged_kernel(pt_ref, *refs, pages):
    kp = refs[:pages]
    vp = refs[pages:2 * pages]
    wk_ref, wv_ref, ok_ref, ov_ref = refs[2 * pages:]
    chunks = kp[0].shape[0] // (PAGE_ROWS_PER_TOKEN * CMP_STRIDE)
    stride = PAGE_ROWS_PER_TOKEN * CMP_STRIDE
    for src, w, dst in ((kp, wk_ref, ok_ref), (vp, wv_ref, ov_ref)):
        for g in range(KV_GROUPS):
            acc = jnp.zeros((pages * chunks, 2 * HEAD_DIM), F32)
            for jj in range(CMP_PAIRS):
                xj = jnp.concatenate(
                    [jnp.concatenate([p[pl.ds(PAGE_ROWS_PER_TOKEN * (2 * jj + r) + g, chunks, stride=stride), :]
                                      for p in src], axis=0) for r in range(2)], axis=1)
                acc = acc + _dot(xj.astype(BF16), w[jj])
            dst[g] = acc


def _page_specs(page_rows, pages):
    return [pl.BlockSpec((page_rows, HEAD_DIM), functools.partial(lambda i, b, s, pt: (pt[b, s * pages + i], 0), i))
            for i in range(pages)]


def _cmp_paged(pool_k, pool_v, page_table, wk, wv, page_size):
    b, n_pages = page_table.shape
    pages = CMP_PAGES_PER_STEP if n_pages % CMP_PAGES_PER_STEP == 0 else n_pages
    page_rows = page_size * PAGE_ROWS_PER_TOKEN
    chunks = page_size // CMP_STRIDE
    wsp = pl.BlockSpec((CMP_PAIRS, 2 * HEAD_DIM, 2 * HEAD_DIM), lambda i, s, pt: (0, 0, 0))
    osp = pl.BlockSpec((None, KV_GROUPS, pages * chunks, 2 * HEAD_DIM), lambda i, s, pt: (i, 0, s, 0))
    osh = jax.ShapeDtypeStruct((b, KV_GROUPS, n_pages * chunks, 2 * HEAD_DIM), F32)
    return pl.pallas_call(
        functools.partial(_cmp_paged_kernel, pages=pages),
        grid_spec=pltpu.PrefetchScalarGridSpec(
            num_scalar_prefetch=1,
            grid=(b, n_pages // pages),
            in_specs=_page_specs(page_rows, pages) * 2 + [wsp, wsp],
            out_specs=[osp, osp]),
        out_shape=[osh, osh],
        compiler_params=_cparams(("parallel", "arbitrary")),
        name="cmp_paged",
    )(page_table, *([pool_k] * pages), *([pool_v] * pages), wk, wv)


def _nsa_sample_kernel(pt_ref, q_ref, ck_ref, cv_ref, a_ref, ksn_ref, vsn_ref, wk_ref, wv_ref, kwn_ref, vwn_ref,
                       gt_ref, *rest, pages, page_size, n_blk, n_cmp, n_pick, past_len):
    kp = rest[:pages]
    vp = rest[pages:2 * pages]
    o_ref, wko_ref, wvo_ref, sel_scr, oc_scr, m_scr, l_scr, acc_scr, q_scr, sel_smem, sel_sem = rest[2 * pages:]
    step = pl.program_id(1)
    last = pl.num_programs(1) - 1
    qpos = past_len
    nbp = sel_scr.shape[1]
    rowi = lax.broadcasted_iota(jnp.int32, (SUBLANES, 1), 0)
    lane = lax.broadcasted_iota(jnp.int32, (1, LANES), 1)
    blk = lax.broadcasted_iota(jnp.int32, (1, nbp), 1)

    def slopes(g):
        sl = jnp.zeros((SUBLANES, 1), F32)
        for h in range(HEADS_PER_GROUP):
            sl = jnp.where(rowi == h, 2.0 ** -(HEADS_PER_GROUP * g + h + 1), sl)
        return sl

    @pl.when(step == 0)
    def _():
        qrow = q_ref[...]
        ncp = ck_ref.shape[1]
        cidx = lax.broadcasted_iota(jnp.int32, (1, ncp), 1)
        dist = jnp.broadcast_to(qpos - (cidx * CMP_STRIDE + (CMP_BLOCK - 1)), (SUBLANES, ncp))
        mask = (dist >= 0) & (cidx < n_cmp)
        cur = qpos // SEL_BLOCK
        valid = blk <= cur
        forced = (blk == 0) | (blk == cur) | (blk == cur - 1)
        for g in range(KV_GROUPS):
            heads = [qrow[:, (HEADS_PER_GROUP * g + h) * HEAD_DIM:(HEADS_PER_GROUP * g + h + 1) * HEAD_DIM]
                     for h in range(HEADS_PER_GROUP)]
            q8 = jnp.concatenate(heads + [jnp.zeros((SUBLANES - HEADS_PER_GROUP, HEAD_DIM), BF16)], axis=0)
            q_scr[g] = q8
            p = _masked_softmax(_dot_nt(q8, ck_ref[g]) - slopes(g) * dist.astype(F32), mask)
            oc_scr[g] = _dot(p.astype(BF16), cv_ref[g])
            p_sum = p[0:1]
            for h in range(1, HEADS_PER_GROUP):
                p_sum = p_sum + p[h:h + 1]
            p_sum = jnp.broadcast_to(p_sum, (SUBLANES, ncp))
            imp = sum(_dot(t, a_ref[...]) for t in _split3(p_sum))
            score = jnp.where(forced & valid, FORCE_SCORE, jnp.where(valid, imp, -FORCE_SCORE))
            score = jnp.where(blk < n_blk, score, -2.0 * FORCE_SCORE)

            cnt = jnp.zeros((SUBLANES, nbp), F32)
            for c in range(-(-n_blk // SUBLANES)):
                j = SUBLANES * c + rowi
                sj = jnp.max(jnp.where(blk == j, score, -jnp.inf), axis=-1, keepdims=True)
                before = ((sj > score) | ((sj == score) & (j < blk))) & (j < n_blk)
                cnt = cnt + jnp.where(before, 1.0, 0.0)
            rank = jnp.broadcast_to(jnp.sum(cnt, axis=0, keepdims=True), (SUBLANES, nbp))
            sel_scr[g * SUBLANES:(g + 1) * SUBLANES, :] = jnp.where((rank < n_pick) & (blk < n_blk), 1, 0)
            m_scr[g] = jnp.full((SUBLANES, LANES), NEG_INF, F32)
            l_scr[g] = jnp.zeros((SUBLANES, LANES), F32)
            acc_scr[g] = jnp.zeros((SUBLANES, HEAD_DIM), F32)
        flags = pltpu.make_async_copy(sel_scr, sel_smem, sel_sem.at[0])
        flags.start()
        flags.wait()

    blocks_per_page = page_size // SEL_BLOCK
    for g in range(KV_GROUPS):
        for i in range(pages):
            pg = step * pages + i
            first_blk = blocks_per_page * pg
            flags = [sel_smem[g * SUBLANES, first_blk + r] for r in range(blocks_per_page)]

            @pl.when(sum(flags) > 0)
            def _(g=g, i=i, pg=pg, flags=flags):
                k = kp[i][pl.ds(g, page_size, stride=PAGE_ROWS_PER_TOKEN), :].astype(BF16)
                v = vp[i][pl.ds(g, page_size, stride=PAGE_ROWS_PER_TOKEN), :].astype(BF16)
                dist = qpos - (pg * page_size + lane)
                picked = jnp.zeros((1, LANES), jnp.int32)
                for r in range(blocks_per_page):
                    picked = jnp.where(lane // SEL_BLOCK == r, flags[r], picked)
                mask = jnp.broadcast_to((dist >= 0) & (picked > 0), (SUBLANES, LANES))
                s = jnp.where(mask, _dot_nt(q_scr[g], k) - slopes(g) * dist.astype(F32), NEG_INF)
                m_old = m_scr[g][:, 0:1]
                m_new = jnp.maximum(m_old, jnp.max(s, axis=-1, keepdims=True))
                p = jnp.where(mask, jnp.exp(s - m_new), 0.0)
                alpha = jnp.exp(m_old - m_new)
                acc_scr[g] = alpha * acc_scr[g] + _dot(p.astype(BF16), v)
                l_scr[g] = jnp.broadcast_to(alpha * l_scr[g][:, 0:1] + jnp.sum(p, axis=-1, keepdims=True),
                                            (SUBLANES, LANES))
                m_scr[g] = jnp.broadcast_to(m_new, (SUBLANES, LANES))

    @pl.when(step == last)
    def _():
        wb = wk_ref.shape[0]
        wrow = lax.broadcasted_iota(jnp.int32, (wb, 1), 0)
        wk_new = jnp.where(wrow == wb - 1, kwn_ref[...], pltpu.roll(wk_ref[...], wb - 1, 0))
        wv_new = jnp.where(wrow == wb - 1, vwn_ref[...], pltpu.roll(wv_ref[...], wb - 1, 0))
        wko_ref[...] = wk_new
        wvo_ref[...] = wv_new
        widx = lax.broadcasted_iota(jnp.int32, (1, wb), 1)
        wdist = jnp.broadcast_to(wb - 1 - widx, (SUBLANES, wb))
        wmask = (wdist >= 0) & (wdist < WINDOW) & (qpos - wdist >= 0)
        for g in range(KV_GROUPS):
            gs = slice(g * HEAD_DIM, (g + 1) * HEAD_DIM)
            q8 = q_scr[g]
            sl = slopes(g)
            kn = ksn_ref[:, gs].astype(BF16).astype(F32)
            vn = vsn_ref[:, gs].astype(BF16).astype(F32)
            s_n = jnp.sum(q8.astype(F32) * kn, axis=-1, keepdims=True)
            sel_n = sel_smem[g * SUBLANES, n_blk - 1] > 0
            m_old = m_scr[g][:, 0:1]
            m_new = jnp.maximum(m_old, jnp.where(sel_n, s_n, NEG_INF))
            p_n = jnp.where(sel_n, jnp.exp(s_n - m_new), 0.0)
            alpha = jnp.exp(m_old - m_new)
            acc = alpha * acc_scr[g] + p_n.astype(BF16).astype(F32) * vn
            l = alpha * l_scr[g][:, 0:1] + p_n
            o_s = acc / jnp.maximum(l, TINY)
            p = _masked_softmax(_dot_nt(q8, wk_new[:, gs].astype(BF16)) - sl * wdist.astype(F32), wmask)
            o_w = _dot(p.astype(BF16), wv_new[:, gs].astype(BF16))
            gt = jnp.broadcast_to(gt_ref[:, g * LANES:(g + 1) * LANES], (SUBLANES, LANES))
            gcol = [jnp.sum(jnp.where(lane == k * HEADS_PER_GROUP + rowi, gt, 0.0), axis=-1, keepdims=True)
                    for k in range(3)]
            o = gcol[0] * oc_scr[g] + gcol[1] * o_s + gcol[2] * o_w
            for h in range(HEADS_PER_GROUP):
                hd = HEADS_PER_GROUP * g + h
                o_ref[:, hd * HEAD_DIM:(hd + 1) * HEAD_DIM] = o[h:h + 1]


def _nsa_sample(qbf, ck, cv, ks_new, vs_new, win_k, win_v, kw_new, vw_new, gates, pool_k, pool_v, page_table,
                page_size):
    b, n_pages = page_table.shape
    past_len = n_pages * page_size
    ncp = ck.shape[2]
    n_cmp = (past_len + 1 + CMP_STRIDE - 1) // CMP_STRIDE - 1
    n_blk = (past_len + 1 + SEL_BLOCK - 1) // SEL_BLOCK
    nbp = -(-n_blk // LANES) * LANES
    wb = win_k.shape[1]
    assert wb == WINDOW and n_blk >= TOP_N and ncp == n_cmp and page_size % SEL_BLOCK == 0
    pages = PAGES_PER_STEP if n_pages % PAGES_PER_STEP == 0 else n_pages
    page_rows = page_size * PAGE_ROWS_PER_TOKEN
    a = _imp_matrix(ncp, nbp)
    tok = lambda c: pl.BlockSpec((None, 1, c), lambda i, s, pt: (i, 0, 0))
    csp = pl.BlockSpec((None, KV_GROUPS, ncp, HEAD_DIM), lambda i, s, pt: (i, 0, 0, 0))
    wsp = pl.BlockSpec((None, wb, KV_WIDTH), lambda i, s, pt: (i, 0, 0))
    asp = pl.BlockSpec(a.shape, lambda i, s, pt: (0, 0))
    grp_scr = lambda c, dt: pltpu.VMEM((KV_GROUPS, SUBLANES, c), dt)
    return pl.pallas_call(
        functools.partial(_nsa_sample_kernel, pages=pages, page_size=page_size, n_blk=n_blk, n_cmp=n_cmp,
                          n_pick=TOP_N, past_len=past_len),
        grid_spec=pltpu.PrefetchScalarGridSpec(
            num_scalar_prefetch=1,
            grid=(b, n_pages // pages),
            in_specs=[tok(NSA_WIDTH), csp, csp, asp, tok(KV_WIDTH), tok(KV_WIDTH), wsp, wsp, tok(KV_WIDTH),
                      tok(KV_WIDTH), tok(GATE_COLS)] + _page_specs(page_rows, pages) * 2,
            out_specs=[tok(NSA_WIDTH), wsp, wsp],
            scratch_shapes=[pltpu.VMEM((KV_GROUPS * SUBLANES, nbp), jnp.int32), grp_scr(HEAD_DIM, F32),
                            grp_scr(LANES, F32), grp_scr(LANES, F32), grp_scr(HEAD_DIM, F32), grp_scr(HEAD_DIM, BF16),
                            pltpu.SMEM((KV_GROUPS * SUBLANES, nbp), jnp.int32), pltpu.SemaphoreType.DMA((1,))]),
        out_shape=[jax.ShapeDtypeStruct((b, 1, NSA_WIDTH), F32), jax.ShapeDtypeStruct((b, wb, KV_WIDTH), F32),
                   jax.ShapeDtypeStruct((b, wb, KV_WIDTH), F32)],
        compiler_params=_cparams(("parallel", "arbitrary")),
        name="nsa_sample",
    )(page_table, qbf, ck, cv, a, ks_new, vs_new, win_k, win_v, kw_new, vw_new, gates,
      *([pool_k] * pages), *([pool_v] * pages))


def kernel(x_prompt, x_sample, p_prompt, p_sample, cache_cmp_k, cache_cmp_v, cache_sel_k, cache_sel_v, cache_win_k, cache_win_v, state_ssm_re, state_ssm_im, page_table, norm_mix, w_in, cmp_pe_k, cmp_w1_k, cmp_w2_k, cmp_pe_v, cmp_w1_v, cmp_w2_v, ssm_a_re, ssm_a_im, ssm_log_dt, ssm_b_re, ssm_b_im, ssm_c_re, ssm_c_im, ssm_d, w_glu, b_glu, norm_nsa_out, norm_ssm_out, w_out, norm_ffn, router_grp_w, router_grp_b, router_exp_w, router_exp_b, exp_w1, exp_w3, exp_w2, norm_ple, w_ple_gate, b_ple_gate, w_ple_proj, norm_final):
    depth = w_in.shape[0]
    assert depth == 1, "one layer per step"
    i = 0
    b, s, d = x_prompt.shape
    db, t, _ = x_sample.shape
    assert t == 1, "the sample group decodes one token per sequence"
    ssm_ch = w_glu.shape[1]
    page_size = cache_cmp_k.shape[2]
    n_phys = cache_cmp_k.shape[1]
    wp = min(WINDOW, s)
    row1 = lambda a: a.reshape(1, -1)

    w_inp = _prep_w_in(w_in[i], ssm_ch)
    wk1 = _prep_cmp_w1(cmp_w1_k[i])
    wv1 = _prep_cmp_w1(cmp_w1_v[i])
    s5p = _prep_s5(ssm_a_re[i], ssm_a_im[i], ssm_log_dt[i], ssm_b_re[i], ssm_b_im[i], ssm_c_re[i], ssm_c_im[i])
    router = _prep_router(router_grp_w[i], router_grp_b[i], router_exp_w[i], router_exp_b[i])
    wglu = w_glu[i].astype(BF16)
    wout = w_out[i].astype(BF16)
    wpg = w_ple_gate[i].astype(BF16)
    wpp = w_ple_proj[i].astype(BF16)
    cmp_par = (cmp_pe_k[i], cmp_pe_v[i], cmp_w1_k[i], cmp_w1_v[i], cmp_w2_k[i], cmp_w2_v[i])

    def tail1(h, o_nsa, o_ssm):
        return _tail1(o_nsa, o_ssm, h, row1(norm_nsa_out[i]), row1(norm_ssm_out[i]), wout, row1(norm_ffn[i]), router)

    def tail2(pos, tok_off, h1, rw, y, p):
        return _tail2(pos, tok_off, h1, rw, y, p, row1(norm_ple[i]), wpg, row1(b_ple_gate[i]), wpp, row1(norm_final))

    m = b * s
    hp = x_prompt.reshape(m, d)
    qbf, kc, vc, ks, vs, kw, vw, ksb, vsb, kwb, vwb, u, gates = _inproj(hp, row1(norm_mix[i]), w_inp, ssm_ch, 512)
    seq = lambda a: a.reshape(b, s, a.shape[-1])
    lk, lv = _cmp_prompt(seq(kc), seq(vc), wk1, wv1)
    ck, cv = _cmp_finish(lk, lv, *cmp_par)
    o_nsa = _nsa_prompt(seq(qbf), ck, cv, seq(ksb), seq(vsb), seq(kwb), seq(vwb), seq(gates))
    ns = s5p[0].shape[1]
    zero = jnp.zeros((b, 1, ns), F32)
    o_ssm, hre_p, him_p = _s5(seq(u), zero, zero, s5p, row1(ssm_d[i]), wglu, row1(b_glu[i]), scan=True, precise=False)
    h1_p, xn_p, rid_p, rw_p = tail1(hp, o_nsa.reshape(m, -1), o_ssm.reshape(m, -1))
    kv5 = lambda a: a.reshape(1, b, s, KV_GROUPS, HEAD_DIM)
    win5 = lambda a: a.reshape(b, s, KV_GROUPS, HEAD_DIM)[None, :, s - wp:]
    st4 = lambda a, n: a.reshape(1, n, ns // SSM_STATE, SSM_STATE)
    prompt_state = (kv5(kc), kv5(vc), kv5(ks), kv5(vs), win5(kw), win5(vw), st4(hre_p, b), st4(him_p, b))

    hs = x_sample.reshape(db, d)
    qbf, kc, vc, ks, vs, kw, vw, _, _, _, _, u, gates = _inproj(hs, row1(norm_mix[i]), w_inp, ssm_ch, 512)
    pool = lambda c: c[i].reshape(n_phys * page_size * KV_GROUPS, HEAD_DIM)
    one = lambda a: a.reshape(db, 1, a.shape[-1])
    lk, lv = _cmp_paged(pool(cache_cmp_k), pool(cache_cmp_v), page_table, wk1, wv1, page_size)
    ck, cv = _cmp_finish(lk, lv, *cmp_par, kn=one(kc), vn=one(vc))
    wb = cache_win_k.shape[2]
    o_nsa, win_k, win_v = _nsa_sample(one(qbf), ck, cv, one(ks), one(vs), cache_win_k[i].reshape(db, wb, KV_WIDTH),
                                      cache_win_v[i].reshape(db, wb, KV_WIDTH), one(kw), one(vw), one(gates),
                                      pool(cache_sel_k), pool(cache_sel_v), page_table, page_size)
    o_ssm, hre_s, him_s = _s5(u, state_ssm_re[i].reshape(db, ns), state_ssm_im[i].reshape(db, ns), s5p,
                              row1(ssm_d[i]), wglu, row1(b_glu[i]), scan=False, precise=True)
    h1_s, xn_s, rid_s, rw_s = tail1(hs, o_nsa.reshape(db, -1), o_ssm)

    rid = jnp.concatenate([rid_p[:, :2], rid_s[:, :2]], axis=0)
    pos, tile_expert, row_token, n_tiles = _moe_dispatch(rid)
    y = _moe_sparse(jnp.concatenate([xn_p, xn_s], axis=0), tile_expert, row_token, n_tiles,
                    exp_w1[i], exp_w3[i], exp_w2[i])
    y_prompt = tail2(pos, 0, h1_p, rw_p, y, p_prompt[i].reshape(m, -1)).reshape(b, s, d)
    y_sample = tail2(pos, m, h1_s, rw_s, y, p_sample[i].reshape(db, -1)).reshape(db, 1, d)
    new5 = lambda a: a.reshape(1, db, 1, KV_GROUPS, HEAD_DIM)
    buf5 = lambda a: a.reshape(1, db, wb, KV_GROUPS, HEAD_DIM)
    sample_state = (new5(kc), new5(vc), new5(ks), new5(vs), buf5(win_k), buf5(win_v), st4(hre_s, db), st4(him_s, db))
    return (y_prompt, y_sample) + prompt_state + sample_state
```

```python
import functools

import jax
import jax.numpy as jnp
import numpy as np
from jax import lax
from jax.experimental import pallas as pl
from jax.experimental.pallas import tpu as pltpu

F32 = jnp.float32
BF16 = jnp.bfloat16

HEAD_DIM = 128
N_HEADS = 8
KV_GROUPS = 2
HEADS_PER_GROUP = N_HEADS // KV_GROUPS
NSA_WIDTH = N_HEADS * HEAD_DIM
KV_WIDTH = KV_GROUPS * HEAD_DIM
CMP_STRIDE = 16
CMP_BLOCK = 2 * CMP_STRIDE
SEL_BLOCK = 64
TOP_N = 16
WINDOW = 512
Q_BLOCK = 128
SSM_GROUP_CH = 16
SSM_STATE = 64
N_EXPERT_GROUPS = 4
EXPERTS_PER_GROUP = 8
N_EXPERTS = N_EXPERT_GROUPS * EXPERTS_PER_GROUP
RMS_EPS = 1e-6
NEG_INF = -1e30
FORCE_SCORE = 1e4
TINY = 1e-30

LANES = 128
SUBLANES = 8
GATE_COLS = KV_GROUPS * LANES
ROUTER_COLS = LANES
ROUTER_EXP_OFF = N_EXPERT_GROUPS
SEL_KV_TILE = 512
VMEM_LIMIT_BYTES = 56 * 1024 * 1024


def _cparams(sem):
    return pltpu.CompilerParams(dimension_semantics=sem, vmem_limit_bytes=VMEM_LIMIT_BYTES)


def _rms(x, g):
    return x * lax.rsqrt(jnp.mean(x * x, axis=-1, keepdims=True) + RMS_EPS) * g


def _gelu(x):
    return x * (0.5 * (1.0 + jnp.tanh(0.7978845608028654 * (x + 0.044715 * (x * x * x)))))


def _dot(a, b):
    return jnp.dot(a, b, preferred_element_type=F32)


def _dot_nt(a, b):
    return lax.dot_general(a, b, (((1,), (1,)), ((), ())), preferred_element_type=F32)


def _split2(x):
    hi = x.astype(BF16)
    lo = (x - hi.astype(F32)).astype(BF16)
    return hi, lo


def _split3(x):
    hi = x.astype(BF16)
    r = x - hi.astype(F32)
    mid = r.astype(BF16)
    lo = (r - mid.astype(F32)).astype(BF16)
    return hi, mid, lo


def _masked_softmax(s, mask):
    s = jnp.where(mask, s, NEG_INF)
    m = jnp.max(s, axis=-1, keepdims=True)
    p = jnp.where(mask, jnp.exp(s - m), 0.0)
    return p / jnp.maximum(jnp.sum(p, axis=-1, keepdims=True), TINY)


def _row_tile(m, pref):
    return pref if m % pref == 0 else m


Q_END = NSA_WIDTH
KV_END = Q_END + 6 * KV_WIDTH
U_OFF = KV_END


def _inproj_kernel(x_ref, g_ref, w_ref, qbf, kc, vc, ks, vs, kw, vw, ksb, vsb, kwb, vwb, u, gates, *, ssm_ch):
    xn = _rms(x_ref[...], g_ref[...]).astype(BF16)

    def mm(lo, hi):
        return _dot(xn, w_ref[:, lo:hi])

    qbf[...] = (mm(0, Q_END) * (HEAD_DIM ** -0.5)).astype(BF16)
    for i, (f, b) in enumerate(((kc, None), (vc, None), (ks, ksb), (vs, vsb), (kw, kwb), (vw, vwb))):
        z = mm(Q_END + KV_WIDTH * i, Q_END + KV_WIDTH * (i + 1))
        f[...] = z
        if b is not None:
            b[...] = z.astype(BF16)
    u[...] = mm(U_OFF, U_OFF + ssm_ch)
    gates[...] = jax.nn.sigmoid(mm(U_OFF + ssm_ch, U_OFF + ssm_ch + GATE_COLS))


def _prep_w_in(w_in, ssm_ch):
    wq = w_in[:, :KV_END]
    wg = w_in[:, KV_END:KV_END + 3 * N_HEADS]
    wu = w_in[:, KV_END + 3 * N_HEADS:]
    d = w_in.shape[0]
    wg = wg.reshape(d, KV_GROUPS, HEADS_PER_GROUP, 3).transpose(0, 1, 3, 2).reshape(d, KV_GROUPS, 3 * HEADS_PER_GROUP)
    wg = jnp.pad(wg, ((0, 0), (0, 0), (0, LANES - 3 * HEADS_PER_GROUP))).reshape(d, GATE_COLS)
    return jnp.concatenate([wq, wu, wg], axis=1).astype(BF16)


def _inproj(x, g, w, ssm_ch, tm_pref):
    m, d = x.shape
    tm = _row_tile(m, tm_pref)
    ncols = w.shape[1]
    row = lambda c: pl.BlockSpec((tm, c), lambda i: (i, 0))
    f32s = lambda c: jax.ShapeDtypeStruct((m, c), F32)
    bfs = lambda c: jax.ShapeDtypeStruct((m, c), BF16)
    out_shape = ([bfs(NSA_WIDTH)] + [f32s(KV_WIDTH)] * 6 + [bfs(KV_WIDTH)] * 4 + [f32s(ssm_ch), f32s(GATE_COLS)])
    out_specs = ([row(NSA_WIDTH)] + [row(KV_WIDTH)] * 10 + [row(ssm_ch), row(GATE_COLS)])
    return pl.pallas_call(
        functools.partial(_inproj_kernel, ssm_ch=ssm_ch),
        grid=(m // tm,),
        in_specs=[row(d), pl.BlockSpec((1, d), lambda i: (0, 0)),
                  pl.BlockSpec((d, ncols), lambda i: (0, 0), pipeline_mode=pl.Buffered(1))],
        out_specs=out_specs,
        out_shape=out_shape,
        compiler_params=_cparams(("parallel",)),
        name="inproj",
    )(x, g, w)


CMP_PAIRS = CMP_STRIDE // 2


def _prep_cmp_w1(w1):
    w = jnp.concatenate([w1[:CMP_STRIDE], w1[CMP_STRIDE:]], axis=-1).astype(BF16)
    return w.reshape(CMP_PAIRS, 2 * HEAD_DIM, 2 * HEAD_DIM)


def _cmp_prompt_kernel(kc_ref, vc_ref, wk_ref, wv_ref, ok_ref, ov_ref, *, n_ch):
    for src, w, dst in ((kc_ref, wk_ref, ok_ref), (vc_ref, wv_ref, ov_ref)):
        acc = jnp.zeros((n_ch, 2 * HEAD_DIM), F32)
        for jj in range(CMP_PAIRS):
            xj = jnp.concatenate([src[pl.ds(2 * jj + r, n_ch, stride=CMP_STRIDE), :] for r in range(2)], axis=1)
            acc = acc + _dot(xj.astype(BF16), w[jj])
        dst[...] = acc


def _cmp_prompt(kc, vc, wk, wv):
    b, s, _ = kc.shape
    n_ch = s // CMP_STRIDE
    tok = pl.BlockSpec((None, s, HEAD_DIM), lambda i, g: (i, 0, g))
    wsp = pl.BlockSpec((CMP_PAIRS, 2 * HEAD_DIM, 2 * HEAD_DIM), lambda i, g: (0, 0, 0))
    osp = pl.BlockSpec((None, None, n_ch, 2 * HEAD_DIM), lambda i, g: (i, g, 0, 0))
    osh = jax.ShapeDtypeStruct((b, KV_GROUPS, n_ch, 2 * HEAD_DIM), F32)
    return pl.pallas_call(
        functools.partial(_cmp_prompt_kernel, n_ch=n_ch),
        grid=(b, KV_GROUPS),
        in_specs=[tok, tok, wsp, wsp],
        out_specs=[osp, osp],
        out_shape=[osh, osh],
        compiler_params=_cparams(("parallel", "parallel")),
        name="cmp_prompt",
    )(kc, vc, wk, wv)


def _cmp_finish_kernel(*refs, nch, has_tail):
    if has_tail:
        lk, lv, kn, vn, pek, pev, w1k, w1v, w2k, w2v, ock, ocv = refs
    else:
        lk, lv, pek, pev, w1k, w1v, w2k, w2v, ock, ocv = refs
        kn = vn = None
    row = lax.broadcasted_iota(jnp.int32, (nch, 1), 0)
    half = CMP_STRIDE * HEAD_DIM
    for l_ref, n_ref, pe_ref, w1_ref, w2_ref, o_ref in ((lk, kn, pek, w1k, w2k, ock), (lv, vn, pev, w1v, w2v, ocv)):
        lohi = l_ref[...]
        lo = lohi[:, :HEAD_DIM]
        hi = lohi[:, HEAD_DIM:]
        pe = jnp.broadcast_to(pe_ref[0:1, :], (SUBLANES, half)).astype(BF16)
        b_lo = _dot(pe, w1_ref[0:half, :].astype(BF16))[0:1]
        pe = jnp.broadcast_to(pe_ref[1:2, :], (SUBLANES, half)).astype(BF16)
        b_hi = _dot(pe, w1_ref[half:2 * half, :].astype(BF16))[0:1]
        hi_next = pltpu.roll(hi, nch - 1, 0)
        if has_tail:
            new = jnp.broadcast_to(n_ref[...], (SUBLANES, HEAD_DIM)).astype(BF16)
            tail = _dot(new, w1_ref[half:half + HEAD_DIM, :].astype(BF16))[0:1]
            hi_next = jnp.where(row == nch - 1, tail, hi_next)
        hid = _gelu(lo + hi_next + (b_lo + b_hi))
        o_ref[...] = _dot(hid.astype(BF16), w2_ref[...].astype(BF16)).astype(BF16)


def _cmp_finish(lk, lv, pek, pev, w1k, w1v, w2k, w2v, kn=None, vn=None):
    b, _, nch, _ = lk.shape
    has_tail = kn is not None
    lsp = pl.BlockSpec((None, None, nch, 2 * HEAD_DIM), lambda i, g: (i, g, 0, 0))
    full = lambda a: pl.BlockSpec(a.shape, lambda i, g: (0,) * a.ndim)
    pek2 = pek.reshape(2, CMP_STRIDE * HEAD_DIM)
    pev2 = pev.reshape(2, CMP_STRIDE * HEAD_DIM)
    w1k2 = w1k.reshape(CMP_BLOCK * HEAD_DIM, HEAD_DIM)
    w1v2 = w1v.reshape(CMP_BLOCK * HEAD_DIM, HEAD_DIM)
    args = [lk, lv]
    specs = [lsp, lsp]
    if has_tail:
        nsp = pl.BlockSpec((None, 1, HEAD_DIM), lambda i, g: (i, 0, g))
        args += [kn, vn]
        specs += [nsp, nsp]
    params = [pek2, pev2, w1k2, w1v2, w2k, w2v]
    args += params
    specs += [full(a) for a in params]
    osp = pl.BlockSpec((None, None, nch, HEAD_DIM), lambda i, g: (i, g, 0, 0))
    osh = jax.ShapeDtypeStruct((b, KV_GROUPS, nch, HEAD_DIM), BF16)
    return pl.pallas_call(
        functools.partial(_cmp_finish_kernel, nch=nch, has_tail=has_tail),
        grid=(b, KV_GROUPS),
        in_specs=specs,
        out_specs=[osp, osp],
        out_shape=[osh, osh],
        compiler_params=_cparams(("parallel", "parallel")),
        name="cmp_finish",
    )(*args)


def _imp_matrix(n_cmp_pad, n_blk_pad):
    ratio = SEL_BLOCK // CMP_STRIDE
    c = np.arange(n_cmp_pad)[:, None]
    j = np.arange(n_blk_pad)[None, :]
    return jnp.asarray(((c >= ratio * j - 1) & (c <= ratio * j + ratio - 1)).astype(np.float32), dtype=BF16)


def _expand_matrix(n_blk_pad, n_keys):
    j = np.arange(n_blk_pad)[:, None]
    k = np.arange(n_keys)[None, :]
    return jnp.asarray((k // SEL_BLOCK == j).astype(np.float32), dtype=BF16)


def _group_slopes(g):
    sg = jnp.where(g == 0, 1.0, 2.0 ** -HEADS_PER_GROUP).astype(F32)
    return [jnp.full((Q_BLOCK, 1), 2.0 ** -(h + 1), F32) * sg for h in range(HEADS_PER_GROUP)]


def _nsa_prompt_kernel(q_ref, ck_ref, cv_ref, ks_ref, vs_ref, kw_ref, vw_ref, gt_ref, a_ref, e_ref, o_ref, sk_ref,
                       *, seq, n_cmp, n_blk, n_pick):
    g = pl.program_id(1)
    n = pl.program_id(2)
    rows = HEADS_PER_GROUP * Q_BLOCK
    qb = q_ref[...]
    q = jnp.concatenate([qb[:, h * HEAD_DIM:(h + 1) * HEAD_DIM] for h in range(HEADS_PER_GROUP)], axis=0)
    qpos1 = n * Q_BLOCK + lax.broadcasted_iota(jnp.int32, (Q_BLOCK, 1), 0)
    qpos = jnp.concatenate([qpos1] * HEADS_PER_GROUP, axis=0)
    slope = jnp.concatenate(_group_slopes(g), axis=0)

    ncp = ck_ref.shape[0]
    cidx = lax.broadcasted_iota(jnp.int32, (1, ncp), 1)
    dist = qpos - (cidx * CMP_STRIDE + (CMP_BLOCK - 1))
    mask = (dist >= 0) & (cidx < n_cmp)
    p = _masked_softmax(_dot_nt(q, ck_ref[...]) - slope * dist.astype(F32), mask)
    o_c = _dot(p.astype(BF16), cv_ref[...])
    p_sum = p[0:Q_BLOCK]
    for h in range(1, HEADS_PER_GROUP):
        p_sum = p_sum + p[h * Q_BLOCK:(h + 1) * Q_BLOCK]

    imp = sum(_dot(t, a_ref[...]) for t in _split3(p_sum))
    blk = lax.broadcasted_iota(jnp.int32, (1, LANES), 1)
    cur = qpos1 // SEL_BLOCK
    valid = blk <= cur
    forced = (blk == 0) | (blk == cur) | (blk == cur - 1)
    score = jnp.where(forced & valid, FORCE_SCORE, jnp.where(valid, imp, -FORCE_SCORE))
    score_t = score.T[:n_blk]
    bidx = lax.broadcasted_iota(jnp.int32, (n_blk, 1), 0)
    rank = jnp.zeros((n_blk, Q_BLOCK), F32)
    for j in range(n_blk):
        r = score_t[j:j + 1, :]
        rank = rank + jnp.where((r > score_t) | ((r == score_t) & (j < bidx)), 1.0, 0.0)
    sel_t = jnp.where(rank < n_pick, 1.0, 0.0)
    if n_blk < LANES:
        sel_t = jnp.concatenate([sel_t, jnp.zeros((LANES - n_blk, Q_BLOCK), F32)], axis=0)
    sk_ref[...] = _dot(sel_t.T.astype(BF16), e_ref[...])

    tk = min(SEL_KV_TILE, seq)
    n_tiles = (n * Q_BLOCK + Q_BLOCK + tk - 1) // tk
    col = lax.broadcasted_iota(jnp.int32, (1, HEAD_DIM), 1)
    q_aug = jnp.where(col == 0, slope * float(SEL_BLOCK), jnp.where(col == 1, slope, 0.0)).astype(BF16)
    q2 = jnp.concatenate([q, q_aug], axis=1)
    krow = lax.broadcasted_iota(jnp.int32, (tk, 1), 0)

    def body(t, carry):
        m, l, acc = carry
        k0 = pl.multiple_of(t * tk, tk)
        kpos = k0 + lax.broadcasted_iota(jnp.int32, (1, tk), 1)
        kp_col = k0 + krow
        k_aug = jnp.where(col == 0, kp_col // SEL_BLOCK, jnp.where(col == 1, kp_col % SEL_BLOCK, 0))
        k2 = jnp.concatenate([ks_ref[pl.ds(k0, tk), :], k_aug.astype(F32).astype(BF16)], axis=1)
        picked = sk_ref[:, pl.ds(k0, tk)] > 0.5
        bias = jnp.where((kpos <= qpos1) & picked, 0.0, NEG_INF)
        s = _dot_nt(q2, k2) + jnp.concatenate([bias] * HEADS_PER_GROUP, axis=0)
        m_new = jnp.maximum(m, jnp.max(s, axis=-1, keepdims=True))
        p = jnp.exp(s - m_new)
        alpha = jnp.exp(m - m_new)
        l = alpha * l + jnp.sum(p, axis=-1, keepdims=True)
        acc = alpha * acc + _dot(p.astype(BF16), vs_ref[pl.ds(k0, tk), :])
        return m_new, l, acc

    init = (jnp.full((rows, 1), NEG_INF, F32), jnp.zeros((rows, 1), F32), jnp.zeros((rows, HEAD_DIM), F32))
    _, l, acc = lax.fori_loop(0, n_tiles, body, init)
    o_s = acc / jnp.maximum(l, TINY)

    band = min(WINDOW + Q_BLOCK, seq)
    st = pl.multiple_of(jnp.minimum(jnp.maximum(n * Q_BLOCK - WINDOW, 0), seq - band), Q_BLOCK)
    kpos = st + lax.broadcasted_iota(jnp.int32, (1, band), 1)
    dist = qpos - kpos
    mask = (dist >= 0) & (dist < WINDOW)
    p = _masked_softmax(_dot_nt(q, kw_ref[pl.ds(st, band), :]) - slope * dist.astype(F32), mask)
    o_w = _dot(p.astype(BF16), vw_ref[pl.ds(st, band), :])

    gt = gt_ref[...]
    for h in range(HEADS_PER_GROUP):
        sl = slice(h * Q_BLOCK, (h + 1) * Q_BLOCK)
        o_ref[:, h * HEAD_DIM:(h + 1) * HEAD_DIM] = (
            gt[:, h:h + 1] * o_c[sl]
            + gt[:, HEADS_PER_GROUP + h:HEADS_PER_GROUP + h + 1] * o_s[sl]
            + gt[:, 2 * HEADS_PER_GROUP + h:2 * HEADS_PER_GROUP + h + 1] * o_w[sl])


def _nsa_prompt(qbf, ck, cv, ksb, vsb, kwb, vwb, gates):
    b, s, _ = qbf.shape
    ncp = ck.shape[2]
    n_cmp = s // CMP_STRIDE - 1
    n_blk = s // SEL_BLOCK
    assert s % Q_BLOCK == 0 and TOP_N <= n_blk <= LANES
    a = _imp_matrix(ncp, LANES)
    e = _expand_matrix(LANES, s)
    gw = HEADS_PER_GROUP * HEAD_DIM
    qsp = pl.BlockSpec((None, Q_BLOCK, gw), lambda i, g, n: (i, n, g))
    csp = pl.BlockSpec((None, None, ncp, HEAD_DIM), lambda i, g, n: (i, g, 0, 0))
    ksp = pl.BlockSpec((None, s, HEAD_DIM), lambda i, g, n: (i, 0, g))
    gsp = pl.BlockSpec((None, Q_BLOCK, LANES), lambda i, g, n: (i, n, g))
    full = lambda x: pl.BlockSpec(x.shape, lambda i, g, n: (0,) * x.ndim)
    return pl.pallas_call(
        functools.partial(_nsa_prompt_kernel, seq=s, n_cmp=n_cmp, n_blk=n_blk, n_pick=TOP_N),
        grid=(b, KV_GROUPS, s // Q_BLOCK),
        in_specs=[qsp, csp, csp, ksp, ksp, ksp, ksp, gsp, full(a), full(e)],
        out_specs=qsp,
        out_shape=jax.ShapeDtypeStruct((b, s, NSA_WIDTH), F32),
        scratch_shapes=[pltpu.VMEM((Q_BLOCK, s), F32)],
        compiler_params=_cparams(("parallel", "parallel", "arbitrary")),
        name="nsa_prompt",
    )(qbf, ck, cv, ksb, vsb, kwb, vwb, gates, a, e)


SSM_LANE_GROUPS = LANES // SSM_GROUP_CH
SSM_STATE_TILE = SSM_LANE_GROUPS * SSM_STATE


def _prep_s5(a_re, a_im, log_dt, b_re, b_im, c_re, c_im):
    ng = a_re.shape[0]
    nk = ng // SSM_LANE_GROUPS
    dt = jnp.exp(log_dt)[:, None]
    mag = jnp.exp(a_re * dt)
    lb_re = mag * jnp.cos(a_im * dt)
    lb_im = mag * jnp.sin(a_im * dt)
    den = a_re * a_re + a_im * a_im
    nr = lb_re - 1.0
    f_re = (nr * a_re + lb_im * a_im) / den
    f_im = (lb_im * a_re - nr * a_im) / den
    bb_re = f_re[..., None] * b_re - f_im[..., None] * b_im
    bb_im = f_re[..., None] * b_im + f_im[..., None] * b_re
    eye = jnp.eye(SSM_LANE_GROUPS, dtype=F32)

    def in_map(bb):
        t = bb.reshape(nk, SSM_LANE_GROUPS, SSM_STATE, SSM_GROUP_CH)
        t = jnp.einsum('kgnc,gh->kgchn', t, eye)
        return t.reshape(nk, LANES, SSM_STATE_TILE)

    def out_map(c):
        t = c.reshape(nk, SSM_LANE_GROUPS, SSM_GROUP_CH, SSM_STATE)
        t = jnp.einsum('kgcn,gh->kgnhc', t, eye)
        return t.reshape(nk, SSM_STATE_TILE, LANES)

    bb = jnp.concatenate([in_map(bb_re), in_map(bb_im)], axis=-1)
    cm = jnp.concatenate([out_map(c_re), -out_map(c_im)], axis=1)
    bb_hi = bb.astype(BF16)
    bb_lo = (bb - bb_hi.astype(F32)).astype(BF16)
    lam = jnp.stack([lb_re.reshape(-1), lb_im.reshape(-1)])
    return lam, bb_hi, bb_lo, cm.astype(BF16)


def _s5_kernel(u_ref, h0r_ref, h0i_ref, lam_ref, bbh_ref, bbl_ref, cm_ref, d_ref, wglu_ref, bglu_ref,
               o_ref, hr_ref, hi_ref, *scratch, rows, scan, precise):
    nk = bbh_ref.shape[0]
    st = SSM_STATE_TILE
    if scan:
        cr_scr, ci_scr = scratch
        t = pl.program_id(1)

        @pl.when(t == 0)
        def _():
            cr_scr[...] = h0r_ref[...]
            ci_scr[...] = h0i_ref[...]

        row8 = lax.broadcasted_iota(jnp.int32, (rows, 1), 0) % SUBLANES
    u = u_ref[...]
    ys = []
    for k in range(nk):
        uk = u[:, k * LANES:(k + 1) * LANES]
        if precise:
            uh, ul = _split2(uk)
            bu = _dot(uh, bbh_ref[k]) + _dot(ul, bbh_ref[k]) + _dot(uh, bbl_ref[k])
        else:
            bu = _dot(uk.astype(BF16), bbh_ref[k])
        br = bu[:, :st]
        bi = bu[:, st:]
        ks = slice(k * st, (k + 1) * st)
        lr = lam_ref[0:1, ks]
        li = lam_ref[1:2, ks]
        if scan:
            pr, pi = lr, li
            sh = 1
            while sh < min(SUBLANES, rows):
                sr = jnp.where(row8 >= sh, pltpu.roll(br, sh, 0), 0.0)
                si = jnp.where(row8 >= sh, pltpu.roll(bi, sh, 0), 0.0)
                br, bi = br + (pr * sr - pi * si), bi + (pr * si + pi * sr)
                pr, pi = pr * pr - pi * pi, 2.0 * (pr * pi)
                sh *= 2
            tr, ti = [lr], [li]
            for _ in range(min(SUBLANES, rows) - 1):
                tr, ti = tr + [tr[-1] * lr - ti[-1] * li], ti + [tr[-1] * li + ti[-1] * lr]
            tab_r = jnp.concatenate(tr, axis=0)
            tab_i = jnp.concatenate(ti, axis=0)
            c_r = cr_scr[:, ks]
            c_i = ci_scr[:, ks]
            out_r, out_i = [], []
            tile = min(SUBLANES, rows)
            for i0 in range(0, rows, tile):
                h_r = br[i0:i0 + tile] + (tab_r * c_r - tab_i * c_i)
                h_i = bi[i0:i0 + tile] + (tab_r * c_i + tab_i * c_r)
                out_r.append(h_r)
                out_i.append(h_i)
                c_r, c_i = h_r[tile - 1:tile], h_i[tile - 1:tile]
            br = jnp.concatenate(out_r, axis=0)
            bi = jnp.concatenate(out_i, axis=0)
            cr_scr[:, ks] = c_r
            ci_scr[:, ks] = c_i
        else:
            h0r = h0r_ref[:, ks]
            h0i = h0i_ref[:, ks]
            br, bi = br + (lr * h0r - li * h0i), bi + (lr * h0i + li * h0r)
            hr_ref[:, ks] = br
            hi_ref[:, ks] = bi
        hcat = jnp.concatenate([br, bi], axis=1).astype(BF16)
        ys.append(_dot(hcat, cm_ref[k]) + d_ref[:, k * LANES:(k + 1) * LANES] * uk)
    v = _gelu(jnp.concatenate(ys, axis=1))
    o_ref[...] = v * jax.nn.sigmoid(_dot(v.astype(BF16), wglu_ref[...]) + bglu_ref[...])
    if scan:
        hr_ref[...] = cr_scr[...]
        hi_ref[...] = ci_scr[...]


def _s5(u, h0r, h0i, prep, d, wglu, bglu, *, scan, precise, tc_pref=256):
    lam, bbh, bbl, cm = prep
    ns = lam.shape[1]
    full = lambda a: pl.BlockSpec(a.shape, lambda *idx: (0,) * a.ndim)
    params = [lam, bbh, bbl, cm, d, wglu, bglu]
    if scan:
        b, t, c = u.shape
        tc = _row_tile(t, tc_pref)
        grid = (b, t // tc)
        usp = pl.BlockSpec((None, tc, c), lambda i, j: (i, j, 0))
        hsp = pl.BlockSpec((None, 1, ns), lambda i, j: (i, 0, 0))
        rows = tc
        scratch = [pltpu.VMEM((1, ns), F32), pltpu.VMEM((1, ns), F32)]
        sem = ("parallel", "arbitrary")
        osh = [jax.ShapeDtypeStruct((b, t, c), F32)] + [jax.ShapeDtypeStruct((b, 1, ns), F32)] * 2
    else:
        b, c = u.shape
        grid = (1,)
        usp = pl.BlockSpec((b, c), lambda i: (0, 0))
        hsp = pl.BlockSpec((b, ns), lambda i: (0, 0))
        rows = b
        scratch = []
        sem = ("arbitrary",)
        osh = [jax.ShapeDtypeStruct((b, c), F32)] + [jax.ShapeDtypeStruct((b, ns), F32)] * 2
    return pl.pallas_call(
        functools.partial(_s5_kernel, rows=rows, scan=scan, precise=precise),
        grid=grid,
        in_specs=[usp, hsp, hsp] + [full(a) for a in params],
        out_specs=[usp, hsp, hsp],
        out_shape=osh,
        scratch_shapes=scratch,
        compiler_params=_cparams(sem),
        name="s5_scan" if scan else "s5_step",
    )(u, h0r, h0i, *params)


def _prep_router(rg_w, rg_b, re_w, re_b):
    d = rg_w.shape[0]
    w = jnp.concatenate([rg_w, re_w.transpose(1, 0, 2).reshape(d, N_EXPERTS)], axis=1)
    w = jnp.pad(w, ((0, 0), (0, ROUTER_COLS - w.shape[1])))
    b = jnp.pad(jnp.concatenate([rg_b, re_b.reshape(-1)]), (0, ROUTER_COLS - N_EXPERT_GROUPS - N_EXPERTS))
    hi = w.astype(BF16)
    lo = (w - hi.astype(F32)).astype(BF16)
    return hi, lo, b.reshape(1, ROUTER_COLS)


def _tail1_kernel(on_ref, os_ref, h_ref, gn_ref, gs_ref, wout_ref, gf_ref, wrh_ref, wrl_ref, br_ref,
                  h1_ref, xn_ref, rid_ref, rw_ref):
    cat = jnp.concatenate([_rms(on_ref[...], gn_ref[...]), _rms(os_ref[...], gs_ref[...])], axis=1)
    h1 = h_ref[...] + _dot(cat.astype(BF16), wout_ref[...])
    h1_ref[...] = h1
    xn = _rms(h1, gf_ref[...])
    xh, xl = _split2(xn)
    xn_ref[...] = xn
    lg = _dot(xh, wrh_ref[...]) + _dot(xl, wrh_ref[...]) + _dot(xh, wrl_ref[...]) + br_ref[...]
    lane_i = lax.broadcasted_iota(jnp.int32, (1, ROUTER_COLS), 1)
    lane = lane_i.astype(F32)
    ninf = -jnp.inf
    big = float(ROUTER_COLS)
    is_grp = lane_i < N_EXPERT_GROUPS
    lgm = jnp.where(is_grp, lg, ninf)
    mx = jnp.max(lgm, axis=-1, keepdims=True)
    grp = jnp.min(jnp.where(lgm == mx, lane, big), axis=-1, keepdims=True)
    p_grp = 1.0 / jnp.sum(jnp.where(is_grp, jnp.exp(lgm - mx), 0.0), axis=-1, keepdims=True)
    first = ROUTER_EXP_OFF + EXPERTS_PER_GROUP * grp
    lem = jnp.where((lane >= first) & (lane < first + EXPERTS_PER_GROUP), lg, ninf)
    v1 = jnp.max(lem, axis=-1, keepdims=True)
    i1 = jnp.min(jnp.where(lem == v1, lane, big), axis=-1, keepdims=True)
    lem2 = jnp.where(lane == i1, ninf, lem)
    v2 = jnp.max(lem2, axis=-1, keepdims=True)
    i2 = jnp.min(jnp.where(lem2 == v2, lane, big), axis=-1, keepdims=True)
    e2 = jnp.exp(v2 - v1)
    w1 = (1.0 / (1.0 + e2)) * p_grp
    w2 = (e2 / (1.0 + e2)) * p_grp
    rid_ref[...] = jnp.where(lane_i == 0, i1 - ROUTER_EXP_OFF,
                             jnp.where(lane_i == 1, i2 - ROUTER_EXP_OFF, 0.0)).astype(jnp.int32)
    rw_ref[...] = jnp.where(lane_i == 0, w1, jnp.where(lane_i == 1, w2, 0.0))


def _tail1(o_nsa, o_ssm, h, g_nsa, g_ssm, wout, g_ffn, router, tm_pref=256):
    m, d = h.shape
    tm = _row_tile(m, tm_pref)
    wrh, wrl, br = router
    row = lambda c: pl.BlockSpec((tm, c), lambda i: (i, 0))
    full = lambda a: pl.BlockSpec(a.shape, lambda i: (0,) * a.ndim)
    params = [g_nsa, g_ssm, wout, g_ffn, wrh, wrl, br]
    return pl.pallas_call(
        _tail1_kernel,
        grid=(m // tm,),
        in_specs=[row(o_nsa.shape[1]), row(o_ssm.shape[1]), row(d)] + [full(a) for a in params],
        out_specs=[row(d), row(d), row(ROUTER_COLS), row(ROUTER_COLS)],
        out_shape=[jax.ShapeDtypeStruct((m, d), F32), jax.ShapeDtypeStruct((m, d), F32),
                   jax.ShapeDtypeStruct((m, ROUTER_COLS), jnp.int32), jax.ShapeDtypeStruct((m, ROUTER_COLS), F32)],
        compiler_params=_cparams(("parallel",)),
        name="tail1",
    )(o_nsa, o_ssm, h, *params)


MOE_TILE = 256
MOE_GATHER_DEPTH = 4


def _moe_dispatch(rid):
    t = rid.shape[0]
    pairs = 2 * t
    e = rid.reshape(-1)
    onehot = (e[:, None] == jnp.arange(N_EXPERTS, dtype=jnp.int32)[None, :]).astype(jnp.int32)
    csum = jnp.cumsum(onehot, axis=0)
    rank = jnp.take_along_axis(csum, e[:, None], axis=1)[:, 0] - 1
    tiles = (csum[-1] + MOE_TILE - 1) // MOE_TILE
    tile_end = jnp.cumsum(tiles)
    pos = (tile_end - tiles)[e] * MOE_TILE + rank
    nt_max = -(-pairs // MOE_TILE) + N_EXPERTS
    n_tiles = tile_end[-1]
    tile_ids = jnp.arange(nt_max, dtype=jnp.int32)
    tile_expert = jnp.minimum(jnp.sum((tile_end[None, :] <= tile_ids[:, None]).astype(jnp.int32), axis=1), N_EXPERTS - 1)
    tile_expert = jnp.where(tile_ids < n_tiles, tile_expert, tile_expert[n_tiles - 1])
    row_token = jnp.zeros((nt_max * MOE_TILE,), jnp.int32).at[pos].set(jnp.arange(pairs, dtype=jnp.int32) // 2)
    return pos.astype(jnp.int32), tile_expert, row_token, n_tiles.astype(jnp.int32).reshape(1)


def _moe_sparse_kernel(te_ref, rt_ref, nt_ref, x_hbm, w1_ref, w3_ref, w2_ref, y_ref, xbuf, w1b, w3b, w2b, sem):
    t = pl.program_id(0)
    n = nt_ref[0]

    def row_copy(tile, slot, r):
        tok = rt_ref[tile * MOE_TILE + r]
        return pltpu.make_async_copy(x_hbm.at[pl.ds(tok, 1), :], xbuf.at[slot, pl.ds(r, 1), :], sem.at[slot])

    def start_tile(tile, slot):
        def body(r, c):
            row_copy(tile, slot, r).start()
            return c
        lax.fori_loop(0, MOE_TILE, body, 0, unroll=8)

    def wait_tile(tile, slot):
        def body(r, c):
            row_copy(tile, slot, r).wait()
            return c
        lax.fori_loop(0, MOE_TILE, body, 0, unroll=8)

    @pl.when(t == 0)
    def _():
        for t0 in range(MOE_GATHER_DEPTH - 1):
            @pl.when(t0 < n)
            def _(t0=t0):
                start_tile(t0, t0)

    ahead = t + (MOE_GATHER_DEPTH - 1)

    @pl.when(ahead < n)
    def _():
        start_tile(ahead, ahead % MOE_GATHER_DEPTH)

    @pl.when(t < n)
    def _():
        slot = t % MOE_GATHER_DEPTH
        wait_tile(t, slot)

        @pl.when((t == 0) | (te_ref[t] != te_ref[jnp.maximum(t - 1, 0)]))
        def _():
            w1b[...] = w1_ref[...].astype(BF16)
            w3b[...] = w3_ref[...].astype(BF16)
            w2b[...] = w2_ref[...].astype(BF16)

        x = xbuf[slot].astype(BF16)
        a = _dot(x, w1b[...])
        hid = (a * jax.nn.sigmoid(a)) * _dot(x, w3b[...])
        y_ref[...] = _dot(hid.astype(BF16), w2b[...])

    @pl.when(t >= n)
    def _():
        y_ref[...] = jnp.zeros_like(y_ref)


def _moe_sparse(xn, tile_expert, row_token, n_tiles, w1, w3, w2):
    _, d = xn.shape
    _, _, hdim = w1.shape
    nt_max = tile_expert.shape[0]
    wsp = lambda r, c: pl.BlockSpec((None, r, c), lambda t, te, rt, nt: (te[t], 0, 0))
    return pl.pallas_call(
        _moe_sparse_kernel,
        grid_spec=pltpu.PrefetchScalarGridSpec(
            num_scalar_prefetch=3,
            grid=(nt_max,),
            in_specs=[pl.BlockSpec(memory_space=pl.ANY), wsp(d, hdim), wsp(d, hdim), wsp(hdim, d)],
            out_specs=pl.BlockSpec((MOE_TILE, d), lambda t, te, rt, nt: (t, 0)),
            scratch_shapes=[pltpu.VMEM((MOE_GATHER_DEPTH, MOE_TILE, d), F32), pltpu.VMEM((d, hdim), BF16),
                            pltpu.VMEM((d, hdim), BF16), pltpu.VMEM((hdim, d), BF16),
                            pltpu.SemaphoreType.DMA((MOE_GATHER_DEPTH,))]),
        out_shape=jax.ShapeDtypeStruct((nt_max * MOE_TILE, d), F32),
        compiler_params=_cparams(("arbitrary",)),
        name="moe_sparse",
    )(tile_expert, row_token, n_tiles, xn, w1, w3, w2)


def _tail2_kernel(pos_ref, h1_ref, rw_ref, y_hbm, p_ref, gp_ref, wg_ref, bg_ref, wp_ref, gfin_ref, o_ref, ybuf, sem,
                  *, tm, tok_off):
    i = pl.program_id(0)

    def row_copy(step, buf, r, k):
        base = (tok_off + step * tm) * 2
        return pltpu.make_async_copy(y_hbm.at[pl.ds(pos_ref[base + 2 * r + k], 1), :],
                                     ybuf.at[buf, k, pl.ds(r, 1), :], sem.at[buf])

    def each_row(step, buf, fn):
        def body(r, c):
            fn(row_copy(step, buf, r, 0), 0)
            fn(row_copy(step, buf, r, 1), 1)
            return c
        lax.fori_loop(0, tm, body, 0, unroll=8)

    def start(cp, k):
        cp.start(priority=k)

    @pl.when(i == 0)
    def _():
        each_row(0, 0, start)

    @pl.when(i + 1 < pl.num_programs(0))
    def _():
        each_row(i + 1, (i + 1) % 2, start)

    emb = _dot(p_ref[...].astype(BF16), wp_ref[...])
    buf = i % 2
    each_row(i, buf, lambda cp, k: cp.wait())
    rw = rw_ref[...]
    h = h1_ref[...] + (rw[:, 0:1] * ybuf[buf, 0] + rw[:, 1:2] * ybuf[buf, 1])
    gate = jax.nn.sigmoid(_dot(_rms(h, gp_ref[...]).astype(BF16), wg_ref[...]) + bg_ref[...])
    h = h + gate * emb
    o_ref[...] = _rms(h, gfin_ref[...])


def _tail2(pos, tok_off, h1, rw, y, p, g_ple, wg, bg, wp, g_fin, tm_pref=256):
    m, d = h1.shape
    tm = _row_tile(m, tm_pref)
    row = lambda c: pl.BlockSpec((tm, c), lambda i, ps: (i, 0))
    full = lambda a: pl.BlockSpec(a.shape, lambda i, ps: (0,) * a.ndim)
    params = [g_ple, wg, bg, wp, g_fin]
    return pl.pallas_call(
        functools.partial(_tail2_kernel, tm=tm, tok_off=tok_off),
        grid_spec=pltpu.PrefetchScalarGridSpec(
            num_scalar_prefetch=1,
            grid=(m // tm,),
            in_specs=[row(d), row(ROUTER_COLS), pl.BlockSpec(memory_space=pl.ANY), row(p.shape[1])]
            + [full(a) for a in params],
            out_specs=row(d),
            scratch_shapes=[pltpu.VMEM((2, 2, tm, d), F32), pltpu.SemaphoreType.DMA((2,))]),
        out_shape=jax.ShapeDtypeStruct((m, d), F32),
        compiler_params=_cparams(("arbitrary",)),
        name="tail2",
    )(pos, h1, rw, y, p, *params)


PAGES_PER_STEP = 16
CMP_PAGES_PER_STEP = 32
PAGE_ROWS_PER_TOKEN = KV_GROUPS


def _cmp_paged_kernel(pt_ref, *refs, pages):
    kp = refs[:pages]
    vp = refs[pages:2 * pages]
    wk_ref, wv_ref, ok_ref, ov_ref = refs[2 * pages:]
    chunks = kp[0].shape[0] // (PAGE_ROWS_PER_TOKEN * CMP_STRIDE)
    stride = PAGE_ROWS_PER_TOKEN * CMP_STRIDE
    for src, w, dst in ((kp, wk_ref, ok_ref), (vp, wv_ref, ov_ref)):
        for g in range(KV_GROUPS):
            acc = jnp.zeros((pages * chunks, 2 * HEAD_DIM), F32)
            for jj in range(CMP_PAIRS):
                xj = jnp.concatenate(
                    [jnp.concatenate([p[pl.ds(PAGE_ROWS_PER_TOKEN * (2 * jj + r) + g, chunks, stride=stride), :]
                                      for p in src], axis=0) for r in range(2)], axis=1)
                acc = acc + _dot(xj.astype(BF16), w[jj])
            dst[g] = acc


def _page_specs(page_rows, pages):
    return [pl.BlockSpec((page_rows, HEAD_DIM), functools.partial(lambda i, b, s, pt: (pt[b, s * pages + i], 0), i))
            for i in range(pages)]


def _cmp_paged(pool_k, pool_v, page_table, wk, wv, page_size):
    b, n_pages = page_table.shape
    pages = CMP_PAGES_PER_STEP if n_pages % CMP_PAGES_PER_STEP == 0 else n_pages
    page_rows = page_size * PAGE_ROWS_PER_TOKEN
    chunks = page_size // CMP_STRIDE
    wsp = pl.BlockSpec((CMP_PAIRS, 2 * HEAD_DIM, 2 * HEAD_DIM), lambda i, s, pt: (0, 0, 0))
    osp = pl.BlockSpec((None, KV_GROUPS, pages * chunks, 2 * HEAD_DIM), lambda i, s, pt: (i, 0, s, 0))
    osh = jax.ShapeDtypeStruct((b, KV_GROUPS, n_pages * chunks, 2 * HEAD_DIM), F32)
    return pl.pallas_call(
        functools.partial(_cmp_paged_kernel, pages=pages),
        grid_spec=pltpu.PrefetchScalarGridSpec(
            num_scalar_prefetch=1,
            grid=(b, n_pages // pages),
            in_specs=_page_specs(page_rows, pages) * 2 + [wsp, wsp],
            out_specs=[osp, osp]),
        out_shape=[osh, osh],
        compiler_params=_cparams(("parallel", "arbitrary")),
        name="cmp_paged",
    )(page_table, *([pool_k] * pages), *([pool_v] * pages), wk, wv)


def _nsa_sample_kernel(pt_ref, q_ref, ck_ref, cv_ref, a_ref, ksn_ref, vsn_ref, wk_ref, wv_ref, kwn_ref, vwn_ref,
                       gt_ref, *rest, pages, page_size, n_blk, n_cmp, n_pick, past_len):
    kp = rest[:pages]
    vp = rest[pages:2 * pages]
    o_ref, wko_ref, wvo_ref, sel_scr, oc_scr, m_scr, l_scr, acc_scr, q_scr, sel_smem, sel_sem = rest[2 * pages:]
    step = pl.program_id(1)
    last = pl.num_programs(1) - 1
    qpos = past_len
    nbp = sel_scr.shape[1]
    rowi = lax.broadcasted_iota(jnp.int32, (SUBLANES, 1), 0)
    lane = lax.broadcasted_iota(jnp.int32, (1, LANES), 1)
    blk = lax.broadcasted_iota(jnp.int32, (1, nbp), 1)

    def slopes(g):
        sl = jnp.zeros((SUBLANES, 1), F32)
        for h in range(HEADS_PER_GROUP):
            sl = jnp.where(rowi == h, 2.0 ** -(HEADS_PER_GROUP * g + h + 1), sl)
        return sl

    @pl.when(step == 0)
    def _():
        qrow = q_ref[...]
        ncp = ck_ref.shape[1]
        cidx = lax.broadcasted_iota(jnp.int32, (1, ncp), 1)
        dist = jnp.broadcast_to(qpos - (cidx * CMP_STRIDE + (CMP_BLOCK - 1)), (SUBLANES, ncp))
        mask = (dist >= 0) & (cidx < n_cmp)
        cur = qpos // SEL_BLOCK
        valid = blk <= cur
        forced = (blk == 0) | (blk == cur) | (blk == cur - 1)
        for g in range(KV_GROUPS):
            heads = [qrow[:, (HEADS_PER_GROUP * g + h) * HEAD_DIM:(HEADS_PER_GROUP * g + h + 1) * HEAD_DIM]
                     for h in range(HEADS_PER_GROUP)]
            q8 = jnp.concatenate(heads + [jnp.zeros((SUBLANES - HEADS_PER_GROUP, HEAD_DIM), BF16)], axis=0)
            q_scr[g] = q8
            p = _masked_softmax(_dot_nt(q8, ck_ref[g]) - slopes(g) * dist.astype(F32), mask)
            oc_scr[g] = _dot(p.astype(BF16), cv_ref[g])
            p_sum = p[0:1]
            for h in range(1, HEADS_PER_GROUP):
                p_sum = p_sum + p[h:h + 1]
            p_sum = jnp.broadcast_to(p_sum, (SUBLANES, ncp))
            imp = sum(_dot(t, a_ref[...]) for t in _split3(p_sum))
            score = jnp.where(forced & valid, FORCE_SCORE, jnp.where(valid, imp, -FORCE_SCORE))
            score = jnp.where(blk < n_blk, score, -2.0 * FORCE_SCORE)

            cnt = jnp.zeros((SUBLANES, nbp), F32)
            for c in range(-(-n_blk // SUBLANES)):
                j = SUBLANES * c + rowi
                sj = jnp.max(jnp.where(blk == j, score, -jnp.inf), axis=-1, keepdims=True)
                before = ((sj > score) | ((sj == score) & (j < blk))) & (j < n_blk)
                cnt = cnt + jnp.where(before, 1.0, 0.0)
            rank = jnp.broadcast_to(jnp.sum(cnt, axis=0, keepdims=True), (SUBLANES, nbp))
            sel_scr[g * SUBLANES:(g + 1) * SUBLANES, :] = jnp.where((rank < n_pick) & (blk < n_blk), 1, 0)
            m_scr[g] = jnp.full((SUBLANES, LANES), NEG_INF, F32)
            l_scr[g] = jnp.zeros((SUBLANES, LANES), F32)
            acc_scr[g] = jnp.zeros((SUBLANES, HEAD_DIM), F32)
        flags = pltpu.make_async_copy(sel_scr, sel_smem, sel_sem.at[0])
        flags.start()
        flags.wait()

    blocks_per_page = page_size // SEL_BLOCK
    for g in range(KV_GROUPS):
        for i in range(pages):
            pg = step * pages + i
            first_blk = blocks_per_page * pg
            flags = [sel_smem[g * SUBLANES, first_blk + r] for r in range(blocks_per_page)]

            @pl.when(sum(flags) > 0)
            def _(g=g, i=i, pg=pg, flags=flags):
                k = kp[i][pl.ds(g, page_size, stride=PAGE_ROWS_PER_TOKEN), :].astype(BF16)
                v = vp[i][pl.ds(g, page_size, stride=PAGE_ROWS_PER_TOKEN), :].astype(BF16)
                dist = qpos - (pg * page_size + lane)
                picked = jnp.zeros((1, LANES), jnp.int32)
                for r in range(blocks_per_page):
                    picked = jnp.where(lane // SEL_BLOCK == r, flags[r], picked)
                mask = jnp.broadcast_to((dist >= 0) & (picked > 0), (SUBLANES, LANES))
                s = jnp.where(mask, _dot_nt(q_scr[g], k) - slopes(g) * dist.astype(F32), NEG_INF)
                m_old = m_scr[g][:, 0:1]
                m_new = jnp.maximum(m_old, jnp.max(s, axis=-1, keepdims=True))
                p = jnp.where(mask, jnp.exp(s - m_new), 0.0)
                alpha = jnp.exp(m_old - m_new)
                acc_scr[g] = alpha * acc_scr[g] + _dot(p.astype(BF16), v)
                l_scr[g] = jnp.broadcast_to(alpha * l_scr[g][:, 0:1] + jnp.sum(p, axis=-1, keepdims=True),
                                            (SUBLANES, LANES))
                m_scr[g] = jnp.broadcast_to(m_new, (SUBLANES, LANES))

    @pl.when(step == last)
    def _():
        wb = wk_ref.shape[0]
        wrow = lax.broadcasted_iota(jnp.int32, (wb, 1), 0)
        wk_new = jnp.where(wrow == wb - 1, kwn_ref[...], pltpu.roll(wk_ref[...], wb - 1, 0))
        wv_new = jnp.where(wrow == wb - 1, vwn_ref[...], pltpu.roll(wv_ref[...], wb - 1, 0))
        wko_ref[...] = wk_new
        wvo_ref[...] = wv_new
        widx = lax.broadcasted_iota(jnp.int32, (1, wb), 1)
        wdist = jnp.broadcast_to(wb - 1 - widx, (SUBLANES, wb))
        wmask = (wdist >= 0) & (wdist < WINDOW) & (qpos - wdist >= 0)
        for g in range(KV_GROUPS):
            gs = slice(g * HEAD_DIM, (g + 1) * HEAD_DIM)
            q8 = q_scr[g]
            sl = slopes(g)
            kn = ksn_ref[:, gs].astype(BF16).astype(F32)
            vn = vsn_ref[:, gs].astype(BF16).astype(F32)
            s_n = jnp.sum(q8.astype(F32) * kn, axis=-1, keepdims=True)
            sel_n = sel_smem[g * SUBLANES, n_blk - 1] > 0
            m_old = m_scr[g][:, 0:1]
            m_new = jnp.maximum(m_old, jnp.where(sel_n, s_n, NEG_INF))
            p_n = jnp.where(sel_n, jnp.exp(s_n - m_new), 0.0)
            alpha = jnp.exp(m_old - m_new)
            acc = alpha * acc_scr[g] + p_n.astype(BF16).astype(F32) * vn
            l = alpha * l_scr[g][:, 0:1] + p_n
            o_s = acc / jnp.maximum(l, TINY)
            p = _masked_softmax(_dot_nt(q8, wk_new[:, gs].astype(BF16)) - sl * wdist.astype(F32), wmask)
            o_w = _dot(p.astype(BF16), wv_new[:, gs].astype(BF16))
            gt = jnp.broadcast_to(gt_ref[:, g * LANES:(g + 1) * LANES], (SUBLANES, LANES))
            gcol = [jnp.sum(jnp.where(lane == k * HEADS_PER_GROUP + rowi, gt, 0.0), axis=-1, keepdims=True)
                    for k in range(3)]
            o = gcol[0] * oc_scr[g] + gcol[1] * o_s + gcol[2] * o_w
            for h in range(HEADS_PER_GROUP):
                hd = HEADS_PER_GROUP * g + h
                o_ref[:, hd * HEAD_DIM:(hd + 1) * HEAD_DIM] = o[h:h + 1]


def _nsa_sample(qbf, ck, cv, ks_new, vs_new, win_k, win_v, kw_new, vw_new, gates, pool_k, pool_v, page_table,
                page_size):
    b, n_pages = page_table.shape
    past_len = n_pages * page_size
    ncp = ck.shape[2]
    n_cmp = (past_len + 1 + CMP_STRIDE - 1) // CMP_STRIDE - 1
    n_blk = (past_len + 1 + SEL_BLOCK - 1) // SEL_BLOCK
    nbp = -(-n_blk // LANES) * LANES
    wb = win_k.shape[1]
    assert wb == WINDOW and n_blk >= TOP_N and ncp == n_cmp and page_size % SEL_BLOCK == 0
    pages = PAGES_PER_STEP if n_pages % PAGES_PER_STEP == 0 else n_pages
    page_rows = page_size * PAGE_ROWS_PER_TOKEN
    a = _imp_matrix(ncp, nbp)
    tok = lambda c: pl.BlockSpec((None, 1, c), lambda i, s, pt: (i, 0, 0))
    csp = pl.BlockSpec((None, KV_GROUPS, ncp, HEAD_DIM), lambda i, s, pt: (i, 0, 0, 0))
    wsp = pl.BlockSpec((None, wb, KV_WIDTH), lambda i, s, pt: (i, 0, 0))
    asp = pl.BlockSpec(a.shape, lambda i, s, pt: (0, 0))
    grp_scr = lambda c, dt: pltpu.VMEM((KV_GROUPS, SUBLANES, c), dt)
    return pl.pallas_call(
        functools.partial(_nsa_sample_kernel, pages=pages, page_size=page_size, n_blk=n_blk, n_cmp=n_cmp,
                          n_pick=TOP_N, past_len=past_len),
        grid_spec=pltpu.PrefetchScalarGridSpec(
            num_scalar_prefetch=1,
            grid=(b, n_pages // pages),
            in_specs=[tok(NSA_WIDTH), csp, csp, asp, tok(KV_WIDTH), tok(KV_WIDTH), wsp, wsp, tok(KV_WIDTH),
                      tok(KV_WIDTH), tok(GATE_COLS)] + _page_specs(page_rows, pages) * 2,
            out_specs=[tok(NSA_WIDTH), wsp, wsp],
            scratch_shapes=[pltpu.VMEM((KV_GROUPS * SUBLANES, nbp), jnp.int32), grp_scr(HEAD_DIM, F32),
                            grp_scr(LANES, F32), grp_scr(LANES, F32), grp_scr(HEAD_DIM, F32), grp_scr(HEAD_DIM, BF16),
                            pltpu.SMEM((KV_GROUPS * SUBLANES, nbp), jnp.int32), pltpu.SemaphoreType.DMA((1,))]),
        out_shape=[jax.ShapeDtypeStruct((b, 1, NSA_WIDTH), F32), jax.ShapeDtypeStruct((b, wb, KV_WIDTH), F32),
                   jax.ShapeDtypeStruct((b, wb, KV_WIDTH), F32)],
        compiler_params=_cparams(("parallel", "arbitrary")),
        name="nsa_sample",
    )(page_table, qbf, ck, cv, a, ks_new, vs_new, win_k, win_v, kw_new, vw_new, gates,
      *([pool_k] * pages), *([pool_v] * pages))


def kernel(x_prompt, x_sample, p_prompt, p_sample, cache_cmp_k, cache_cmp_v, cache_sel_k, cache_sel_v, cache_win_k, cache_win_v, state_ssm_re, state_ssm_im, page_table, norm_mix, w_in, cmp_pe_k, cmp_w1_k, cmp_w2_k, cmp_pe_v, cmp_w1_v, cmp_w2_v, ssm_a_re, ssm_a_im, ssm_log_dt, ssm_b_re, ssm_b_im, ssm_c_re, ssm_c_im, ssm_d, w_glu, b_glu, norm_nsa_out, norm_ssm_out, w_out, norm_ffn, router_grp_w, router_grp_b, router_exp_w, router_exp_b, exp_w1, exp_w3, exp_w2, norm_ple, w_ple_gate, b_ple_gate, w_ple_proj, norm_final):
    depth = w_in.shape[0]
    assert depth == 1, "one layer per step"
    i = 0
    b, s, d = x_prompt.shape
    db, t, _ = x_sample.shape
    assert t == 1, "the sample group decodes one token per sequence"
    ssm_ch = w_glu.shape[1]
    page_size = cache_cmp_k.shape[2]
    n_phys = cache_cmp_k.shape[1]
    wp = min(WINDOW, s)
    row1 = lambda a: a.reshape(1, -1)

    w_inp = _prep_w_in(w_in[i], ssm_ch)
    wk1 = _prep_cmp_w1(cmp_w1_k[i])
    wv1 = _prep_cmp_w1(cmp_w1_v[i])
    s5p = _prep_s5(ssm_a_re[i], ssm_a_im[i], ssm_log_dt[i], ssm_b_re[i], ssm_b_im[i], ssm_c_re[i], ssm_c_im[i])
    router = _prep_router(router_grp_w[i], router_grp_b[i], router_exp_w[i], router_exp_b[i])
    wglu = w_glu[i].astype(BF16)
    wout = w_out[i].astype(BF16)
    wpg = w_ple_gate[i].astype(BF16)
    wpp = w_ple_proj[i].astype(BF16)
    cmp_par = (cmp_pe_k[i], cmp_pe_v[i], cmp_w1_k[i], cmp_w1_v[i], cmp_w2_k[i], cmp_w2_v[i])

    def tail1(h, o_nsa, o_ssm):
        return _tail1(o_nsa, o_ssm, h, row1(norm_nsa_out[i]), row1(norm_ssm_out[i]), wout, row1(norm_ffn[i]), router)

    def tail2(pos, tok_off, h1, rw, y, p):
        return _tail2(pos, tok_off, h1, rw, y, p, row1(norm_ple[i]), wpg, row1(b_ple_gate[i]), wpp, row1(norm_final))

    m = b * s
    hp = x_prompt.reshape(m, d)
    qbf, kc, vc, ks, vs, kw, vw, ksb, vsb, kwb, vwb, u, gates = _inproj(hp, row1(norm_mix[i]), w_inp, ssm_ch, 512)
    seq = lambda a: a.reshape(b, s, a.shape[-1])
    lk, lv = _cmp_prompt(seq(kc), seq(vc), wk1, wv1)
    ck, cv = _cmp_finish(lk, lv, *cmp_par)
    o_nsa = _nsa_prompt(seq(qbf), ck, cv, seq(ksb), seq(vsb), seq(kwb), seq(vwb), seq(gates))
    ns = s5p[0].shape[1]
    zero = jnp.zeros((b, 1, ns), F32)
    o_ssm, hre_p, him_p = _s5(seq(u), zero, zero, s5p, row1(ssm_d[i]), wglu, row1(b_glu[i]), scan=True, precise=False)
    h1_p, xn_p, rid_p, rw_p = tail1(hp, o_nsa.reshape(m, -1), o_ssm.reshape(m, -1))
    kv5 = lambda a: a.reshape(1, b, s, KV_GROUPS, HEAD_DIM)
    win5 = lambda a: a.reshape(b, s, KV_GROUPS, HEAD_DIM)[None, :, s - wp:]
    st4 = lambda a, n: a.reshape(1, n, ns // SSM_STATE, SSM_STATE)
    prompt_state = (kv5(kc), kv5(vc), kv5(ks), kv5(vs), win5(kw), win5(vw), st4(hre_p, b), st4(him_p, b))

    hs = x_sample.reshape(db, d)
    qbf, kc, vc, ks, vs, kw, vw, _, _, _, _, u, gates = _inproj(hs, row1(norm_mix[i]), w_inp, ssm_ch, 512)
    pool = lambda c: c[i].reshape(n_phys * page_size * KV_GROUPS, HEAD_DIM)
    one = lambda a: a.reshape(db, 1, a.shape[-1])
    lk, lv = _cmp_paged(pool(cache_cmp_k), pool(cache_cmp_v), page_table, wk1, wv1, page_size)
    ck, cv = _cmp_finish(lk, lv, *cmp_par, kn=one(kc), vn=one(vc))
    wb = cache_win_k.shape[2]
    o_nsa, win_k, win_v = _nsa_sample(one(qbf), ck, cv, one(ks), one(vs), cache_win_k[i].reshape(db, wb, KV_WIDTH),
                                      cache_win_v[i].reshape(db, wb, KV_WIDTH), one(kw), one(vw), one(gates),
                                      pool(cache_sel_k), pool(cache_sel_v), page_table, page_size)
    o_ssm, hre_s, him_s = _s5(u, state_ssm_re[i].reshape(db, ns), state_ssm_im[i].reshape(db, ns), s5p,
                              row1(ssm_d[i]), wglu, row1(b_glu[i]), scan=False, precise=True)
    h1_s, xn_s, rid_s, rw_s = tail1(hs, o_nsa.reshape(db, -1), o_ssm)

    rid = jnp.concatenate([rid_p[:, :2], rid_s[:, :2]], axis=0)
    pos, tile_expert, row_token, n_tiles = _moe_dispatch(rid)
    y = _moe_sparse(jnp.concatenate([xn_p, xn_s], axis=0), tile_expert, row_token, n_tiles,
                    exp_w1[i], exp_w3[i], exp_w2[i])
    y_prompt = tail2(pos, 0, h1_p, rw_p, y, p_prompt[i].reshape(m, -1)).reshape(b, s, d)
    y_sample = tail2(pos, m, h1_s, rw_s, y, p_sample[i].reshape(db, -1)).reshape(db, 1, d)
    new5 = lambda a: a.reshape(1, db, 1, KV_GROUPS, HEAD_DIM)
    buf5 = lambda a: a.reshape(1, db, wb, KV_GROUPS, HEAD_DIM)
    sample_state = (new5(kc), new5(vc), new5(ks), new5(vs), buf5(win_k), buf5(win_v), st4(hre_s, db), st4(him_s, db))
    return (y_prompt, y_sample) + prompt_state + sample_state
```
